```python
import jax
import jax.numpy as jnp
from jax import lax
import numpy as np

D_MODEL = 2048
BATCH = 4
SEQ = 4096
DEPTH = 1

MEM_LEN = 256
EPS = 1e-6

GLA_HEADS = 4
GLA_DK = D_MODEL // (2 * GLA_HEADS)
GLA_DV = D_MODEL // GLA_HEADS
GLA_RANK = 16
GLA_GATE_NORM = 16.0
GLA_CHUNK = 64
GLA_QK = GLA_HEADS * GLA_DK
GLA_V = GLA_HEADS * GLA_DV

FOX_HEADS = 16
FOX_DH = D_MODEL // FOX_HEADS
FOX_W = FOX_HEADS * FOX_DH
FOX_BLOCK = 128

MEM_HEADS = 4
MEM_DH = 128
MEM_W = MEM_HEADS * MEM_DH

N_EXPERTS = 32
TOP_K = 4
D_FF = D_MODEL
SWIGLU_LIMIT = 7.0
SWIGLU_ALPHA = 1.702
MOE_BLOCK = 512

IN_SPLITS = (GLA_QK, GLA_QK, GLA_V, GLA_V, GLA_RANK, FOX_W, FOX_W, FOX_W, FOX_HEADS, D_MODEL, D_MODEL)
IN_COLS = sum(IN_SPLITS)

kernel_name = 'hybrid_gla_fox_memory_moe_block'


def rms_norm(x, g):
    xf = x.astype(jnp.float32)
    y = xf * lax.rsqrt(jnp.mean(xf * xf, axis=-1, keepdims=True) + EPS)
    return (y * g.astype(jnp.float32)).astype(x.dtype)


def split_points(sizes):
    pts, acc = [], 0
    for s in sizes[:-1]:
        acc += s
        pts.append(acc)
    return pts


def to_heads(t, n):
    b, s, w = t.shape
    return t.reshape(b, s, n, w // n).transpose(0, 2, 1, 3)


def from_heads(t):
    b, n, s, d = t.shape
    return t.transpose(0, 2, 1, 3).reshape(b, s, n * d)


def gla_chunked(q, k, v, log_a):
    B, H, S, DK = q.shape
    DV = v.shape[-1]
    C = GLA_CHUNK
    NC = S // C

    def to_chunks(t):
        return jnp.moveaxis(t.reshape(B, H, NC, C, t.shape[-1]), 2, 0)

    xs = tuple(to_chunks(t) for t in (q, k, v, log_a))
    causal = jnp.tril(jnp.ones((C, C), dtype=bool))

    def step(state, inp):
        qi, ki, vi, gi = (t.astype(jnp.float32) for t in inp)
        b = jnp.cumsum(gi, axis=-2)
        o_inter = jnp.einsum('bhtk,bhkv->bhtv', qi * jnp.exp(b), state)
        rel = b[:, :, :, None, :] - b[:, :, None, :, :]
        decay = jnp.exp(jnp.where(causal[:, :, None], rel, -jnp.inf))
        scores = jnp.einsum('bhtk,bhsk,bhtsk->bhts', qi, ki, decay)
        o_intra = jnp.einsum('bhts,bhsv->bhtv', scores, vi)
        b_last = b[:, :, -1:, :]
        new_state = jnp.exp(b_last[:, :, 0, :])[..., None] * state + jnp.einsum(
            'bhsk,bhsv->bhkv', ki * jnp.exp(b_last - b), vi)
        return new_state, o_inter + o_intra

    state0 = jnp.zeros((B, H, DK, DV), jnp.float32)
    _, o = lax.scan(step, state0, xs)
    return jnp.moveaxis(o, 0, 2).reshape(B, H, S, DV).astype(v.dtype)


def forgetting_attention(q, k, v, log_f):
    B, H, S, D = q.shape
    c = jnp.cumsum(log_f, axis=-1)
    scale = D ** -0.5
    outs = []
    for i in range(S // FOX_BLOCK):
        lo, hi = i * FOX_BLOCK, (i + 1) * FOX_BLOCK
        logits = jnp.einsum('bhqd,bhkd->bhqk', q[:, :, lo:hi], k[:, :, :hi]).astype(jnp.float32) * scale
        logits = logits + (c[:, :, lo:hi, None] - c[:, :, None, :hi])
        mask = (lo + jnp.arange(FOX_BLOCK))[:, None] >= jnp.arange(hi)[None, :]
        p = jax.nn.softmax(jnp.where(mask, logits, -jnp.inf), axis=-1)
        outs.append(jnp.einsum('bhqk,bhkd->bhqd', p.astype(v.dtype), v[:, :, :hi]))
    return jnp.concatenate(outs, axis=2)


def mixer_block(x, mix_norm_g, w_in, gla_alpha_up, gla_alpha_bias, gla_out_norm_g,
                fox_f_bias, fox_q_norm_g, fox_k_norm_g, w_branch_gla, w_branch_fox, w_out):
    h = rms_norm(x, mix_norm_g)
    proj = h @ w_in
    (gq, gk, gv, gr, ga, fq, fk, fv, ff, gate_gla, gate_fox) = jnp.split(
        proj, split_points(IN_SPLITS), axis=-1)

    q = to_heads(gq, GLA_HEADS) * (GLA_DK ** -0.5)
    k = to_heads(gk, GLA_HEADS)
    v = to_heads(gv, GLA_HEADS)
    log_a = jax.nn.log_sigmoid((ga @ gla_alpha_up + gla_alpha_bias).astype(jnp.float32)) / GLA_GATE_NORM
    o = gla_chunked(q, k, v, to_heads(log_a, GLA_HEADS))
    o = from_heads(rms_norm(o, gla_out_norm_g)) * jax.nn.silu(gr)
    y_gla = o @ w_branch_gla

    fq_h = rms_norm(to_heads(fq, FOX_HEADS), fox_q_norm_g)
    fk_h = rms_norm(to_heads(fk, FOX_HEADS), fox_k_norm_g)
    fv_h = to_heads(fv, FOX_HEADS)
    log_f = jax.nn.log_sigmoid((ff + fox_f_bias).astype(jnp.float32)).transpose(0, 2, 1)
    y_fox = from_heads(forgetting_attention(fq_h, fk_h, fv_h, log_f)) @ w_branch_fox

    merged = jax.nn.sigmoid(gate_gla) * y_gla + jax.nn.sigmoid(gate_fox) * y_fox
    return x + merged @ w_out


def memory_block(x, mem, mem_norm_g, mem_kv_norm_g, mem_w_q, mem_w_kv,
                 mem_q_norm_g, mem_k_norm_g, mem_w_o):
    h = rms_norm(x, mem_norm_g)
    mn = rms_norm(mem, mem_kv_norm_g)
    q = rms_norm(to_heads(h @ mem_w_q, MEM_HEADS), mem_q_norm_g)
    kv = mn @ mem_w_kv
    k = rms_norm(to_heads(kv[..., :MEM_W], MEM_HEADS), mem_k_norm_g)
    v = to_heads(kv[..., MEM_W:], MEM_HEADS)
    logits = jnp.einsum('bhqd,bhkd->bhqk', q, k).astype(jnp.float32) * (MEM_DH ** -0.5)
    p = jax.nn.softmax(logits, axis=-1)
    o = jnp.einsum('bhqk,bhkd->bhqd', p.astype(v.dtype), v)
    return x + from_heads(o) @ mem_w_o


def moe_block(x, moe_norm_g, w_router, b_router, w_expert_up, b_expert_up,
              w_expert_down, b_expert_down):
    B, S, D = x.shape
    T = B * S
    A = T * TOP_K
    xs = rms_norm(x, moe_norm_g).reshape(T, D)
    logits = (xs @ w_router).astype(jnp.float32) + b_router.astype(jnp.float32)
    top_vals, top_idx = lax.top_k(logits, TOP_K)
    gates = jax.nn.softmax(top_vals, axis=-1)

    flat_e = top_idx.reshape(A)
    order = jnp.argsort(flat_e)
    sorted_e = flat_e[order]
    tok = (order // TOP_K).astype(jnp.int32)
    w_assign = gates.reshape(A)[order]
    counts = jnp.bincount(flat_e, length=N_EXPERTS)
    padded = (counts + MOE_BLOCK - 1) // MOE_BLOCK * MOE_BLOCK
    pend = jnp.cumsum(padded)
    pstart = pend - padded
    ustart = jnp.cumsum(counts) - counts
    slot = pstart[sorted_e] + (jnp.arange(A) - ustart[sorted_e])
    NB = (A + MOE_BLOCK - 1) // MOE_BLOCK + N_EXPERTS
    P = NB * MOE_BLOCK
    slot_tok = jnp.zeros((P,), jnp.int32).at[slot].set(tok)
    slot_w = jnp.zeros((P,), jnp.float32).at[slot].set(w_assign)
    block_e = jnp.minimum(jnp.searchsorted(pend, jnp.arange(NB) * MOE_BLOCK, side='right'), N_EXPERTS - 1)

    def expert_rows(args):
        idx, wts, e = args
        xb = xs[idx]
        gu = xb @ w_expert_up[e] + b_expert_up[e]
        glu = jnp.minimum(gu[:, :D_FF], SWIGLU_LIMIT)
        lin = jnp.clip(gu[:, D_FF:], -SWIGLU_LIMIT, SWIGLU_LIMIT)
        act = glu * jax.nn.sigmoid(SWIGLU_ALPHA * glu) * (lin + 1.0)
        y = act @ w_expert_down[e] + b_expert_down[e]
        return y * wts.astype(y.dtype)[:, None]

    y_blocks = lax.map(expert_rows, (slot_tok.reshape(NB, MOE_BLOCK), slot_w.reshape(NB, MOE_BLOCK), block_e))
    out = jax.ops.segment_sum(y_blocks.reshape(P, D), slot_tok, num_segments=T)
    return x + out.reshape(B, S, D).astype(x.dtype)


def setup_inputs(seed: int = 0) -> dict:
    key = jax.random.key(seed)
    ks = jax.random.split(key, 32)
    L = DEPTH

    def dense(k, shape, fan_in):
        return jax.random.normal(k, shape, jnp.float32) * (fan_in ** -0.5)

    def gain(k, shape):
        return 1.0 + 0.02 * jax.random.normal(k, shape, jnp.float32)

    def small(k, shape, s):
        return s * jax.random.normal(k, shape, jnp.float32)

    return {
        'x': jax.random.normal(ks[0], (BATCH, SEQ, D_MODEL), jnp.float32),
        'mem': jax.random.normal(ks[1], (BATCH, MEM_LEN, D_MODEL), jnp.float32),
        'mix_norm_g': gain(ks[2], (L, D_MODEL)),
        'w_in': dense(ks[3], (L, D_MODEL, IN_COLS), D_MODEL),
        'gla_alpha_up': dense(ks[4], (L, GLA_RANK, GLA_QK), GLA_RANK),
        'gla_alpha_bias': small(ks[5], (L, GLA_QK), 0.1),
        'gla_out_norm_g': gain(ks[6], (L, GLA_DV)),
        'fox_f_bias': 2.0 + small(ks[7], (L, FOX_HEADS), 0.5),
        'fox_q_norm_g': gain(ks[8], (L, FOX_DH)),
        'fox_k_norm_g': gain(ks[9], (L, FOX_DH)),
        'w_branch_gla': dense(ks[10], (L, GLA_V, D_MODEL), GLA_V),
        'w_branch_fox': dense(ks[11], (L, FOX_W, D_MODEL), FOX_W),
        'w_out': dense(ks[12], (L, D_MODEL, D_MODEL), D_MODEL),
        'mem_norm_g': gain(ks[13], (L, D_MODEL)),
        'mem_kv_norm_g': gain(ks[14], (L, D_MODEL)),
        'mem_w_q': dense(ks[15], (L, D_MODEL, MEM_W), D_MODEL),
        'mem_w_kv': dense(ks[16], (L, D_MODEL, 2 * MEM_W), D_MODEL),
        'mem_q_norm_g': gain(ks[17], (L, MEM_DH)),
        'mem_k_norm_g': gain(ks[18], (L, MEM_DH)),
        'mem_w_o': dense(ks[19], (L, MEM_W, D_MODEL), MEM_W),
        'moe_norm_g': gain(ks[20], (L, D_MODEL)),
        'w_router': dense(ks[21], (L, D_MODEL, N_EXPERTS), D_MODEL),
        'b_router': small(ks[22], (L, N_EXPERTS), 0.01),
        'w_expert_up': dense(ks[23], (L, N_EXPERTS, D_MODEL, 2 * D_FF), D_MODEL),
        'b_expert_up': small(ks[24], (L, N_EXPERTS, 2 * D_FF), 0.02),
        'w_expert_down': dense(ks[25], (L, N_EXPERTS, D_FF, D_MODEL), D_FF),
        'b_expert_down': small(ks[26], (L, N_EXPERTS, D_MODEL), 0.02),
    }


def reference(x, mem, mix_norm_g, w_in, gla_alpha_up, gla_alpha_bias, gla_out_norm_g,
              fox_f_bias, fox_q_norm_g, fox_k_norm_g, w_branch_gla, w_branch_fox, w_out,
              mem_norm_g, mem_kv_norm_g, mem_w_q, mem_w_kv, mem_q_norm_g, mem_k_norm_g, mem_w_o,
              moe_norm_g, w_router, b_router, w_expert_up, b_expert_up, w_expert_down, b_expert_down):
    for l in range(DEPTH):
        x = mixer_block(x, mix_norm_g[l], w_in[l], gla_alpha_up[l], gla_alpha_bias[l], gla_out_norm_g[l],
                        fox_f_bias[l], fox_q_norm_g[l], fox_k_norm_g[l], w_branch_gla[l], w_branch_fox[l], w_out[l])
        x = memory_block(x, mem, mem_norm_g[l], mem_kv_norm_g[l], mem_w_q[l], mem_w_kv[l],
                         mem_q_norm_g[l], mem_k_norm_g[l], mem_w_o[l])
        x = moe_block(x, moe_norm_g[l], w_router[l], b_router[l], w_expert_up[l], b_expert_up[l],
                      w_expert_down[l], b_expert_down[l])
    return x
```

```python
import functools

import jax
import jax.numpy as jnp
from jax import lax
from jax.experimental import pallas as pl
from jax.experimental.pallas import tpu as pltpu

F32 = jnp.float32
BF16 = jnp.bfloat16

EPS = 1e-6
LANES = 128
VMEM_LIMIT = 56 * 1024 * 1024

GLA_HEADS = 4
GLA_GATE_NORM = 16.0
GLA_CHUNK = 64
FOX_HEADS = 16
MEM_HEADS = 4
TOP_K = 4
SWIGLU_LIMIT = 7.0
SWIGLU_ALPHA = 1.702

NT_DIMS = (((1,), (1,)), ((), ()))
TN_DIMS = (((0,), (0,)), ((), ()))


def _params(sem, vmem=VMEM_LIMIT):
    return pltpu.CompilerParams(dimension_semantics=sem, vmem_limit_bytes=vmem)


def _rms(xf, g):
    return xf * lax.rsqrt(jnp.mean(xf * xf, axis=-1, keepdims=True) + EPS) * g


def _log_sigmoid(z):
    return jnp.minimum(z, 0.0) - jnp.log1p(jnp.exp(-jnp.abs(z)))


def _norm_matmul_kernel(*refs, norm_lo, norm_hi, tn, has_small):
    if has_small:
        x_ref, g_ref, w_ref, cs_ref, ws_ref, out_ref, small_ref, h_ref = refs
    else:
        x_ref, g_ref, w_ref, cs_ref, out_ref, h_ref = refs
    j = pl.program_id(1)

    @pl.when(j == 0)
    def _():
        hb = _rms(x_ref[...], g_ref[...]).astype(BF16)
        h_ref[...] = hb
        if has_small:
            small_ref[...] = jnp.dot(hb, ws_ref[...], preferred_element_type=F32)

    acc = jnp.dot(h_ref[...], w_ref[...], preferred_element_type=F32)
    is_norm = jnp.logical_and(j >= norm_lo, j < norm_hi)

    @pl.when(is_norm)
    def _():
        for c in range(tn // LANES):
            sl = slice(c * LANES, (c + 1) * LANES)
            out_ref[:, sl] = _rms(acc[:, sl], cs_ref[:, sl]).astype(out_ref.dtype)

    @pl.when(jnp.logical_not(is_norm))
    def _():
        out_ref[...] = acc.astype(out_ref.dtype)


def _norm_matmul(x, g, w, colscale, w_small, *, tm, tn, norm_lo, norm_hi):
    m, d = x.shape
    n = w.shape[1]
    has_small = w_small is not None
    in_specs = [
        pl.BlockSpec((tm, d), lambda i, j: (i, 0)),
        pl.BlockSpec((1, d), lambda i, j: (0, 0)),
        pl.BlockSpec((d, tn), lambda i, j: (0, j)),
        pl.BlockSpec((1, tn), lambda i, j: (0, j)),
    ]
    args = [x, g, w, colscale]
    out_shape = [jax.ShapeDtypeStruct((m, n), BF16)]
    out_specs = [pl.BlockSpec((tm, tn), lambda i, j: (i, j))]
    if has_small:
        in_specs.append(pl.BlockSpec((d, LANES), lambda i, j: (0, 0)))
        args.append(w_small)
        out_shape.append(jax.ShapeDtypeStruct((m, LANES), F32))
        out_specs.append(pl.BlockSpec((tm, LANES), lambda i, j: (i, 0)))
    res = pl.pallas_call(
        functools.partial(_norm_matmul_kernel, norm_lo=norm_lo, norm_hi=norm_hi, tn=tn,
                          has_small=has_small),
        grid=(m // tm, n // tn),
        in_specs=in_specs,
        out_specs=out_specs,
        out_shape=out_shape,
        scratch_shapes=[pltpu.VMEM((tm, d), BF16)],
        compiler_params=_params(("arbitrary", "arbitrary")),
        name="norm_matmul",
    )(*args)
    return res if has_small else res[0]


def _gla_kernel(q_ref, k_ref, v_ref, r_ref, sm_ref, au_ref, ab_ref, gn_ref, out_ref, st_ref,
                *, chunk, nsub, scale):
    @pl.when(pl.program_id(2) == 0)
    def _():
        st_ref[...] = jnp.zeros_like(st_ref)

    row = lax.broadcasted_iota(jnp.int32, (chunk, chunk), 0)
    col = lax.broadcasted_iota(jnp.int32, (chunk, chunk), 1)
    causal = row >= col
    tri = causal.astype(F32)
    for s in range(nsub):
        rows = pl.ds(s * chunk, chunk)
        z = jnp.dot(sm_ref[rows, :].astype(BF16), au_ref[...], preferred_element_type=F32)
        la = _log_sigmoid(z + ab_ref[...]) * (1.0 / GLA_GATE_NORM)
        b = jnp.dot(tri, la, precision=lax.Precision.HIGHEST, preferred_element_type=F32)
        b_last = b[chunk - 1:chunk, :]
        q = q_ref[rows, :].astype(F32)
        k = k_ref[rows, :].astype(F32)
        v = v_ref[rows, :]
        qd = (q * (scale * jnp.exp(b))).astype(BF16)
        kd = (k * jnp.exp(-b)).astype(BF16)
        kl = (k * jnp.exp(b_last - b)).astype(BF16)
        sc = lax.dot_general(qd, kd, NT_DIMS, preferred_element_type=F32)
        sc = jnp.where(causal, sc, 0.0).astype(BF16)
        st = st_ref[...]
        o = jnp.dot(sc, v, preferred_element_type=F32) + lax.dot_general(
            qd, st.astype(BF16), NT_DIMS, preferred_element_type=F32)
        st_ref[...] = st * jnp.exp(b_last) + lax.dot_general(
            v, kl, TN_DIMS, preferred_element_type=F32)
        r = r_ref[rows, :].astype(F32)
        out_ref[rows, :] = (_rms(o, gn_ref[...]) * (r * jax.nn.sigmoid(r))).astype(out_ref.dtype)


def _gla(proj, small, alpha_up_pad, alpha_bias, out_norm_g, *, batch, seq, dk, dv, offs, rows):
    t = batch * seq
    nblk = seq // rows
    h = GLA_HEADS
    qo, ko, vo, ro = (offs[0] // dk, offs[1] // dk, offs[2] // dv, offs[3] // dv)
    rmap = lambda b, hh, c: b * nblk + c
    return pl.pallas_call(
        functools.partial(_gla_kernel, chunk=GLA_CHUNK, nsub=rows // GLA_CHUNK, scale=dk ** -0.5),
        grid=(batch, h, nblk),
        in_specs=[
            pl.BlockSpec((rows, dk), lambda b, hh, c: (rmap(b, hh, c), qo + hh)),
            pl.BlockSpec((rows, dk), lambda b, hh, c: (rmap(b, hh, c), ko + hh)),
            pl.BlockSpec((rows, dv), lambda b, hh, c: (rmap(b, hh, c), vo + hh)),
            pl.BlockSpec((rows, dv), lambda b, hh, c: (rmap(b, hh, c), ro + hh)),
            pl.BlockSpec((rows, LANES), lambda b, hh, c: (rmap(b, hh, c), 0)),
            pl.BlockSpec((LANES, dk), lambda b, hh, c: (0, hh)),
            pl.BlockSpec((1, dk), lambda b, hh, c: (0, hh)),
            pl.BlockSpec((1, dv), lambda b, hh, c: (0, 0)),
        ],
        out_specs=pl.BlockSpec((rows, dv), lambda b, hh, c: (rmap(b, hh, c), hh)),
        out_shape=jax.ShapeDtypeStruct((t, h * dv), BF16),
        scratch_shapes=[pltpu.VMEM((dv, dk), F32)],
        compiler_params=_params(("arbitrary", "arbitrary", "arbitrary")),
        name="gla",
    )(proj, proj, proj, proj, small, alpha_up_pad, alpha_bias, out_norm_g)


def _fcum_kernel(sm_ref, bias_ref, out_ref, carry_ref, *, tb):
    @pl.when(pl.program_id(1) == 0)
    def _():
        carry_ref[...] = jnp.zeros_like(carry_ref)

    row = lax.broadcasted_iota(jnp.int32, (tb, tb), 0)
    col = lax.broadcasted_iota(jnp.int32, (tb, tb), 1)
    tri = (row >= col).astype(F32)
    lf = _log_sigmoid(sm_ref[...] + bias_ref[...])
    c = jnp.dot(tri, lf, precision=lax.Precision.HIGHEST, preferred_element_type=F32) + carry_ref[...]
    out_ref[...] = c
    carry_ref[...] = c[tb - 1:tb, :]


def _fcum(small, bias_pad, *, batch, seq, tb):
    nb = seq // tb
    return pl.pallas_call(
        functools.partial(_fcum_kernel, tb=tb),
        grid=(batch, nb),
        in_specs=[pl.BlockSpec((tb, LANES), lambda b, i: (b * nb + i, 0)),
                  pl.BlockSpec((1, LANES), lambda b, i: (0, 0))],
        out_specs=pl.BlockSpec((tb, LANES), lambda b, i: (b * nb + i, 0)),
        out_shape=jax.ShapeDtypeStruct(small.shape, F32),
        scratch_shapes=[pltpu.VMEM((1, LANES), F32)],
        compiler_params=_params(("arbitrary", "arbitrary")),
        name="forget_cumsum",
    )(small, bias_pad)


def _fox_kernel(qt_ref, kt_ref, q_ref, k_ref, v_ref, c_ref, out_ref, m_ref, l_ref, acc_ref, *, tq):
    p = pl.program_id(2)
    qi = qt_ref[p]
    ki = kt_ref[p]

    @pl.when(ki == 0)
    def _():
        m_ref[...] = jnp.full_like(m_ref, -jnp.inf)
        l_ref[...] = jnp.zeros_like(l_ref)
        acc_ref[...] = jnp.zeros_like(acc_ref)

    def update(mask_diag):
        s = lax.dot_general(q_ref[...], k_ref[...], NT_DIMS, preferred_element_type=F32) - c_ref[0]
        if mask_diag:
            row = lax.broadcasted_iota(jnp.int32, (tq, tq), 0)
            col = lax.broadcasted_iota(jnp.int32, (tq, tq), 1)
            s = jnp.where(row >= col, s, -jnp.inf)
        m_old = m_ref[...]
        m_new = jnp.maximum(m_old, jnp.max(s, axis=-1, keepdims=True))
        alpha = jnp.exp(m_old - m_new)
        pr = jnp.exp(s - m_new)
        l_ref[...] = alpha * l_ref[...] + jnp.sum(pr, axis=-1, keepdims=True)
        acc_ref[...] = alpha * acc_ref[...] + jnp.dot(pr.astype(BF16), v_ref[...],
                                                      preferred_element_type=F32)
        m_ref[...] = m_new

    @pl.when(ki < qi)
    def _():
        update(False)

    @pl.when(ki == qi)
    def _():
        update(True)
        out_ref[...] = (acc_ref[...] / l_ref[...]).astype(out_ref.dtype)


def _fox(proj, c_row, *, batch, seq, dh, offs, tq):
    t = batch * seq
    nq = seq // tq
    pairs = [(a, b) for a in range(nq) for b in range(a + 1)]
    qt = jnp.asarray([a for a, _ in pairs], jnp.int32)
    kt = jnp.asarray([b for _, b in pairs], jnp.int32)
    qo, ko, vo = (o // dh for o in offs)
    grid_spec = pltpu.PrefetchScalarGridSpec(
        num_scalar_prefetch=2,
        grid=(batch, FOX_HEADS, len(pairs)),
        in_specs=[
            pl.BlockSpec((tq, dh), lambda b, h, p, qt, kt: (b * nq + qt[p], qo + h)),
            pl.BlockSpec((tq, dh), lambda b, h, p, qt, kt: (b * nq + kt[p], ko + h)),
            pl.BlockSpec((tq, dh), lambda b, h, p, qt, kt: (b * nq + kt[p], vo + h)),
            pl.BlockSpec((1, 1, tq), lambda b, h, p, qt, kt: (b * FOX_HEADS + h, 0, kt[p])),
        ],
        out_specs=pl.BlockSpec((tq, dh), lambda b, h, p, qt, kt: (b * nq + qt[p], h)),
        scratch_shapes=[pltpu.VMEM((tq, 1), F32), pltpu.VMEM((tq, 1), F32),
                        pltpu.VMEM((tq, dh), F32)],
    )
    return pl.pallas_call(
        functools.partial(_fox_kernel, tq=tq),
        grid_spec=grid_spec,
        out_shape=jax.ShapeDtypeStruct((t, FOX_HEADS * dh), BF16),
        compiler_params=_params(("arbitrary", "arbitrary", "arbitrary")),
        name="fox_attention",
    )(qt, kt, proj, proj, proj, c_row)


def _merge_kernel(a_ref, f_ref, ga_ref, gf_ref, wa_ref, wf_ref, out_ref):
    ya = jnp.dot(a_ref[...], wa_ref[...], preferred_element_type=F32)
    yf = jnp.dot(f_ref[...], wf_ref[...], preferred_element_type=F32)
    out_ref[...] = (jax.nn.sigmoid(ga_ref[...].astype(F32)) * ya
                    + jax.nn.sigmoid(gf_ref[...].astype(F32)) * yf).astype(out_ref.dtype)


def _merge(o_gla, o_fox, proj, w_gla, w_fox, *, offs, tm, tn):
    m, kdim = o_gla.shape
    n = w_gla.shape[1]
    go, fo = offs[0] // tn, offs[1] // tn
    return pl.pallas_call(
        _merge_kernel,
        grid=(m // tm, n // tn),
        in_specs=[
            pl.BlockSpec((tm, kdim), lambda i, j: (i, 0)),
            pl.BlockSpec((tm, o_fox.shape[1]), lambda i, j: (i, 0)),
            pl.BlockSpec((tm, tn), lambda i, j: (i, go + j)),
            pl.BlockSpec((tm, tn), lambda i, j: (i, fo + j)),
            pl.BlockSpec((kdim, tn), lambda i, j: (0, j)),
            pl.BlockSpec((o_fox.shape[1], tn), lambda i, j: (0, j)),
        ],
        out_specs=pl.BlockSpec((tm, tn), lambda i, j: (i, j)),
        out_shape=jax.ShapeDtypeStruct((m, n), BF16),
        compiler_params=_params(("arbitrary", "arbitrary")),
        name="gated_merge",
    )(o_gla, o_fox, proj, proj, w_gla, w_fox)


def _resid_matmul_kernel(a_ref, w_ref, res_ref, out_ref):
    out_ref[...] = res_ref[...] + jnp.dot(a_ref[...], w_ref[...], preferred_element_type=F32)


def _resid_matmul(a, w, res, *, tm, tn):
    m, kdim = a.shape
    n = w.shape[1]
    return pl.pallas_call(
        _resid_matmul_kernel,
        grid=(m // tm, n // tn),
        in_specs=[pl.BlockSpec((tm, kdim), lambda i, j: (i, 0)),
                  pl.BlockSpec((kdim, tn), lambda i, j: (0, j)),
                  pl.BlockSpec((tm, tn), lambda i, j: (i, j))],
        out_specs=pl.BlockSpec((tm, tn), lambda i, j: (i, j)),
        out_shape=jax.ShapeDtypeStruct((m, n), F32),
        compiler_params=_params(("arbitrary", "arbitrary")),
        name="resid_matmul",
    )(a, w, res)


def _memory_kernel(x_ref, gn_ref, wq_ref, gq_ref, k_ref, v_ref, wo_ref, gm_ref, wr_ref, br_ref,
                   x2_ref, rl_ref, *, dh):
    x1 = x_ref[...]
    hb = _rms(x1, gn_ref[...]).astype(BF16)
    q = jnp.dot(hb, wq_ref[...], preferred_element_type=F32)
    outs = []
    for h in range(MEM_HEADS):
        sl = slice(h * dh, (h + 1) * dh)
        qn = _rms(q[:, sl], gq_ref[...]).astype(BF16)
        s = lax.dot_general(qn, k_ref[:, sl], NT_DIMS, preferred_element_type=F32)
        s = s - jnp.max(s, axis=-1, keepdims=True)
        p = jnp.exp(s)
        p = p / jnp.sum(p, axis=-1, keepdims=True)
        outs.append(jnp.dot(p.astype(BF16), v_ref[:, sl], preferred_element_type=F32).astype(BF16))
    o = jnp.concatenate(outs, axis=-1)
    x2 = x1 + jnp.dot(o, wo_ref[...], preferred_element_type=F32)
    x2_ref[...] = x2
    xs = _rms(x2, gm_ref[...]).astype(BF16)
    rl_ref[...] = jnp.dot(xs, wr_ref[...], preferred_element_type=F32) + br_ref[...]


def _memory(x1, mem_norm_g, wq, gq, kv, wo, moe_norm_g, wr_pad, br_pad, *, batch, seq,
            mem_len, tm):
    t, d = x1.shape
    w = wq.shape[1]
    dh = w // MEM_HEADS
    nblk = seq // tm
    const = lambda i: (0, 0)
    return pl.pallas_call(
        functools.partial(_memory_kernel, dh=dh),
        grid=(t // tm,),
        in_specs=[
            pl.BlockSpec((tm, d), lambda i: (i, 0)),
            pl.BlockSpec((1, d), const),
            pl.BlockSpec((d, w), const),
            pl.BlockSpec((1, dh), const),
            pl.BlockSpec((mem_len, w), lambda i: (i // nblk, 0)),
            pl.BlockSpec((mem_len, w), lambda i: (i // nblk, 1)),
            pl.BlockSpec((w, d), const),
            pl.BlockSpec((1, d), const),
            pl.BlockSpec((d, LANES), const),
            pl.BlockSpec((1, LANES), const),
        ],
        out_specs=[pl.BlockSpec((tm, d), lambda i: (i, 0)),
                   pl.BlockSpec((tm, LANES), lambda i: (i, 0))],
        out_shape=[jax.ShapeDtypeStruct((t, d), F32), jax.ShapeDtypeStruct((t, LANES), F32)],
        compiler_params=_params(("arbitrary",)),
        name="memory_block",
    )(x1, mem_norm_g, wq, gq, kv, kv, wo, moe_norm_g, wr_pad, br_pad)


def _route_kernel(rl_ref, route_ref, counts_ref, carry_ref, *, tr, n_experts):
    @pl.when(pl.program_id(0) == 0)
    def _():
        carry_ref[...] = jnp.zeros_like(carry_ref)

    lane = lax.broadcasted_iota(jnp.int32, (tr, LANES), 1)
    lg = jnp.where(lane < n_experts, rl_ref[...], -jnp.inf)
    vals, hots = [], []
    for _ in range(TOP_K):
        mx = jnp.max(lg, axis=-1, keepdims=True)
        idx = jnp.min(jnp.where(lg == mx, lane, LANES), axis=-1, keepdims=True)
        hot = lane == idx
        vals.append(mx)
        hots.append(hot)
        lg = jnp.where(hot, -jnp.inf, lg)
    exps = [jnp.exp(v - vals[0]) for v in vals]
    denom = exps[0]
    for e in exps[1:]:
        denom = denom + e
    onehot = hots[0]
    for hsel in hots[1:]:
        onehot = jnp.logical_or(onehot, hsel)
    onehot_f = onehot.astype(F32)
    row = lax.broadcasted_iota(jnp.int32, (tr, tr), 0)
    col = lax.broadcasted_iota(jnp.int32, (tr, tr), 1)
    strict = (row > col).astype(BF16)
    before = jnp.dot(strict, onehot_f.astype(BF16), preferred_element_type=F32) + carry_ref[...]
    route = jnp.zeros((tr, LANES), F32)
    for k in range(TOP_K):
        idx_f = jnp.sum(jnp.where(hots[k], lane, 0), axis=-1, keepdims=True).astype(F32)
        rank = jnp.sum(jnp.where(hots[k], before, 0.0), axis=-1, keepdims=True)
        route = jnp.where(lane == k, idx_f, route)
        route = jnp.where(lane == TOP_K + k, exps[k] / denom, route)
        route = jnp.where(lane == 2 * TOP_K + k, rank, route)
    route_ref[...] = route
    total = carry_ref[...] + jnp.sum(onehot_f, axis=0, keepdims=True)
    carry_ref[...] = total
    counts_ref[...] = jnp.broadcast_to(total, counts_ref.shape)


def _route(rlogits, *, n_experts, tr):
    t = rlogits.shape[0]
    return pl.pallas_call(
        functools.partial(_route_kernel, tr=tr, n_experts=n_experts),
        grid=(t // tr,),
        in_specs=[pl.BlockSpec((tr, LANES), lambda i: (i, 0))],
        out_specs=[pl.BlockSpec((tr, LANES), lambda i: (i, 0)),
                   pl.BlockSpec((8, LANES), lambda i: (0, 0))],
        out_shape=[jax.ShapeDtypeStruct((t, LANES), F32), jax.ShapeDtypeStruct((8, LANES), F32)],
        scratch_shapes=[pltpu.VMEM((1, LANES), F32)],
        compiler_params=_params(("arbitrary",)),
        name="route_topk",
    )(rlogits)


def _gather_kernel(tok_ref, x_hbm, g_ref, out_ref, buf, sem, *, rows):
    i = pl.program_id(0)
    n = pl.num_programs(0)

    def issue(blk, slot):
        def body(r, carry):
            tok = tok_ref[blk * rows + r]
            pltpu.make_async_copy(x_hbm.at[pl.ds(tok, 1)], buf.at[slot, pl.ds(r, 1)],
                                  sem.at[slot]).start()
            return carry
        lax.fori_loop(0, rows, body, 0, unroll=8)

    @pl.when(i == 0)
    def _():
        issue(0, 0)

    @pl.when(i + 1 < n)
    def _():
        issue(i + 1, (i + 1) % 2)

    slot = i % 2
    pltpu.make_async_copy(x_hbm.at[pl.ds(0, rows)], buf.at[slot], sem.at[slot]).wait()
    out_ref[...] = _rms(buf[slot], g_ref[...]).astype(out_ref.dtype)


def _gather_norm(slot_tok, x2, g, *, rows):
    p = slot_tok.shape[0]
    d = x2.shape[1]
    grid_spec = pltpu.PrefetchScalarGridSpec(
        num_scalar_prefetch=1,
        grid=(p // rows,),
        in_specs=[pl.BlockSpec(memory_space=pl.ANY),
                  pl.BlockSpec((1, d), lambda i, tok: (0, 0))],
        out_specs=pl.BlockSpec((rows, d), lambda i, tok: (i, 0)),
        scratch_shapes=[pltpu.VMEM((2, rows, d), F32), pltpu.SemaphoreType.DMA((2,))],
    )
    return pl.pallas_call(
        functools.partial(_gather_kernel, rows=rows),
        grid_spec=grid_spec,
        out_shape=jax.ShapeDtypeStruct((p, d), BF16),
        compiler_params=_params(("arbitrary",)),
        name="moe_gather",
    )(slot_tok, x2, g)


def _expert_up_kernel(blk_ref, tile_ref, exp_ref, flag_ref, x_ref, wg_ref, wl_ref, bg_ref, bl_ref,
                      out_ref, wg_bf, wl_bf):
    flag = flag_ref[pl.program_id(0)]

    @pl.when(flag >= 2)
    def _():
        wg_bf[...] = wg_ref[...].astype(BF16)
        wl_bf[...] = wl_ref[...].astype(BF16)

    @pl.when(flag % 2 == 1)
    def _():
        x = x_ref[...]
        glu = jnp.dot(x, wg_bf[...], preferred_element_type=F32) + bg_ref[...]
        lin = jnp.dot(x, wl_bf[...], preferred_element_type=F32) + bl_ref[...]
        glu = jnp.minimum(glu, SWIGLU_LIMIT)
        lin = jnp.clip(lin, -SWIGLU_LIMIT, SWIGLU_LIMIT)
        out_ref[...] = (glu * jax.nn.sigmoid(SWIGLU_ALPHA * glu) * (lin + 1.0)).astype(out_ref.dtype)

    @pl.when(flag % 2 == 0)
    def _():
        out_ref[...] = jnp.zeros_like(out_ref)


def _expert_up(tables, xg, w_up, b_up, *, tm, tf):
    p, d = xg.shape
    n_e, _, ff2 = w_up.shape
    ff = ff2 // 2
    nt = ff // tf
    nsteps = tables[0].shape[0]
    grid_spec = pltpu.PrefetchScalarGridSpec(
        num_scalar_prefetch=4,
        grid=(nsteps,),
        in_specs=[
            pl.BlockSpec((tm, d), lambda s, blk, tile, ex, fl: (blk[s], 0)),
            pl.BlockSpec((None, d, tf), lambda s, blk, tile, ex, fl: (ex[s], 0, tile[s])),
            pl.BlockSpec((None, d, tf), lambda s, blk, tile, ex, fl: (ex[s], 0, nt + tile[s])),
            pl.BlockSpec((None, 1, tf), lambda s, blk, tile, ex, fl: (ex[s], 0, tile[s])),
            pl.BlockSpec((None, 1, tf), lambda s, blk, tile, ex, fl: (ex[s], 0, nt + tile[s])),
        ],
        out_specs=pl.BlockSpec((tm, tf), lambda s, blk, tile, ex, fl: (blk[s], tile[s])),
        scratch_shapes=[pltpu.VMEM((d, tf), BF16), pltpu.VMEM((d, tf), BF16)],
    )
    return pl.pallas_call(
        _expert_up_kernel,
        grid_spec=grid_spec,
        out_shape=jax.ShapeDtypeStruct((p, ff), BF16),
        compiler_params=_params(("arbitrary",)),
        name="expert_up",
    )(*tables, xg, w_up, w_up, b_up, b_up)


def _expert_down_kernel(blk_ref, tile_ref, exp_ref, flag_ref, a_ref, w_ref, b_ref, out_ref, w_bf):
    flag = flag_ref[pl.program_id(0)]

    @pl.when(flag >= 2)
    def _():
        w_bf[...] = w_ref[...].astype(BF16)

    @pl.when(flag % 2 == 1)
    def _():
        out_ref[...] = jnp.dot(a_ref[...], w_bf[...], preferred_element_type=F32) + b_ref[...]

    @pl.when(flag % 2 == 0)
    def _():
        out_ref[...] = jnp.zeros_like(out_ref)


def _expert_down(tables, act, w_down, b_down, *, tm, tn):
    p, ff = act.shape
    d = w_down.shape[2]
    nsteps = tables[0].shape[0]
    grid_spec = pltpu.PrefetchScalarGridSpec(
        num_scalar_prefetch=4,
        grid=(nsteps,),
        in_specs=[
            pl.BlockSpec((tm, ff), lambda s, blk, tile, ex, fl: (blk[s], 0)),
            pl.BlockSpec((None, ff, tn), lambda s, blk, tile, ex, fl: (ex[s], 0, tile[s])),
            pl.BlockSpec((None, 1, tn), lambda s, blk, tile, ex, fl: (ex[s], 0, tile[s])),
        ],
        out_specs=pl.BlockSpec((tm, tn), lambda s, blk, tile, ex, fl: (blk[s], tile[s])),
        scratch_shapes=[pltpu.VMEM((ff, tn), BF16)],
    )
    return pl.pallas_call(
        _expert_down_kernel,
        grid_spec=grid_spec,
        out_shape=jax.ShapeDtypeStruct((p, d), F32),
        compiler_params=_params(("arbitrary",)),
        name="expert_down",
    )(*tables, act, w_down, b_down)


def _expert_schedule(counts, *, tm, n_tiles, n_blocks):
    nb = (counts + tm - 1) // tm
    bend = jnp.cumsum(nb)
    bstart = bend - nb
    steps_e = nb * n_tiles
    cs = jnp.cumsum(steps_e)
    total = cs[-1]
    s = jnp.arange(n_blocks * n_tiles, dtype=jnp.int32)
    sc = jnp.minimum(s, total - 1)
    e = jnp.minimum(jnp.searchsorted(cs, sc, side="right"), counts.shape[0] - 1).astype(jnp.int32)
    r = sc - (cs[e] - steps_e[e])
    nbe = jnp.maximum(nb[e], 1)
    valid = s < total
    n_unused = jnp.maximum(n_blocks - bend[-1], 1)
    u = jnp.maximum(s - total, 0)
    tile = jnp.where(valid, r // nbe, u // n_unused)
    blk = jnp.where(valid, bstart[e] + r % nbe, bend[-1] + u % n_unused)
    first = jnp.logical_and(valid, r % nbe == 0)
    flag = valid.astype(jnp.int32) + 2 * first.astype(jnp.int32)
    return (blk.astype(jnp.int32), tile.astype(jnp.int32), e, flag)


def _combine_kernel(slot_ref, x_ref, route_ref, y_hbm, out_ref, buf, sem, *, tc):
    i = pl.program_id(0)
    n = pl.num_programs(0)

    def issue(blk, bslot):
        def body(r, carry):
            for k in range(TOP_K):
                src = slot_ref[(blk * tc + r) * TOP_K + k]
                pltpu.make_async_copy(y_hbm.at[pl.ds(src, 1)], buf.at[bslot, k, pl.ds(r, 1)],
                                      sem.at[bslot]).start()
            return carry
        lax.fori_loop(0, tc, body, 0, unroll=4)

    @pl.when(i == 0)
    def _():
        issue(0, 0)

    @pl.when(i + 1 < n)
    def _():
        issue(i + 1, (i + 1) % 2)

    bslot = i % 2
    for k in range(TOP_K):
        pltpu.make_async_copy(y_hbm.at[pl.ds(0, tc)], buf.at[bslot, k], sem.at[bslot]).wait()
    acc = x_ref[...]
    for k in range(TOP_K):
        acc = acc + route_ref[:, TOP_K + k:TOP_K + k + 1] * buf[bslot, k]
    out_ref[...] = acc


def _combine(slot_flat, x2, route, y, *, tc):
    t, d = x2.shape
    grid_spec = pltpu.PrefetchScalarGridSpec(
        num_scalar_prefetch=1,
        grid=(t // tc,),
        in_specs=[pl.BlockSpec((tc, d), lambda i, sl: (i, 0)),
                  pl.BlockSpec((tc, LANES), lambda i, sl: (i, 0)),
                  pl.BlockSpec(memory_space=pl.ANY)],
        out_specs=pl.BlockSpec((tc, d), lambda i, sl: (i, 0)),
        scratch_shapes=[pltpu.VMEM((2, TOP_K, tc, d), F32), pltpu.SemaphoreType.DMA((2,))],
    )
    return pl.pallas_call(
        functools.partial(_combine_kernel, tc=tc),
        grid_spec=grid_spec,
        out_shape=jax.ShapeDtypeStruct((t, d), F32),
        compiler_params=_params(("arbitrary",)),
        name="moe_combine",
    )(slot_flat, x2, route, y)


def _pad_lanes(a, offset=0):
    width = a.shape[-1]
    return jnp.pad(a, [(0, 0)] * (a.ndim - 1) + [(offset, LANES - offset - width)])


def _layer(x2d, mem2d, lw, *, batch, seq, mem_len):
    t, d = x2d.shape
    gla_qk = lw["gla_alpha_up"].shape[1]
    gla_rank = lw["gla_alpha_up"].shape[0]
    gla_v = lw["w_branch_gla"].shape[0]
    fox_w = lw["w_branch_fox"].shape[0]
    dk, dv = gla_qk // GLA_HEADS, gla_v // GLA_HEADS
    dh = fox_w // FOX_HEADS
    mem_w = lw["mem_w_q"].shape[1]
    mem_dh = mem_w // MEM_HEADS
    n_experts = lw["w_router"].shape[1]
    ff = lw["w_expert_down"].shape[1]

    sizes = (gla_qk, gla_qk, gla_v, gla_v, gla_rank, fox_w, fox_w, fox_w, FOX_HEADS, d, d)
    starts = [0]
    for sz in sizes:
        starts.append(starts[-1] + sz)
    w_in = lw["w_in"]
    seg = lambda i: w_in[:, starts[i]:starts[i + 1]]
    big_ids = (0, 1, 2, 3, 5, 6, 7, 9, 10)
    w_big = jnp.concatenate([seg(i) for i in big_ids], axis=1).astype(BF16)
    w_small = _pad_lanes(jnp.concatenate([seg(4), seg(8)], axis=1)).astype(BF16)
    off = {}
    acc = 0
    for i in big_ids:
        off[i] = acc
        acc += sizes[i]
    n_big = acc
    tn1 = 1024
    colscale = jnp.ones((1, n_big), F32)
    colscale = colscale.at[:, off[5]:off[5] + fox_w].set(
        jnp.tile(lw["fox_q_norm_g"] * (dh ** -0.5), FOX_HEADS)[None])
    colscale = colscale.at[:, off[6]:off[6] + fox_w].set(jnp.tile(lw["fox_k_norm_g"], FOX_HEADS)[None])
    proj, small = _norm_matmul(
        x2d, lw["mix_norm_g"][None], w_big, colscale, w_small,
        tm=1024, tn=tn1, norm_lo=off[5] // tn1, norm_hi=(off[6] + fox_w) // tn1)

    alpha_up_pad = jnp.pad(lw["gla_alpha_up"], ((0, LANES - gla_rank), (0, 0))).astype(BF16)
    o_gla = _gla(proj, small, alpha_up_pad, lw["gla_alpha_bias"][None], lw["gla_out_norm_g"][None],
                 batch=batch, seq=seq, dk=dk, dv=dv, offs=(off[0], off[1], off[2], off[3]), rows=256)

    c = _fcum(small, _pad_lanes(lw["fox_f_bias"][None], gla_rank), batch=batch, seq=seq, tb=512)
    c_row = c[:, gla_rank:gla_rank + FOX_HEADS].reshape(batch, seq, FOX_HEADS)
    c_row = c_row.transpose(0, 2, 1).reshape(batch * FOX_HEADS, 1, seq)
    o_fox = _fox(proj, c_row, batch=batch, seq=seq, dh=dh, offs=(off[5], off[6], off[7]), tq=512)

    merged = _merge(o_gla, o_fox, proj, lw["w_branch_gla"].astype(BF16),
                    lw["w_branch_fox"].astype(BF16), offs=(off[9], off[10]), tm=512, tn=1024)
    x1 = _resid_matmul(merged, lw["w_out"].astype(BF16), x2d, tm=1024, tn=1024)

    kv_scale = jnp.concatenate([jnp.tile(lw["mem_k_norm_g"], MEM_HEADS),
                                jnp.ones((mem_w,), F32)])[None]
    kv = _norm_matmul(mem2d, lw["mem_kv_norm_g"][None], lw["mem_w_kv"].astype(BF16), kv_scale, None,
                      tm=mem2d.shape[0], tn=mem_w, norm_lo=0, norm_hi=1)
    x2, rlogits = _memory(
        x1, lw["mem_norm_g"][None], lw["mem_w_q"].astype(BF16),
        (lw["mem_q_norm_g"] * (mem_dh ** -0.5))[None], kv,
        lw["mem_w_o"].astype(BF16), lw["moe_norm_g"][None],
        _pad_lanes(lw["w_router"]).astype(BF16), _pad_lanes(lw["b_router"][None]),
        batch=batch, seq=seq, mem_len=mem_len, tm=512)

    route, counts_f = _route(rlogits, n_experts=n_experts, tr=512)
    tm_e = 512
    a_total = t * TOP_K
    n_blocks = a_total // tm_e + n_experts
    p_rows = n_blocks * tm_e
    counts = counts_f[0, :n_experts].astype(jnp.int32)
    padded = (counts + tm_e - 1) // tm_e * tm_e
    pstart = jnp.cumsum(padded) - padded
    idx = route[:, :TOP_K].astype(jnp.int32)
    rank = route[:, 2 * TOP_K:3 * TOP_K].astype(jnp.int32)
    slot = (pstart[idx] + rank).reshape(a_total)
    slot_tok = jnp.zeros((p_rows,), jnp.int32).at[slot].set(
        jnp.arange(a_total, dtype=jnp.int32) // TOP_K)

    tf = 512
    xg = _gather_norm(slot_tok, x2, lw["moe_norm_g"][None], rows=256)
    up_tables = _expert_schedule(counts, tm=tm_e, n_tiles=ff // tf, n_blocks=n_blocks)
    act = _expert_up(up_tables, xg, lw["w_expert_up"], lw["b_expert_up"][:, None, :], tm=tm_e, tf=tf)
    tn_d = 512
    down_tables = _expert_schedule(counts, tm=tm_e, n_tiles=d // tn_d, n_blocks=n_blocks)
    y = _expert_down(down_tables, act, lw["w_expert_down"], lw["b_expert_down"][:, None, :],
                     tm=tm_e, tn=tn_d)
    return _combine(slot, x2, route, y, tc=128)


def kernel(x, mem, mix_norm_g, w_in, gla_alpha_up, gla_alpha_bias, gla_out_norm_g, fox_f_bias, fox_q_norm_g, fox_k_norm_g, w_branch_gla, w_branch_fox, w_out, mem_norm_g, mem_kv_norm_g, mem_w_q, mem_w_kv, mem_q_norm_g, mem_k_norm_g, mem_w_o, moe_norm_g, w_router, b_router, w_expert_up, b_expert_up, w_expert_down, b_expert_down):
    weights = dict(
        mix_norm_g=mix_norm_g, w_in=w_in, gla_alpha_up=gla_alpha_up, gla_alpha_bias=gla_alpha_bias,
        gla_out_norm_g=gla_out_norm_g, fox_f_bias=fox_f_bias, fox_q_norm_g=fox_q_norm_g,
        fox_k_norm_g=fox_k_norm_g, w_branch_gla=w_branch_gla, w_branch_fox=w_branch_fox, w_out=w_out,
        mem_norm_g=mem_norm_g, mem_kv_norm_g=mem_kv_norm_g, mem_w_q=mem_w_q, mem_w_kv=mem_w_kv,
        mem_q_norm_g=mem_q_norm_g, mem_k_norm_g=mem_k_norm_g, mem_w_o=mem_w_o, moe_norm_g=moe_norm_g,
        w_router=w_router, b_router=b_router, w_expert_up=w_expert_up, b_expert_up=b_expert_up,
        w_expert_down=w_expert_down, b_expert_down=b_expert_down)
    batch, seq, d = x.shape
    mem_len = mem.shape[1]
    x2d = x.reshape(batch * seq, d)
    mem2d = mem.reshape(batch * mem_len, d)
    for layer in range(mix_norm_g.shape[0]):
        lw = {name: w[layer] for name, w in weights.items()}
        x2d = _layer(x2d, mem2d, lw, batch=batch, seq=seq, mem_len=mem_len)
    return x2d.reshape(batch, seq, d)
```

```python
import functools

import jax
import jax.numpy as jnp
from jax import lax
from jax.experimental import pallas as pl
from jax.experimental.pallas import tpu as pltpu

F32 = jnp.float32
BF16 = jnp.bfloat16

EPS = 1e-6
LANES = 128
VMEM_LIMIT = 56 * 1024 * 1024

GLA_HEADS = 4
GLA_GATE_NORM = 16.0
GLA_CHUNK = 64
FOX_HEADS = 16
FOX_STRIP = 16
LOG2E = 1.4426950408889634
MEM_HEADS = 4
TOP_K = 4
SWIGLU_LIMIT = 7.0
SWIGLU_ALPHA = 1.702

NT_DIMS = (((1,), (1,)), ((), ()))
TN_DIMS = (((0,), (0,)), ((), ()))


def _params(sem, vmem=VMEM_LIMIT):
    return pltpu.CompilerParams(dimension_semantics=sem, vmem_limit_bytes=vmem)


def _rms(xf, g):
    return xf * lax.rsqrt(jnp.mean(xf * xf, axis=-1, keepdims=True) + EPS) * g


def _log_sigmoid(z):
    return jnp.minimum(z, 0.0) - jnp.log1p(jnp.exp(-jnp.abs(z)))


def _norm_matmul_kernel(*refs, norm_lo, norm_hi, tn, has_small):
    if has_small:
        x_ref, g_ref, w_ref, cs_ref, ws_ref, out_ref, small_ref, h_ref = refs
    else:
        x_ref, g_ref, w_ref, cs_ref, out_ref, h_ref = refs
    j = pl.program_id(1)

    @pl.when(j == 0)
    def _():
        hb = _rms(x_ref[...], g_ref[...]).astype(BF16)
        h_ref[...] = hb
        if has_small:
            small_ref[...] = jnp.dot(hb, ws_ref[...], preferred_element_type=F32)

    acc = jnp.dot(h_ref[...], w_ref[...], preferred_element_type=F32)
    is_norm = jnp.logical_and(j >= norm_lo, j < norm_hi)

    @pl.when(is_norm)
    def _():
        for c in range(tn // LANES):
            sl = slice(c * LANES, (c + 1) * LANES)
            out_ref[:, sl] = _rms(acc[:, sl], cs_ref[:, sl]).astype(out_ref.dtype)

    @pl.when(jnp.logical_not(is_norm))
    def _():
        out_ref[...] = acc.astype(out_ref.dtype)


def _norm_matmul(x, g, w, colscale, w_small, *, tm, tn, norm_lo, norm_hi):
    m, d = x.shape
    n = w.shape[1]
    has_small = w_small is not None
    in_specs = [
        pl.BlockSpec((tm, d), lambda i, j: (i, 0)),
        pl.BlockSpec((1, d), lambda i, j: (0, 0)),
        pl.BlockSpec((d, tn), lambda i, j: (0, j)),
        pl.BlockSpec((1, tn), lambda i, j: (0, j)),
    ]
    args = [x, g, w, colscale]
    out_shape = [jax.ShapeDtypeStruct((m, n), BF16)]
    out_specs = [pl.BlockSpec((tm, tn), lambda i, j: (i, j))]
    if has_small:
        in_specs.append(pl.BlockSpec((d, LANES), lambda i, j: (0, 0)))
        args.append(w_small)
        out_shape.append(jax.ShapeDtypeStruct((m, LANES), F32))
        out_specs.append(pl.BlockSpec((tm, LANES), lambda i, j: (i, 0)))
    res = pl.pallas_call(
        functools.partial(_norm_matmul_kernel, norm_lo=norm_lo, norm_hi=norm_hi, tn=tn,
                          has_small=has_small),
        grid=(m // tm, n // tn),
        in_specs=in_specs,
        out_specs=out_specs,
        out_shape=out_shape,
        scratch_shapes=[pltpu.VMEM((tm, d), BF16)],
        compiler_params=_params(("arbitrary", "arbitrary")),
        name="norm_matmul",
    )(*args)
    return res if has_small else res[0]


def _gla_kernel(q_ref, k_ref, v_ref, r_ref, sm_ref, au_ref, ab_ref, gn_ref, out_ref, st_ref,
                *, chunk, nsub, scale):
    @pl.when(pl.program_id(2) == 0)
    def _():
        st_ref[...] = jnp.zeros_like(st_ref)

    row = lax.broadcasted_iota(jnp.int32, (chunk, chunk), 0)
    col = lax.broadcasted_iota(jnp.int32, (chunk, chunk), 1)
    causal = row >= col
    tri = causal.astype(F32)
    for s in range(nsub):
        rows = pl.ds(s * chunk, chunk)
        z = jnp.dot(sm_ref[rows, :].astype(BF16), au_ref[...], preferred_element_type=F32)
        la = _log_sigmoid(z + ab_ref[...]) * (1.0 / GLA_GATE_NORM)
        b = jnp.dot(tri, la, precision=lax.Precision.HIGHEST, preferred_element_type=F32)
        b_last = b[chunk - 1:chunk, :]
        q = q_ref[rows, :].astype(F32)
        k = k_ref[rows, :].astype(F32)
        v = v_ref[rows, :]
        qd = (q * (scale * jnp.exp(b))).astype(BF16)
        kd = (k * jnp.exp(-b)).astype(BF16)
        kl = (k * jnp.exp(b_last - b)).astype(BF16)
        sc = lax.dot_general(qd, kd, NT_DIMS, preferred_element_type=F32)
        sc = jnp.where(causal, sc, 0.0).astype(BF16)
        st = st_ref[...]
        o = jnp.dot(sc, v, preferred_element_type=F32) + lax.dot_general(
            qd, st.astype(BF16), NT_DIMS, preferred_element_type=F32)
        st_ref[...] = st * jnp.exp(b_last) + lax.dot_general(
            v, kl, TN_DIMS, preferred_element_type=F32)
        r = r_ref[rows, :].astype(F32)
        out_ref[rows, :] = (_rms(o, gn_ref[...]) * (r * jax.nn.sigmoid(r))).astype(out_ref.dtype)


def _gla(proj, small, alpha_up_pad, alpha_bias, out_norm_g, *, batch, seq, dk, dv, offs, rows):
    t = batch * seq
    nblk = seq // rows
    h = GLA_HEADS
    qo, ko, vo, ro = (offs[0] // dk, offs[1] // dk, offs[2] // dv, offs[3] // dv)
    rmap = lambda b, hh, c: b * nblk + c
    return pl.pallas_call(
        functools.partial(_gla_kernel, chunk=GLA_CHUNK, nsub=rows // GLA_CHUNK, scale=dk ** -0.5),
        grid=(batch, h, nblk),
        in_specs=[
            pl.BlockSpec((rows, dk), lambda b, hh, c: (rmap(b, hh, c), qo + hh)),
            pl.BlockSpec((rows, dk), lambda b, hh, c: (rmap(b, hh, c), ko + hh)),
            pl.BlockSpec((rows, dv), lambda b, hh, c: (rmap(b, hh, c), vo + hh)),
            pl.BlockSpec((rows, dv), lambda b, hh, c: (rmap(b, hh, c), ro + hh)),
            pl.BlockSpec((rows, LANES), lambda b, hh, c: (rmap(b, hh, c), 0)),
            pl.BlockSpec((LANES, dk), lambda b, hh, c: (0, hh)),
            pl.BlockSpec((1, dk), lambda b, hh, c: (0, hh)),
            pl.BlockSpec((1, dv), lambda b, hh, c: (0, 0)),
        ],
        out_specs=pl.BlockSpec((rows, dv), lambda b, hh, c: (rmap(b, hh, c), hh)),
        out_shape=jax.ShapeDtypeStruct((t, h * dv), BF16),
        scratch_shapes=[pltpu.VMEM((dv, dk), F32)],
        compiler_params=_params(("arbitrary", "arbitrary", "arbitrary")),
        name="gla",
    )(proj, proj, proj, proj, small, alpha_up_pad, alpha_bias, out_norm_g)


def _fcum_kernel(sm_ref, bias_ref, out_ref, carry_ref, *, tb):
    @pl.when(pl.program_id(1) == 0)
    def _():
        carry_ref[...] = jnp.zeros_like(carry_ref)

    row = lax.broadcasted_iota(jnp.int32, (tb, tb), 0)
    col = lax.broadcasted_iota(jnp.int32, (tb, tb), 1)
    tri = (row >= col).astype(F32)
    lf = _log_sigmoid(sm_ref[...] + bias_ref[...])
    c = jnp.dot(tri, lf, precision=lax.Precision.HIGHEST, preferred_element_type=F32) + carry_ref[...]
    out_ref[...] = c
    carry_ref[...] = c[tb - 1:tb, :]


def _fcum(small, bias_pad, *, batch, seq, tb):
    nb = seq // tb
    return pl.pallas_call(
        functools.partial(_fcum_kernel, tb=tb),
        grid=(batch, nb),
        in_specs=[pl.BlockSpec((tb, LANES), lambda b, i: (b * nb + i, 0)),
                  pl.BlockSpec((1, LANES), lambda b, i: (0, 0))],
        out_specs=pl.BlockSpec((tb, LANES), lambda b, i: (b * nb + i, 0)),
        out_shape=jax.ShapeDtypeStruct(small.shape, F32),
        scratch_shapes=[pltpu.VMEM((1, LANES), F32)],
        compiler_params=_params(("arbitrary", "arbitrary")),
        name="forget_cumsum",
    )(small, bias_pad)


def _fox_kernel(qt_ref, kt_ref, q_ref, k_ref, v_ref, c_ref, out_ref, m_ref, mn_ref, a_ref, l_ref,
                acc_ref, s_ref, p_ref, *, tq, rb):
    p = pl.program_id(2)
    qi = qt_ref[p]
    ki = kt_ref[p]

    @pl.when(ki == 0)
    def _():
        m_ref[...] = jnp.full_like(m_ref, -jnp.inf)
        l_ref[...] = jnp.zeros_like(l_ref)
        acc_ref[...] = jnp.zeros_like(acc_ref)

    def update(mask_diag):
        s_ref[...] = lax.dot_general(q_ref[...], k_ref[...], NT_DIMS, preferred_element_type=F32)
        c = c_ref[0]

        def logits(r, rows, cols, t):
            s = s_ref[rows, cols] - c[:, cols]
            if mask_diag:
                row = r * rb + lax.broadcasted_iota(jnp.int32, (rb, LANES), 0)
                col = t * LANES + lax.broadcasted_iota(jnp.int32, (rb, LANES), 1)
                s = jnp.where(row >= col, s, -jnp.inf)
            return s

        def max_strip(r, carry):
            rows = pl.ds(pl.multiple_of(r * rb, rb), rb)
            part = logits(r, rows, slice(0, LANES), 0)
            for t in range(1, tq // LANES):
                part = jnp.maximum(part, logits(r, rows, slice(t * LANES, (t + 1) * LANES), t))
            m_old = m_ref[rows, :]
            m_new = jnp.maximum(m_old, jnp.max(part, axis=-1, keepdims=True))
            a_ref[rows, :] = jnp.exp2(m_old - m_new)
            mn_ref[rows, :] = m_new
            return carry

        def exp_strip(r, carry):
            rows = pl.ds(pl.multiple_of(r * rb, rb), rb)
            m_new = mn_ref[rows, :]
            m_ref[rows, :] = m_new
            part = jnp.zeros((rb, LANES), F32)
            for t in range(tq // LANES):
                cols = slice(t * LANES, (t + 1) * LANES)
                pr = jnp.exp2(logits(r, rows, cols, t) - m_new)
                part = part + pr
                p_ref[rows, cols] = pr.astype(BF16)
            alpha = a_ref[rows, :]
            l_ref[rows, :] = alpha * l_ref[rows, :] + part
            acc_ref[rows, :] = alpha * acc_ref[rows, :]
            return carry

        lax.fori_loop(0, tq // rb, max_strip, 0, unroll=16)
        lax.fori_loop(0, tq // rb, exp_strip, 0, unroll=4)
        acc_ref[...] += jnp.dot(p_ref[...], v_ref[...], preferred_element_type=F32)

    @pl.when(ki < qi)
    def _():
        update(False)

    @pl.when(ki == qi)
    def _():
        update(True)
        denom = jnp.sum(l_ref[...], axis=-1, keepdims=True)
        out_ref[...] = (acc_ref[...] / denom).astype(out_ref.dtype)


def _fox(proj, c_row, *, batch, seq, dh, offs, tq):
    t = batch * seq
    nq = seq // tq
    pairs = [(a, b) for a in range(nq) for b in range(a + 1)]
    qt = jnp.asarray([a for a, _ in pairs], jnp.int32)
    kt = jnp.asarray([b for _, b in pairs], jnp.int32)
    qo, ko, vo = (o // dh for o in offs)
    grid_spec = pltpu.PrefetchScalarGridSpec(
        num_scalar_prefetch=2,
        grid=(batch, FOX_HEADS, len(pairs)),
        in_specs=[
            pl.BlockSpec((tq, dh), lambda b, h, p, qt, kt: (b * nq + qt[p], qo + h)),
            pl.BlockSpec((tq, dh), lambda b, h, p, qt, kt: (b * nq + kt[p], ko + h)),
            pl.BlockSpec((tq, dh), lambda b, h, p, qt, kt: (b * nq + kt[p], vo + h)),
            pl.BlockSpec((1, 1, tq), lambda b, h, p, qt, kt: (b * FOX_HEADS + h, 0, kt[p])),
        ],
        out_specs=pl.BlockSpec((tq, dh), lambda b, h, p, qt, kt: (b * nq + qt[p], h)),
        scratch_shapes=[pltpu.VMEM((tq, LANES), F32)] * 4 + [
                        pltpu.VMEM((tq, dh), F32), pltpu.VMEM((tq, tq), F32),
                        pltpu.VMEM((tq, tq), BF16)],
    )
    return pl.pallas_call(
        functools.partial(_fox_kernel, tq=tq, rb=FOX_STRIP),
        grid_spec=grid_spec,
        out_shape=jax.ShapeDtypeStruct((t, FOX_HEADS * dh), BF16),
        compiler_params=_params(("arbitrary", "arbitrary", "arbitrary")),
        name="fox_attention",
    )(qt, kt, proj, proj, proj, c_row)


def _merge_kernel(a_ref, f_ref, ga_ref, gf_ref, wa_ref, wf_ref, out_ref):
    ya = jnp.dot(a_ref[...], wa_ref[...], preferred_element_type=F32)
    yf = jnp.dot(f_ref[...], wf_ref[...], preferred_element_type=F32)
    out_ref[...] = (jax.nn.sigmoid(ga_ref[...].astype(F32)) * ya
                    + jax.nn.sigmoid(gf_ref[...].astype(F32)) * yf).astype(out_ref.dtype)


def _merge(o_gla, o_fox, proj, w_gla, w_fox, *, offs, tm, tn):
    m, kdim = o_gla.shape
    n = w_gla.shape[1]
    go, fo = offs[0] // tn, offs[1] // tn
    return pl.pallas_call(
        _merge_kernel,
        grid=(m // tm, n // tn),
        in_specs=[
            pl.BlockSpec((tm, kdim), lambda i, j: (i, 0)),
            pl.BlockSpec((tm, o_fox.shape[1]), lambda i, j: (i, 0)),
            pl.BlockSpec((tm, tn), lambda i, j: (i, go + j)),
            pl.BlockSpec((tm, tn), lambda i, j: (i, fo + j)),
            pl.BlockSpec((kdim, tn), lambda i, j: (0, j)),
            pl.BlockSpec((o_fox.shape[1], tn), lambda i, j: (0, j)),
        ],
        out_specs=pl.BlockSpec((tm, tn), lambda i, j: (i, j)),
        out_shape=jax.ShapeDtypeStruct((m, n), BF16),
        compiler_params=_params(("arbitrary", "arbitrary")),
        name="gated_merge",
    )(o_gla, o_fox, proj, proj, w_gla, w_fox)


def _resid_matmul_kernel(a_ref, w_ref, res_ref, out_ref):
    out_ref[...] = res_ref[...] + jnp.dot(a_ref[...], w_ref[...], preferred_element_type=F32)


def _resid_matmul(a, w, res, *, tm, tn):
    m, kdim = a.shape
    n = w.shape[1]
    return pl.pallas_call(
        _resid_matmul_kernel,
        grid=(m // tm, n // tn),
        in_specs=[pl.BlockSpec((tm, kdim), lambda i, j: (i, 0)),
                  pl.BlockSpec((kdim, tn), lambda i, j: (0, j)),
                  pl.BlockSpec((tm, tn), lambda i, j: (i, j))],
        out_specs=pl.BlockSpec((tm, tn), lambda i, j: (i, j)),
        out_shape=jax.ShapeDtypeStruct((m, n), F32),
        compiler_params=_params(("arbitrary", "arbitrary")),
        name="resid_matmul",
    )(a, w, res)


def _memory_kernel(x_ref, gn_ref, wq_ref, gq_ref, k_ref, v_ref, wo_ref, gm_ref, wr_ref, br_ref,
                   x2_ref, rl_ref, *, dh):
    x1 = x_ref[...]
    hb = _rms(x1, gn_ref[...]).astype(BF16)
    q = jnp.dot(hb, wq_ref[...], preferred_element_type=F32)
    outs = []
    for h in range(MEM_HEADS):
        sl = slice(h * dh, (h + 1) * dh)
        qn = _rms(q[:, sl], gq_ref[...]).astype(BF16)
        s = lax.dot_general(qn, k_ref[:, sl], NT_DIMS, preferred_element_type=F32)
        s = s - jnp.max(s, axis=-1, keepdims=True)
        p = jnp.exp(s)
        p = p / jnp.sum(p, axis=-1, keepdims=True)
        outs.append(jnp.dot(p.astype(BF16), v_ref[:, sl], preferred_element_type=F32).astype(BF16))
    o = jnp.concatenate(outs, axis=-1)
    x2 = x1 + jnp.dot(o, wo_ref[...], preferred_element_type=F32)
    x2_ref[...] = x2
    xs = _rms(x2, gm_ref[...]).astype(BF16)
    rl_ref[...] = jnp.dot(xs, wr_ref[...], preferred_element_type=F32) + br_ref[...]


def _memory(x1, mem_norm_g, wq, gq, kv, wo, moe_norm_g, wr_pad, br_pad, *, batch, seq,
            mem_len, tm):
    t, d = x1.shape
    w = wq.shape[1]
    dh = w // MEM_HEADS
    nblk = seq // tm
    const = lambda i: (0, 0)
    return pl.pallas_call(
        functools.partial(_memory_kernel, dh=dh),
        grid=(t // tm,),
        in_specs=[
            pl.BlockSpec((tm, d), lambda i: (i, 0)),
            pl.BlockSpec((1, d), const),
            pl.BlockSpec((d, w), const),
            pl.BlockSpec((1, dh), const),
            pl.BlockSpec((mem_len, w), lambda i: (i // nblk, 0)),
            pl.BlockSpec((mem_len, w), lambda i: (i // nblk, 1)),
            pl.BlockSpec((w, d), const),
            pl.BlockSpec((1, d), const),
            pl.BlockSpec((d, LANES), const),
            pl.BlockSpec((1, LANES), const),
        ],
        out_specs=[pl.BlockSpec((tm, d), lambda i: (i, 0)),
                   pl.BlockSpec((tm, LANES), lambda i: (i, 0))],
        out_shape=[jax.ShapeDtypeStruct((t, d), F32), jax.ShapeDtypeStruct((t, LANES), F32)],
        compiler_params=_params(("arbitrary",)),
        name="memory_block",
    )(x1, mem_norm_g, wq, gq, kv, kv, wo, moe_norm_g, wr_pad, br_pad)


def _route_kernel(rl_ref, route_ref, counts_ref, carry_ref, *, tr, n_experts):
    @pl.when(pl.program_id(0) == 0)
    def _():
        carry_ref[...] = jnp.zeros_like(carry_ref)

    lane = lax.broadcasted_iota(jnp.int32, (tr, LANES), 1)
    lg = jnp.where(lane < n_experts, rl_ref[...], -jnp.inf)
    vals, hots = [], []
    for _ in range(TOP_K):
        mx = jnp.max(lg, axis=-1, keepdims=True)
        idx = jnp.min(jnp.where(lg == mx, lane, LANES), axis=-1, keepdims=True)
        hot = lane == idx
        vals.append(mx)
        hots.append(hot)
        lg = jnp.where(hot, -jnp.inf, lg)
    exps = [jnp.exp(v - vals[0]) for v in vals]
    denom = exps[0]
    for e in exps[1:]:
        denom = denom + e
    onehot = hots[0]
    for hsel in hots[1:]:
        onehot = jnp.logical_or(onehot, hsel)
    onehot_f = onehot.astype(F32)
    row = lax.broadcasted_iota(jnp.int32, (tr, tr), 0)
    col = lax.broadcasted_iota(jnp.int32, (tr, tr), 1)
    strict = (row > col).astype(BF16)
    before = jnp.dot(strict, onehot_f.astype(BF16), preferred_element_type=F32) + carry_ref[...]
    route = jnp.zeros((tr, LANES), F32)
    for k in range(TOP_K):
        idx_f = jnp.sum(jnp.where(hots[k], lane, 0), axis=-1, keepdims=True).astype(F32)
        rank = jnp.sum(jnp.where(hots[k], before, 0.0), axis=-1, keepdims=True)
        route = jnp.where(lane == k, idx_f, route)
        route = jnp.where(lane == TOP_K + k, exps[k] / denom, route)
        route = jnp.where(lane == 2 * TOP_K + k, rank, route)
    route_ref[...] = route
    total = carry_ref[...] + jnp.sum(onehot_f, axis=0, keepdims=True)
    carry_ref[...] = total
    counts_ref[...] = jnp.broadcast_to(total, counts_ref.shape)


def _route(rlogits, *, n_experts, tr):
    t = rlogits.shape[0]
    return pl.pallas_call(
        functools.partial(_route_kernel, tr=tr, n_experts=n_experts),
        grid=(t // tr,),
        in_specs=[pl.BlockSpec((tr, LANES), lambda i: (i, 0))],
        out_specs=[pl.BlockSpec((tr, LANES), lambda i: (i, 0)),
                   pl.BlockSpec((8, LANES), lambda i: (0, 0))],
        out_shape=[jax.ShapeDtypeStruct((t, LANES), F32), jax.ShapeDtypeStruct((8, LANES), F32)],
        scratch_shapes=[pltpu.VMEM((1, LANES), F32)],
        compiler_params=_params(("arbitrary",)),
        name="route_topk",
    )(rlogits)


def _gather_kernel(tok_ref, nused_ref, x_hbm, g_ref, out_ref, buf, sem, *, rows):
    i = pl.program_id(0)
    nused = nused_ref[0]

    def issue(blk, slot):
        def body(r, carry):
            tok = tok_ref[blk * rows + r]
            pltpu.make_async_copy(x_hbm.at[pl.ds(tok, 1)], buf.at[slot, pl.ds(r, 1)],
                                  sem.at[slot]).start()
            return carry
        lax.fori_loop(0, rows, body, 0, unroll=8)

    @pl.when(i == 0)
    def _():
        issue(0, 0)

    @pl.when(i + 1 < nused)
    def _():
        issue(i + 1, (i + 1) % 2)

    @pl.when(i < nused)
    def _():
        slot = i % 2
        pltpu.make_async_copy(x_hbm.at[pl.ds(0, rows)], buf.at[slot], sem.at[slot]).wait()
        out_ref[...] = _rms(buf[slot], g_ref[...]).astype(out_ref.dtype)

    @pl.when(i >= nused)
    def _():
        out_ref[...] = jnp.zeros_like(out_ref)


def _gather_norm(slot_tok, nused, x2, g, *, rows):
    p = slot_tok.shape[0]
    d = x2.shape[1]
    grid_spec = pltpu.PrefetchScalarGridSpec(
        num_scalar_prefetch=2,
        grid=(p // rows,),
        in_specs=[pl.BlockSpec(memory_space=pl.ANY),
                  pl.BlockSpec((1, d), lambda i, tok, nu: (0, 0))],
        out_specs=pl.BlockSpec((rows, d), lambda i, tok, nu: (i, 0)),
        scratch_shapes=[pltpu.VMEM((2, rows, d), F32), pltpu.SemaphoreType.DMA((2,))],
    )
    return pl.pallas_call(
        functools.partial(_gather_kernel, rows=rows),
        grid_spec=grid_spec,
        out_shape=jax.ShapeDtypeStruct((p, d), BF16),
        compiler_params=_params(("arbitrary",)),
        name="moe_gather",
    )(slot_tok, nused, x2, g)


def _expert_up_kernel(blk_ref, tile_ref, exp_ref, flag_ref, x_ref, wg_ref, wl_ref, bg_ref, bl_ref,
                      out_ref, wg_bf, wl_bf):
    flag = flag_ref[pl.program_id(0)]

    @pl.when(flag >= 2)
    def _():
        wg_bf[...] = wg_ref[...].astype(BF16)
        wl_bf[...] = wl_ref[...].astype(BF16)

    @pl.when(flag % 2 == 1)
    def _():
        x = x_ref[...]
        glu = jnp.dot(x, wg_bf[...], preferred_element_type=F32) + bg_ref[...]
        lin = jnp.dot(x, wl_bf[...], preferred_element_type=F32) + bl_ref[...]
        glu = jnp.minimum(glu, SWIGLU_LIMIT)
        lin = jnp.clip(lin, -SWIGLU_LIMIT, SWIGLU_LIMIT)
        out_ref[...] = (glu * jax.nn.sigmoid(SWIGLU_ALPHA * glu) * (lin + 1.0)).astype(out_ref.dtype)

    @pl.when(flag % 2 == 0)
    def _():
        out_ref[...] = jnp.zeros_like(out_ref)


def _expert_up(tables, xg, w_up, b_up, *, tm, tf):
    p, d = xg.shape
    n_e, _, ff2 = w_up.shape
    ff = ff2 // 2
    nt = ff // tf
    nsteps = tables[0].shape[0]
    grid_spec = pltpu.PrefetchScalarGridSpec(
        num_scalar_prefetch=4,
        grid=(nsteps,),
        in_specs=[
            pl.BlockSpec((tm, d), lambda s, blk, tile, ex, fl: (blk[s], 0)),
            pl.BlockSpec((None, d, tf), lambda s, blk, tile, ex, fl: (ex[s], 0, tile[s])),
            pl.BlockSpec((None, d, tf), lambda s, blk, tile, ex, fl: (ex[s], 0, nt + tile[s])),
            pl.BlockSpec((None, 1, tf), lambda s, blk, tile, ex, fl: (ex[s], 0, tile[s])),
            pl.BlockSpec((None, 1, tf), lambda s, blk, tile, ex, fl: (ex[s], 0, nt + tile[s])),
        ],
        out_specs=pl.BlockSpec((tm, tf), lambda s, blk, tile, ex, fl: (blk[s], tile[s])),
        scratch_shapes=[pltpu.VMEM((d, tf), BF16), pltpu.VMEM((d, tf), BF16)],
    )
    return pl.pallas_call(
        _expert_up_kernel,
        grid_spec=grid_spec,
        out_shape=jax.ShapeDtypeStruct((p, ff), BF16),
        compiler_params=_params(("arbitrary",)),
        name="expert_up",
    )(*tables, xg, w_up, w_up, b_up, b_up)


def _expert_down_kernel(blk_ref, tile_ref, exp_ref, flag_ref, a_ref, w_ref, b_ref, out_ref, w_bf):
    flag = flag_ref[pl.program_id(0)]

    @pl.when(flag >= 2)
    def _():
        w_bf[...] = w_ref[...].astype(BF16)

    @pl.when(flag % 2 == 1)
    def _():
        out_ref[...] = jnp.dot(a_ref[...], w_bf[...], preferred_element_type=F32) + b_ref[...]

    @pl.when(flag % 2 == 0)
    def _():
        out_ref[...] = jnp.zeros_like(out_ref)


def _expert_down(tables, act, w_down, b_down, *, tm, tn):
    p, ff = act.shape
    d = w_down.shape[2]
    nsteps = tables[0].shape[0]
    grid_spec = pltpu.PrefetchScalarGridSpec(
        num_scalar_prefetch=4,
        grid=(nsteps,),
        in_specs=[
            pl.BlockSpec((tm, ff), lambda s, blk, tile, ex, fl: (blk[s], 0)),
            pl.BlockSpec((None, ff, tn), lambda s, blk, tile, ex, fl: (ex[s], 0, tile[s])),
            pl.BlockSpec((None, 1, tn), lambda s, blk, tile, ex, fl: (ex[s], 0, tile[s])),
        ],
        out_specs=pl.BlockSpec((tm, tn), lambda s, blk, tile, ex, fl: (blk[s], tile[s])),
        scratch_shapes=[pltpu.VMEM((ff, tn), BF16)],
    )
    return pl.pallas_call(
        _expert_down_kernel,
        grid_spec=grid_spec,
        out_shape=jax.ShapeDtypeStruct((p, d), F32),
        compiler_params=_params(("arbitrary",)),
        name="expert_down",
    )(*tables, act, w_down, b_down)


def _expert_schedule(counts, *, tm, n_tiles, n_blocks):
    nb = (counts + tm - 1) // tm
    bend = jnp.cumsum(nb)
    bstart = bend - nb
    steps_e = nb * n_tiles
    cs = jnp.cumsum(steps_e)
    total = cs[-1]
    s = jnp.arange(n_blocks * n_tiles, dtype=jnp.int32)
    sc = jnp.minimum(s, total - 1)
    e = jnp.sum((cs[None, :] <= sc[:, None]).astype(jnp.int32), axis=1)
    e = jnp.minimum(e, counts.shape[0] - 1)
    r = sc - (cs[e] - steps_e[e])
    nbe = jnp.maximum(nb[e], 1)
    valid = s < total
    n_unused = jnp.maximum(n_blocks - bend[-1], 1)
    u = jnp.maximum(s - total, 0)
    tile = jnp.where(valid, r // nbe, u // n_unused)
    blk = jnp.where(valid, bstart[e] + r % nbe, bend[-1] + u % n_unused)
    first = jnp.logical_and(valid, r % nbe == 0)
    flag = valid.astype(jnp.int32) + 2 * first.astype(jnp.int32)
    return (blk.astype(jnp.int32), tile.astype(jnp.int32), e, flag)


def _combine_kernel(slot_ref, x_ref, route_ref, y_hbm, out_ref, buf, sem, *, tc):
    i = pl.program_id(0)
    n = pl.num_programs(0)

    def issue(blk, bslot):
        def body(r, carry):
            for k in range(TOP_K):
                src = slot_ref[(blk * tc + r) * TOP_K + k]
                pltpu.make_async_copy(y_hbm.at[pl.ds(src, 1)], buf.at[bslot, k, pl.ds(r, 1)],
                                      sem.at[bslot]).start()
            return carry
        lax.fori_loop(0, tc, body, 0, unroll=4)

    @pl.when(i == 0)
    def _():
        issue(0, 0)

    @pl.when(i + 1 < n)
    def _():
        issue(i + 1, (i + 1) % 2)

    bslot = i % 2
    for k in range(TOP_K):
        pltpu.make_async_copy(y_hbm.at[pl.ds(0, tc)], buf.at[bslot, k], sem.at[bslot]).wait()
    acc = x_ref[...]
    for k in range(TOP_K):
        acc = acc + route_ref[:, TOP_K + k:TOP_K + k + 1] * buf[bslot, k]
    out_ref[...] = acc


def _combine(slot_flat, x2, route, y, *, tc):
    t, d = x2.shape
    grid_spec = pltpu.PrefetchScalarGridSpec(
        num_scalar_prefetch=1,
        grid=(t // tc,),
        in_specs=[pl.BlockSpec((tc, d), lambda i, sl: (i, 0)),
                  pl.BlockSpec((tc, LANES), lambda i, sl: (i, 0)),
                  pl.BlockSpec(memory_space=pl.ANY)],
        out_specs=pl.BlockSpec((tc, d), lambda i, sl: (i, 0)),
        scratch_shapes=[pltpu.VMEM((2, TOP_K, tc, d), F32), pltpu.SemaphoreType.DMA((2,))],
    )
    return pl.pallas_call(
        functools.partial(_combine_kernel, tc=tc),
        grid_spec=grid_spec,
        out_shape=jax.ShapeDtypeStruct((t, d), F32),
        compiler_params=_params(("arbitrary",)),
        name="moe_combine",
    )(slot_flat, x2, route, y)


def _pad_lanes(a, offset=0):
    width = a.shape[-1]
    return jnp.pad(a, [(0, 0)] * (a.ndim - 1) + [(offset, LANES - offset - width)])


def _layer(x2d, mem2d, lw, *, batch, seq, mem_len):
    t, d = x2d.shape
    gla_qk = lw["gla_alpha_up"].shape[1]
    gla_rank = lw["gla_alpha_up"].shape[0]
    gla_v = lw["w_branch_gla"].shape[0]
    fox_w = lw["w_branch_fox"].shape[0]
    dk, dv = gla_qk // GLA_HEADS, gla_v // GLA_HEADS
    dh = fox_w // FOX_HEADS
    mem_w = lw["mem_w_q"].shape[1]
    mem_dh = mem_w // MEM_HEADS
    n_experts = lw["w_router"].shape[1]
    ff = lw["w_expert_down"].shape[1]

    sizes = (gla_qk, gla_qk, gla_v, gla_v, gla_rank, fox_w, fox_w, fox_w, FOX_HEADS, d, d)
    starts = [0]
    for sz in sizes:
        starts.append(starts[-1] + sz)
    w_in = lw["w_in"]
    seg = lambda i: w_in[:, starts[i]:starts[i + 1]]
    big_ids = (0, 1, 2, 3, 5, 6, 7, 9, 10)
    w_big = jnp.concatenate([seg(i) for i in big_ids], axis=1).astype(BF16)
    w_small = _pad_lanes(jnp.concatenate([seg(4), seg(8)], axis=1)).astype(BF16)
    off = {}
    acc = 0
    for i in big_ids:
        off[i] = acc
        acc += sizes[i]
    n_big = acc
    tn1 = 1024
    colscale = jnp.ones((1, n_big), F32)
    colscale = colscale.at[:, off[5]:off[5] + fox_w].set(
        jnp.tile(lw["fox_q_norm_g"] * (dh ** -0.5 * LOG2E), FOX_HEADS)[None])
    colscale = colscale.at[:, off[6]:off[6] + fox_w].set(jnp.tile(lw["fox_k_norm_g"], FOX_HEADS)[None])
    proj, small = _norm_matmul(
        x2d, lw["mix_norm_g"][None], w_big, colscale, w_small,
        tm=1024, tn=tn1, norm_lo=off[5] // tn1, norm_hi=(off[6] + fox_w) // tn1)

    alpha_up_pad = jnp.pad(lw["gla_alpha_up"], ((0, LANES - gla_rank), (0, 0))).astype(BF16)
    o_gla = _gla(proj, small, alpha_up_pad, lw["gla_alpha_bias"][None], lw["gla_out_norm_g"][None],
                 batch=batch, seq=seq, dk=dk, dv=dv, offs=(off[0], off[1], off[2], off[3]), rows=256)

    c = _fcum(small, _pad_lanes(lw["fox_f_bias"][None], gla_rank), batch=batch, seq=seq, tb=512)
    c_row = c[:, gla_rank:gla_rank + FOX_HEADS].reshape(batch, seq, FOX_HEADS)
    c_row = c_row.transpose(0, 2, 1).reshape(batch * FOX_HEADS, 1, seq) * LOG2E
    o_fox = _fox(proj, c_row, batch=batch, seq=seq, dh=dh, offs=(off[5], off[6], off[7]),
                 tq=min(seq, 1024))

    merged = _merge(o_gla, o_fox, proj, lw["w_branch_gla"].astype(BF16),
                    lw["w_branch_fox"].astype(BF16), offs=(off[9], off[10]), tm=512, tn=1024)
    x1 = _resid_matmul(merged, lw["w_out"].astype(BF16), x2d, tm=1024, tn=1024)

    kv_scale = jnp.concatenate([jnp.tile(lw["mem_k_norm_g"], MEM_HEADS),
                                jnp.ones((mem_w,), F32)])[None]
    kv = _norm_matmul(mem2d, lw["mem_kv_norm_g"][None], lw["mem_w_kv"].astype(BF16), kv_scale, None,
                      tm=mem2d.shape[0], tn=mem_w, norm_lo=0, norm_hi=1)
    x2, rlogits = _memory(
        x1, lw["mem_norm_g"][None], lw["mem_w_q"].astype(BF16),
        (lw["mem_q_norm_g"] * (mem_dh ** -0.5))[None], kv,
        lw["mem_w_o"].astype(BF16), lw["moe_norm_g"][None],
        _pad_lanes(lw["w_router"]).astype(BF16), _pad_lanes(lw["b_router"][None]),
        batch=batch, seq=seq, mem_len=mem_len, tm=512)

    route, counts_f = _route(rlogits, n_experts=n_experts, tr=512)
    tm_e = 512
    a_total = t * TOP_K
    n_blocks = a_total // tm_e + n_experts
    p_rows = n_blocks * tm_e
    counts = counts_f[0, :n_experts].astype(jnp.int32)
    padded = (counts + tm_e - 1) // tm_e * tm_e
    pstart = jnp.cumsum(padded) - padded
    idx = route[:, :TOP_K].astype(jnp.int32)
    rank = route[:, 2 * TOP_K:3 * TOP_K].astype(jnp.int32)
    slot = (pstart[idx] + rank).reshape(a_total)
    slot_tok = (jnp.arange(p_rows, dtype=jnp.int32) % t).at[slot].set(
        jnp.arange(a_total, dtype=jnp.int32) // TOP_K)

    tf = 512
    g_rows = 256
    nused = (jnp.sum(padded) // g_rows).astype(jnp.int32).reshape(1)
    xg = _gather_norm(slot_tok, nused, x2, lw["moe_norm_g"][None], rows=g_rows)
    up_tables = _expert_schedule(counts, tm=tm_e, n_tiles=ff // tf, n_blocks=n_blocks)
    act = _expert_up(up_tables, xg, lw["w_expert_up"], lw["b_expert_up"][:, None, :], tm=tm_e, tf=tf)
    tn_d = 1024
    down_tables = _expert_schedule(counts, tm=tm_e, n_tiles=d // tn_d, n_blocks=n_blocks)
    y = _expert_down(down_tables, act, lw["w_expert_down"], lw["b_expert_down"][:, None, :],
                     tm=tm_e, tn=tn_d)
    return _combine(slot, x2, route, y, tc=128)


def kernel(x, mem, mix_norm_g, w_in, gla_alpha_up, gla_alpha_bias, gla_out_norm_g, fox_f_bias, fox_q_norm_g, fox_k_norm_g, w_branch_gla, w_branch_fox, w_out, mem_norm_g, mem_kv_norm_g, mem_w_q, mem_w_kv, mem_q_norm_g, mem_k_norm_g, mem_w_o, moe_norm_g, w_router, b_router, w_expert_up, b_expert_up, w_expert_down, b_expert_down):
    weights = dict(
        mix_norm_g=mix_norm_g, w_in=w_in, gla_alpha_up=gla_alpha_up, gla_alpha_bias=gla_alpha_bias,
        gla_out_norm_g=gla_out_norm_g, fox_f_bias=fox_f_bias, fox_q_norm_g=fox_q_norm_g,
        fox_k_norm_g=fox_k_norm_g, w_branch_gla=w_branch_gla, w_branch_fox=w_branch_fox, w_out=w_out,
        mem_norm_g=mem_norm_g, mem_kv_norm_g=mem_kv_norm_g, mem_w_q=mem_w_q, mem_w_kv=mem_w_kv,
        mem_q_norm_g=mem_q_norm_g, mem_k_norm_g=mem_k_norm_g, mem_w_o=mem_w_o, moe_norm_g=moe_norm_g,
        w_router=w_router, b_router=b_router, w_expert_up=w_expert_up, b_expert_up=b_expert_up,
        w_expert_down=w_expert_down, b_expert_down=b_expert_down)
    batch, seq, d = x.shape
    mem_len = mem.shape[1]
    x2d = x.reshape(batch * seq, d)
    mem2d = mem.reshape(batch * mem_len, d)
    for layer in range(mix_norm_g.shape[0]):
        lw = {name: w[layer] for name, w in weights.items()}
        x2d = _layer(x2d, mem2d, lw, batch=batch, seq=seq, mem_len=mem_len)
    return x2d.reshape(batch, seq, d)
```

```python
import functools

import jax
import jax.numpy as jnp
from jax import lax
from jax.experimental import pallas as pl
from jax.experimental.pallas import tpu as pltpu

F32 = jnp.float32
BF16 = jnp.bfloat16

EPS = 1e-6
LANES = 128
VMEM_LIMIT = 56 * 1024 * 1024

GLA_HEADS = 4
GLA_GATE_NORM = 16.0
GLA_CHUNK = 64
GLA_SUB = 16
FOX_HEADS = 16
FOX_STRIP = 16
FOX_GROUP = 256
LOG2E = 1.4426950408889634
MEM_HEADS = 4
TOP_K = 4
SWIGLU_LIMIT = 7.0
SWIGLU_ALPHA = 1.702

NT_DIMS = (((1,), (1,)), ((), ()))
TN_DIMS = (((0,), (0,)), ((), ()))


def _params(sem, vmem=VMEM_LIMIT):
    return pltpu.CompilerParams(dimension_semantics=sem, vmem_limit_bytes=vmem)


def _rms(xf, g):
    return xf * lax.rsqrt(jnp.mean(xf * xf, axis=-1, keepdims=True) + EPS) * g


def _log_sigmoid(z):
    return jnp.minimum(z, 0.0) - jnp.log1p(jnp.exp(-jnp.abs(z)))


def _norm_matmul_kernel(*refs, norm_lo, norm_hi, tn, has_small):
    if has_small:
        x_ref, g_ref, w_ref, cs_ref, ws_ref, out_ref, small_ref, h_ref = refs
    else:
        x_ref, g_ref, w_ref, cs_ref, out_ref, h_ref = refs
    j = pl.program_id(1)

    @pl.when(j == 0)
    def _():
        hb = _rms(x_ref[...], g_ref[...]).astype(BF16)
        h_ref[...] = hb
        if has_small:
            small_ref[...] = jnp.dot(hb, ws_ref[...], preferred_element_type=F32)

    acc = jnp.dot(h_ref[...], w_ref[...], preferred_element_type=F32)
    is_norm = jnp.logical_and(j >= norm_lo, j < norm_hi)

    @pl.when(is_norm)
    def _():
        for c in range(tn // LANES):
            sl = slice(c * LANES, (c + 1) * LANES)
            out_ref[:, sl] = _rms(acc[:, sl], cs_ref[:, sl]).astype(out_ref.dtype)

    @pl.when(jnp.logical_not(is_norm))
    def _():
        out_ref[...] = acc.astype(out_ref.dtype)


def _norm_matmul(x, g, w, colscale, w_small, *, tm, tn, norm_lo, norm_hi):
    m, d = x.shape
    n = w.shape[1]
    has_small = w_small is not None
    in_specs = [
        pl.BlockSpec((tm, d), lambda i, j: (i, 0)),
        pl.BlockSpec((1, d), lambda i, j: (0, 0)),
        pl.BlockSpec((d, tn), lambda i, j: (0, j)),
        pl.BlockSpec((1, tn), lambda i, j: (0, j)),
    ]
    args = [x, g, w, colscale]
    out_shape = [jax.ShapeDtypeStruct((m, n), BF16)]
    out_specs = [pl.BlockSpec((tm, tn), lambda i, j: (i, j))]
    if has_small:
        in_specs.append(pl.BlockSpec((d, LANES), lambda i, j: (0, 0)))
        args.append(w_small)
        out_shape.append(jax.ShapeDtypeStruct((m, LANES), F32))
        out_specs.append(pl.BlockSpec((tm, LANES), lambda i, j: (i, 0)))
    res = pl.pallas_call(
        functools.partial(_norm_matmul_kernel, norm_lo=norm_lo, norm_hi=norm_hi, tn=tn,
                          has_small=has_small),
        grid=(m // tm, n // tn),
        in_specs=in_specs,
        out_specs=out_specs,
        out_shape=out_shape,
        scratch_shapes=[pltpu.VMEM((tm, d), BF16)],
        compiler_params=_params(("arbitrary", "arbitrary")),
        name="norm_matmul",
    )(*args)
    return res if has_small else res[0]


def _gla_kernel(q_ref, k_ref, v_ref, r_ref, sm_ref, au_ref, ab_ref, gn_ref, out_ref, st_ref,
                *, chunk, nsub, scale):
    @pl.when(pl.program_id(2) == 0)
    def _():
        st_ref[...] = jnp.zeros_like(st_ref)

    row = lax.broadcasted_iota(jnp.int32, (chunk, chunk), 0)
    col = lax.broadcasted_iota(jnp.int32, (chunk, chunk), 1)
    causal = row >= col
    tri = causal.astype(F32)
    key_row = lax.broadcasted_iota(jnp.int32, (chunk, 1), 0)
    for s in range(nsub):
        rows = pl.ds(s * chunk, chunk)
        z = jnp.dot(sm_ref[rows, :].astype(BF16), au_ref[...], preferred_element_type=F32)
        la = _log_sigmoid(z + ab_ref[...]) * (1.0 / GLA_GATE_NORM)
        b = jnp.dot(tri, la, precision=lax.Precision.HIGHEST, preferred_element_type=F32)
        b_last = b[chunk - 1:chunk, :]
        q = q_ref[rows, :].astype(F32)
        k = k_ref[rows, :].astype(F32)
        v = v_ref[rows, :]
        qd = (q * (scale * jnp.exp(b))).astype(BF16)
        kl = (k * jnp.exp(b_last - b)).astype(BF16)
        blocks = []
        for gi in range(chunk // GLA_SUB):
            grp = slice(gi * GLA_SUB, (gi + 1) * GLA_SUB)
            if gi == 0:
                q_fac, k_arg = b[grp], -b
            else:
                r = b[gi * GLA_SUB - 1:gi * GLA_SUB, :]
                q_fac, k_arg = b[grp] - r, r - b
            k_arg = jnp.where(key_row < (gi + 1) * GLA_SUB, k_arg, -jnp.inf)
            qg = (q[grp] * (scale * jnp.exp(q_fac))).astype(BF16)
            kg = (k * jnp.exp(k_arg)).astype(BF16)
            blocks.append(lax.dot_general(qg, kg, NT_DIMS, preferred_element_type=F32))
        sc = jnp.where(causal, jnp.concatenate(blocks, axis=0), 0.0).astype(BF16)
        st = st_ref[...]
        o = jnp.dot(sc, v, preferred_element_type=F32) + lax.dot_general(
            qd, st.astype(BF16), NT_DIMS, preferred_element_type=F32)
        st_ref[...] = st * jnp.exp(b_last) + lax.dot_general(
            v, kl, TN_DIMS, preferred_element_type=F32)
        r = r_ref[rows, :].astype(F32)
        out_ref[rows, :] = (_rms(o, gn_ref[...]) * (r * jax.nn.sigmoid(r))).astype(out_ref.dtype)


def _gla(proj, small, alpha_up_pad, alpha_bias, out_norm_g, *, batch, seq, dk, dv, offs, rows):
    t = batch * seq
    nblk = seq // rows
    h = GLA_HEADS
    qo, ko, vo, ro = (offs[0] // dk, offs[1] // dk, offs[2] // dv, offs[3] // dv)
    rmap = lambda b, hh, c: b * nblk + c
    return pl.pallas_call(
        functools.partial(_gla_kernel, chunk=GLA_CHUNK, nsub=rows // GLA_CHUNK, scale=dk ** -0.5),
        grid=(batch, h, nblk),
        in_specs=[
            pl.BlockSpec((rows, dk), lambda b, hh, c: (rmap(b, hh, c), qo + hh)),
            pl.BlockSpec((rows, dk), lambda b, hh, c: (rmap(b, hh, c), ko + hh)),
            pl.BlockSpec((rows, dv), lambda b, hh, c: (rmap(b, hh, c), vo + hh)),
            pl.BlockSpec((rows, dv), lambda b, hh, c: (rmap(b, hh, c), ro + hh)),
            pl.BlockSpec((rows, LANES), lambda b, hh, c: (rmap(b, hh, c), 0)),
            pl.BlockSpec((LANES, dk), lambda b, hh, c: (0, hh)),
            pl.BlockSpec((1, dk), lambda b, hh, c: (0, hh)),
            pl.BlockSpec((1, dv), lambda b, hh, c: (0, 0)),
        ],
        out_specs=pl.BlockSpec((rows, dv), lambda b, hh, c: (rmap(b, hh, c), hh)),
        out_shape=jax.ShapeDtypeStruct((t, h * dv), BF16),
        scratch_shapes=[pltpu.VMEM((dv, dk), F32)],
        compiler_params=_params(("arbitrary", "arbitrary", "arbitrary")),
        name="gla",
    )(proj, proj, proj, proj, small, alpha_up_pad, alpha_bias, out_norm_g)


def _fcum_kernel(sm_ref, bias_ref, out_ref, carry_ref, *, tb):
    @pl.when(pl.program_id(1) == 0)
    def _():
        carry_ref[...] = jnp.zeros_like(carry_ref)

    row = lax.broadcasted_iota(jnp.int32, (tb, tb), 0)
    col = lax.broadcasted_iota(jnp.int32, (tb, tb), 1)
    tri = (row >= col).astype(F32)
    lf = _log_sigmoid(sm_ref[...] + bias_ref[...])
    c = jnp.dot(tri, lf, precision=lax.Precision.HIGHEST, preferred_element_type=F32) + carry_ref[...]
    out_ref[...] = c
    carry_ref[...] = c[tb - 1:tb, :]


def _fcum(small, bias_pad, *, batch, seq, tb):
    nb = seq // tb
    return pl.pallas_call(
        functools.partial(_fcum_kernel, tb=tb),
        grid=(batch, nb),
        in_specs=[pl.BlockSpec((tb, LANES), lambda b, i: (b * nb + i, 0)),
                  pl.BlockSpec((1, LANES), lambda b, i: (0, 0))],
        out_specs=pl.BlockSpec((tb, LANES), lambda b, i: (b * nb + i, 0)),
        out_shape=jax.ShapeDtypeStruct(small.shape, F32),
        scratch_shapes=[pltpu.VMEM((1, LANES), F32)],
        compiler_params=_params(("arbitrary", "arbitrary")),
        name="forget_cumsum",
    )(small, bias_pad)


def _fox_kernel(qt_ref, kt_ref, q_ref, k_ref, v_ref, c_ref, out_ref, m_ref, mn_ref, a_ref, l_ref,
                acc_ref, s_ref, p_ref, *, tq, rb, grp):
    p = pl.program_id(2)
    qi = qt_ref[p]
    ki = kt_ref[p]

    @pl.when(ki == 0)
    def _():
        m_ref[...] = jnp.full_like(m_ref, -jnp.inf)
        l_ref[...] = jnp.zeros_like(l_ref)
        acc_ref[...] = jnp.zeros_like(acc_ref)

    def update(diag):
        c = c_ref[0]

        def ncols(g):
            return (g + 1) * grp if diag else tq

        def qk(g):
            n = ncols(g)
            s_ref[g % 2, :, 0:n] = lax.dot_general(q_ref[g * grp:(g + 1) * grp, :], k_ref[0:n, :],
                                                   NT_DIMS, preferred_element_type=F32)

        def pv(g):
            n = ncols(g)
            rows = slice(g * grp, (g + 1) * grp)
            acc_ref[rows, :] += jnp.dot(p_ref[g % 2, :, 0:n], v_ref[0:n, :],
                                        preferred_element_type=F32)

        def softmax(g):
            buf = g % 2
            strips = []
            for st in range(grp // rb):
                r0 = g * grp + st * rb
                visible = (r0 + rb - 1) // LANES + 1 if diag else tq // LANES
                strips.append((slice(st * rb, (st + 1) * rb), slice(r0, r0 + rb), r0, visible))

            def logits(rows_l, r0, t):
                cols = slice(t * LANES, (t + 1) * LANES)
                s = s_ref[buf, rows_l, cols] - c[:, cols]
                if diag and (t + 1) * LANES - 1 > r0:
                    row = r0 + lax.broadcasted_iota(jnp.int32, (rb, LANES), 0)
                    col = t * LANES + lax.broadcasted_iota(jnp.int32, (rb, LANES), 1)
                    s = jnp.where(row >= col, s, -jnp.inf)
                return s

            for rows_l, rows, r0, visible in strips:
                part = logits(rows_l, r0, 0)
                for t in range(1, visible):
                    part = jnp.maximum(part, logits(rows_l, r0, t))
                m_old = m_ref[rows, :]
                m_new = jnp.maximum(m_old, jnp.max(part, axis=-1, keepdims=True))
                a_ref[rows, :] = jnp.exp2(m_old - m_new)
                mn_ref[rows, :] = m_new
            for rows_l, rows, r0, visible in strips:
                m_new = mn_ref[rows, :]
                m_ref[rows, :] = m_new
                part = jnp.zeros((rb, LANES), F32)
                for t in range(ncols(g) // LANES):
                    cols = slice(t * LANES, (t + 1) * LANES)
                    if t < visible:
                        pr = jnp.exp2(logits(rows_l, r0, t) - m_new)
                        part = part + pr
                        p_ref[buf, rows_l, cols] = pr.astype(BF16)
                    else:
                        p_ref[buf, rows_l, cols] = jnp.zeros((rb, LANES), BF16)
                alpha = a_ref[rows, :]
                l_ref[rows, :] = alpha * l_ref[rows, :] + part
                acc_ref[rows, :] = alpha * acc_ref[rows, :]

        ngrp = tq // grp
        qk(0)
        for g in range(ngrp):
            if g + 1 < ngrp:
                qk(g + 1)
            softmax(g)
            if g >= 1:
                pv(g - 1)
        pv(ngrp - 1)

    @pl.when(ki < qi)
    def _():
        update(False)

    @pl.when(ki == qi)
    def _():
        update(True)
        denom = jnp.sum(l_ref[...], axis=-1, keepdims=True)
        out_ref[...] = (acc_ref[...] / denom).astype(out_ref.dtype)


def _fox(proj, c_row, *, batch, seq, dh, offs, tq):
    t = batch * seq
    nq = seq // tq
    pairs = [(a, b) for a in range(nq) for b in range(a + 1)]
    qt = jnp.asarray([a for a, _ in pairs], jnp.int32)
    kt = jnp.asarray([b for _, b in pairs], jnp.int32)
    qo, ko, vo = (o // dh for o in offs)
    grp = min(tq, FOX_GROUP)
    grid_spec = pltpu.PrefetchScalarGridSpec(
        num_scalar_prefetch=2,
        grid=(batch, FOX_HEADS, len(pairs)),
        in_specs=[
            pl.BlockSpec((tq, dh), lambda b, h, p, qt, kt: (b * nq + qt[p], qo + h)),
            pl.BlockSpec((tq, dh), lambda b, h, p, qt, kt: (b * nq + kt[p], ko + h)),
            pl.BlockSpec((tq, dh), lambda b, h, p, qt, kt: (b * nq + kt[p], vo + h)),
            pl.BlockSpec((1, 1, tq), lambda b, h, p, qt, kt: (b * FOX_HEADS + h, 0, kt[p])),
        ],
        out_specs=pl.BlockSpec((tq, dh), lambda b, h, p, qt, kt: (b * nq + qt[p], h)),
        scratch_shapes=[pltpu.VMEM((tq, LANES), F32)] * 4 + [
                        pltpu.VMEM((tq, dh), F32), pltpu.VMEM((2, grp, tq), F32),
                        pltpu.VMEM((2, grp, tq), BF16)],
    )
    return pl.pallas_call(
        functools.partial(_fox_kernel, tq=tq, rb=FOX_STRIP, grp=grp),
        grid_spec=grid_spec,
        out_shape=jax.ShapeDtypeStruct((t, FOX_HEADS * dh), BF16),
        compiler_params=_params(("arbitrary", "arbitrary", "arbitrary")),
        name="fox_attention",
    )(qt, kt, proj, proj, proj, c_row)


def _merge_kernel(a_ref, f_ref, ga_ref, gf_ref, wa_ref, wf_ref, out_ref):
    ya = jnp.dot(a_ref[...], wa_ref[...], preferred_element_type=F32)
    yf = jnp.dot(f_ref[...], wf_ref[...], preferred_element_type=F32)
    out_ref[...] = (jax.nn.sigmoid(ga_ref[...].astype(F32)) * ya
                    + jax.nn.sigmoid(gf_ref[...].astype(F32)) * yf).astype(out_ref.dtype)


def _merge(o_gla, o_fox, proj, w_gla, w_fox, *, offs, tm, tn):
    m, kdim = o_gla.shape
    n = w_gla.shape[1]
    go, fo = offs[0] // tn, offs[1] // tn
    return pl.pallas_call(
        _merge_kernel,
        grid=(m // tm, n // tn),
        in_specs=[
            pl.BlockSpec((tm, kdim), lambda i, j: (i, 0)),
            pl.BlockSpec((tm, o_fox.shape[1]), lambda i, j: (i, 0)),
            pl.BlockSpec((tm, tn), lambda i, j: (i, go + j)),
            pl.BlockSpec((tm, tn), lambda i, j: (i, fo + j)),
            pl.BlockSpec((kdim, tn), lambda i, j: (0, j)),
            pl.BlockSpec((o_fox.shape[1], tn), lambda i, j: (0, j)),
        ],
        out_specs=pl.BlockSpec((tm, tn), lambda i, j: (i, j)),
        out_shape=jax.ShapeDtypeStruct((m, n), BF16),
        compiler_params=_params(("arbitrary", "arbitrary")),
        name="gated_merge",
    )(o_gla, o_fox, proj, proj, w_gla, w_fox)


def _resid_matmul_kernel(a_ref, w_ref, res_ref, out_ref):
    out_ref[...] = res_ref[...] + jnp.dot(a_ref[...], w_ref[...], preferred_element_type=F32)


def _resid_matmul(a, w, res, *, tm, tn):
    m, kdim = a.shape
    n = w.shape[1]
    return pl.pallas_call(
        _resid_matmul_kernel,
        grid=(m // tm, n // tn),
        in_specs=[pl.BlockSpec((tm, kdim), lambda i, j: (i, 0)),
                  pl.BlockSpec((kdim, tn), lambda i, j: (0, j)),
                  pl.BlockSpec((tm, tn), lambda i, j: (i, j))],
        out_specs=pl.BlockSpec((tm, tn), lambda i, j: (i, j)),
        out_shape=jax.ShapeDtypeStruct((m, n), F32),
        compiler_params=_params(("arbitrary", "arbitrary")),
        name="resid_matmul",
    )(a, w, res)


def _memory_kernel(x_ref, gn_ref, wq_ref, gq_ref, k_ref, v_ref, wo_ref, gm_ref, wr_ref, br_ref,
                   x2_ref, rl_ref, *, dh):
    x1 = x_ref[...]
    hb = _rms(x1, gn_ref[...]).astype(BF16)
    q = jnp.dot(hb, wq_ref[...], preferred_element_type=F32)
    outs = []
    for h in range(MEM_HEADS):
        sl = slice(h * dh, (h + 1) * dh)
        qn = _rms(q[:, sl], gq_ref[...]).astype(BF16)
        s = lax.dot_general(qn, k_ref[:, sl], NT_DIMS, preferred_element_type=F32)
        s = s - jnp.max(s, axis=-1, keepdims=True)
        p = jnp.exp(s)
        p = p / jnp.sum(p, axis=-1, keepdims=True)
        outs.append(jnp.dot(p.astype(BF16), v_ref[:, sl], preferred_element_type=F32).astype(BF16))
    o = jnp.concatenate(outs, axis=-1)
    x2 = x1 + jnp.dot(o, wo_ref[...], preferred_element_type=F32)
    x2_ref[...] = x2
    xs = _rms(x2, gm_ref[...]).astype(BF16)
    rl_ref[...] = jnp.dot(xs, wr_ref[...], preferred_element_type=F32) + br_ref[...]


def _memory(x1, mem_norm_g, wq, gq, kv, wo, moe_norm_g, wr_pad, br_pad, *, batch, seq,
            mem_len, tm):
    t, d = x1.shape
    w = wq.shape[1]
    dh = w // MEM_HEADS
    nblk = seq // tm
    const = lambda i: (0, 0)
    return pl.pallas_call(
        functools.partial(_memory_kernel, dh=dh),
        grid=(t // tm,),
        in_specs=[
            pl.BlockSpec((tm, d), lambda i: (i, 0)),
            pl.BlockSpec((1, d), const),
            pl.BlockSpec((d, w), const),
            pl.BlockSpec((1, dh), const),
            pl.BlockSpec((mem_len, w), lambda i: (i // nblk, 0)),
            pl.BlockSpec((mem_len, w), lambda i: (i // nblk, 1)),
            pl.BlockSpec((w, d), const),
            pl.BlockSpec((1, d), const),
            pl.BlockSpec((d, LANES), const),
            pl.BlockSpec((1, LANES), const),
        ],
        out_specs=[pl.BlockSpec((tm, d), lambda i: (i, 0)),
                   pl.BlockSpec((tm, LANES), lambda i: (i, 0))],
        out_shape=[jax.ShapeDtypeStruct((t, d), F32), jax.ShapeDtypeStruct((t, LANES), F32)],
        compiler_params=_params(("arbitrary",)),
        name="memory_block",
    )(x1, mem_norm_g, wq, gq, kv, kv, wo, moe_norm_g, wr_pad, br_pad)


def _route_kernel(rl_ref, route_ref, counts_ref, carry_ref, *, tr, n_experts):
    @pl.when(pl.program_id(0) == 0)
    def _():
        carry_ref[...] = jnp.zeros_like(carry_ref)

    lane = lax.broadcasted_iota(jnp.int32, (tr, LANES), 1)
    lg = jnp.where(lane < n_experts, rl_ref[...], -jnp.inf)
    vals, hots = [], []
    for _ in range(TOP_K):
        mx = jnp.max(lg, axis=-1, keepdims=True)
        idx = jnp.min(jnp.where(lg == mx, lane, LANES), axis=-1, keepdims=True)
        hot = lane == idx
        vals.append(mx)
        hots.append(hot)
        lg = jnp.where(hot, -jnp.inf, lg)
    exps = [jnp.exp(v - vals[0]) for v in vals]
    denom = exps[0]
    for e in exps[1:]:
        denom = denom + e
    onehot = hots[0]
    for hsel in hots[1:]:
        onehot = jnp.logical_or(onehot, hsel)
    onehot_f = onehot.astype(F32)
    row = lax.broadcasted_iota(jnp.int32, (tr, tr), 0)
    col = lax.broadcasted_iota(jnp.int32, (tr, tr), 1)
    strict = (row > col).astype(BF16)
    before = jnp.dot(strict, onehot_f.astype(BF16), preferred_element_type=F32) + carry_ref[...]
    route = jnp.zeros((tr, LANES), F32)
    for k in range(TOP_K):
        idx_f = jnp.sum(jnp.where(hots[k], lane, 0), axis=-1, keepdims=True).astype(F32)
        rank = jnp.sum(jnp.where(hots[k], before, 0.0), axis=-1, keepdims=True)
        route = jnp.where(lane == k, idx_f, route)
        route = jnp.where(lane == TOP_K + k, exps[k] / denom, route)
        route = jnp.where(lane == 2 * TOP_K + k, rank, route)
    route_ref[...] = route
    total = carry_ref[...] + jnp.sum(onehot_f, axis=0, keepdims=True)
    carry_ref[...] = total
    counts_ref[...] = jnp.broadcast_to(total, counts_ref.shape)


def _route(rlogits, *, n_experts, tr):
    t = rlogits.shape[0]
    return pl.pallas_call(
        functools.partial(_route_kernel, tr=tr, n_experts=n_experts),
        grid=(t // tr,),
        in_specs=[pl.BlockSpec((tr, LANES), lambda i: (i, 0))],
        out_specs=[pl.BlockSpec((tr, LANES), lambda i: (i, 0)),
                   pl.BlockSpec((8, LANES), lambda i: (0, 0))],
        out_shape=[jax.ShapeDtypeStruct((t, LANES), F32), jax.ShapeDtypeStruct((8, LANES), F32)],
        scratch_shapes=[pltpu.VMEM((1, LANES), F32)],
        compiler_params=_params(("arbitrary",)),
        name="route_topk",
    )(rlogits)


def _gather_kernel(tok_ref, nused_ref, x_hbm, g_ref, out_ref, buf, sem, *, rows):
    i = pl.program_id(0)
    nused = nused_ref[0]

    def issue(blk, slot):
        def body(r, carry):
            tok = tok_ref[blk * rows + r]
            pltpu.make_async_copy(x_hbm.at[pl.ds(tok, 1)], buf.at[slot, pl.ds(r, 1)],
                                  sem.at[slot]).start()
            return carry
        lax.fori_loop(0, rows, body, 0, unroll=8)

    @pl.when(i == 0)
    def _():
        issue(0, 0)

    @pl.when(i + 1 < nused)
    def _():
        issue(i + 1, (i + 1) % 2)

    @pl.when(i < nused)
    def _():
        slot = i % 2
        pltpu.make_async_copy(x_hbm.at[pl.ds(0, rows)], buf.at[slot], sem.at[slot]).wait()
        out_ref[...] = _rms(buf[slot], g_ref[...]).astype(out_ref.dtype)

    @pl.when(i >= nused)
    def _():
        out_ref[...] = jnp.zeros_like(out_ref)


def _gather_norm(slot_tok, nused, x2, g, *, rows):
    p = slot_tok.shape[0]
    d = x2.shape[1]
    grid_spec = pltpu.PrefetchScalarGridSpec(
        num_scalar_prefetch=2,
        grid=(p // rows,),
        in_specs=[pl.BlockSpec(memory_space=pl.ANY),
                  pl.BlockSpec((1, d), lambda i, tok, nu: (0, 0))],
        out_specs=pl.BlockSpec((rows, d), lambda i, tok, nu: (i, 0)),
        scratch_shapes=[pltpu.VMEM((2, rows, d), F32), pltpu.SemaphoreType.DMA((2,))],
    )
    return pl.pallas_call(
        functools.partial(_gather_kernel, rows=rows),
        grid_spec=grid_spec,
        out_shape=jax.ShapeDtypeStruct((p, d), BF16),
        compiler_params=_params(("arbitrary",)),
        name="moe_gather",
    )(slot_tok, nused, x2, g)


def _expert_up_kernel(blk_ref, tile_ref, exp_ref, flag_ref, x_ref, wg_ref, wl_ref, bg_ref, bl_ref,
                      out_ref, wg_bf, wl_bf):
    flag = flag_ref[pl.program_id(0)]

    @pl.when(flag >= 2)
    def _():
        wg_bf[...] = wg_ref[...].astype(BF16)
        wl_bf[...] = wl_ref[...].astype(BF16)

    @pl.when(flag % 2 == 1)
    def _():
        x = x_ref[...]
        glu = jnp.dot(x, wg_bf[...], preferred_element_type=F32) + bg_ref[...]
        lin = jnp.dot(x, wl_bf[...], preferred_element_type=F32) + bl_ref[...]
        glu = jnp.minimum(glu, SWIGLU_LIMIT)
        lin = jnp.clip(lin, -SWIGLU_LIMIT, SWIGLU_LIMIT)
        out_ref[...] = (glu * jax.nn.sigmoid(SWIGLU_ALPHA * glu) * (lin + 1.0)).astype(out_ref.dtype)

    @pl.when(flag % 2 == 0)
    def _():
        out_ref[...] = jnp.zeros_like(out_ref)


def _expert_up(tables, xg, w_up, b_up, *, tm, tf):
    p, d = xg.shape
    n_e, _, ff2 = w_up.shape
    ff = ff2 // 2
    nt = ff // tf
    nsteps = tables[0].shape[0]
    grid_spec = pltpu.PrefetchScalarGridSpec(
        num_scalar_prefetch=4,
        grid=(nsteps,),
        in_specs=[
            pl.BlockSpec((tm, d), lambda s, blk, tile, ex, fl: (blk[s], 0)),
            pl.BlockSpec((None, d, tf), lambda s, blk, tile, ex, fl: (ex[s], 0, tile[s])),
            pl.BlockSpec((None, d, tf), lambda s, blk, tile, ex, fl: (ex[s], 0, nt + tile[s])),
            pl.BlockSpec((None, 1, tf), lambda s, blk, tile, ex, fl: (ex[s], 0, tile[s])),
            pl.BlockSpec((None, 1, tf), lambda s, blk, tile, ex, fl: (ex[s], 0, nt + tile[s])),
        ],
        out_specs=pl.BlockSpec((tm, tf), lambda s, blk, tile, ex, fl: (blk[s], tile[s])),
        scratch_shapes=[pltpu.VMEM((d, tf), BF16), pltpu.VMEM((d, tf), BF16)],
    )
    return pl.pallas_call(
        _expert_up_kernel,
        grid_spec=grid_spec,
        out_shape=jax.ShapeDtypeStruct((p, ff), BF16),
        compiler_params=_params(("arbitrary",)),
        name="expert_up",
    )(*tables, xg, w_up, w_up, b_up, b_up)


def _expert_down_kernel(blk_ref, tile_ref, exp_ref, flag_ref, a_ref, w_ref, b_ref, out_ref, w_bf):
    flag = flag_ref[pl.program_id(0)]

    @pl.when(flag >= 2)
    def _():
        w_bf[...] = w_ref[...].astype(BF16)

    @pl.when(flag % 2 == 1)
    def _():
        out_ref[...] = jnp.dot(a_ref[...], w_bf[...], preferred_element_type=F32) + b_ref[...]

    @pl.when(flag % 2 == 0)
    def _():
        out_ref[...] = jnp.zeros_like(out_ref)


def _expert_down(tables, act, w_down, b_down, *, tm, tn):
    p, ff = act.shape
    d = w_down.shape[2]
    nsteps = tables[0].shape[0]
    grid_spec = pltpu.PrefetchScalarGridSpec(
        num_scalar_prefetch=4,
        grid=(nsteps,),
        in_specs=[
            pl.BlockSpec((tm, ff), lambda s, blk, tile, ex, fl: (blk[s], 0)),
            pl.BlockSpec((None, ff, tn), lambda s, blk, tile, ex, fl: (ex[s], 0, tile[s])),
            pl.BlockSpec((None, 1, tn), lambda s, blk, tile, ex, fl: (ex[s], 0, tile[s])),
        ],
        out_specs=pl.BlockSpec((tm, tn), lambda s, blk, tile, ex, fl: (blk[s], tile[s])),
        scratch_shapes=[pltpu.VMEM((ff, tn), BF16)],
    )
    return pl.pallas_call(
        _expert_down_kernel,
        grid_spec=grid_spec,
        out_shape=jax.ShapeDtypeStruct((p, d), F32),
        compiler_params=_params(("arbitrary",)),
        name="expert_down",
    )(*tables, act, w_down, b_down)


def _expert_schedule(counts, *, tm, n_tiles, n_blocks):
    nb = (counts + tm - 1) // tm
    bend = jnp.cumsum(nb)
    bstart = bend - nb
    steps_e = nb * n_tiles
    cs = jnp.cumsum(steps_e)
    total = cs[-1]
    s = jnp.arange(n_blocks * n_tiles, dtype=jnp.int32)
    sc = jnp.minimum(s, total - 1)
    e = jnp.sum((cs[None, :] <= sc[:, None]).astype(jnp.int32), axis=1)
    e = jnp.minimum(e, counts.shape[0] - 1)
    r = sc - (cs[e] - steps_e[e])
    nbe = jnp.maximum(nb[e], 1)
    valid = s < total
    n_unused = jnp.maximum(n_blocks - bend[-1], 1)
    u = jnp.maximum(s - total, 0)
    tile = jnp.where(valid, r // nbe, u // n_unused)
    blk = jnp.where(valid, bstart[e] + r % nbe, bend[-1] + u % n_unused)
    first = jnp.logical_and(valid, r % nbe == 0)
    flag = valid.astype(jnp.int32) + 2 * first.astype(jnp.int32)
    return (blk.astype(jnp.int32), tile.astype(jnp.int32), e, flag)


def _combine_kernel(slot_ref, x_ref, route_ref, y_hbm, out_ref, buf, sem, *, tc):
    i = pl.program_id(0)
    n = pl.num_programs(0)

    def issue(blk, bslot):
        def body(r, carry):
            for k in range(TOP_K):
                src = slot_ref[(blk * tc + r) * TOP_K + k]
                pltpu.make_async_copy(y_hbm.at[pl.ds(src, 1)], buf.at[bslot, k, pl.ds(r, 1)],
                                      sem.at[bslot]).start()
            return carry
        lax.fori_loop(0, tc, body, 0, unroll=4)

    @pl.when(i == 0)
    def _():
        issue(0, 0)

    @pl.when(i + 1 < n)
    def _():
        issue(i + 1, (i + 1) % 2)

    bslot = i % 2
    for k in range(TOP_K):
        pltpu.make_async_copy(y_hbm.at[pl.ds(0, tc)], buf.at[bslot, k], sem.at[bslot]).wait()
    acc = x_ref[...]
    for k in range(TOP_K):
        acc = acc + route_ref[:, TOP_K + k:TOP_K + k + 1] * buf[bslot, k]
    out_ref[...] = acc


def _combine(slot_flat, x2, route, y, *, tc):
    t, d = x2.shape
    grid_spec = pltpu.PrefetchScalarGridSpec(
        num_scalar_prefetch=1,
        grid=(t // tc,),
        in_specs=[pl.BlockSpec((tc, d), lambda i, sl: (i, 0)),
                  pl.BlockSpec((tc, LANES), lambda i, sl: (i, 0)),
                  pl.BlockSpec(memory_space=pl.ANY)],
        out_specs=pl.BlockSpec((tc, d), lambda i, sl: (i, 0)),
        scratch_shapes=[pltpu.VMEM((2, TOP_K, tc, d), F32), pltpu.SemaphoreType.DMA((2,))],
    )
    return pl.pallas_call(
        functools.partial(_combine_kernel, tc=tc),
        grid_spec=grid_spec,
        out_shape=jax.ShapeDtypeStruct((t, d), F32),
        compiler_params=_params(("arbitrary",)),
        name="moe_combine",
    )(slot_flat, x2, route, y)


def _pad_lanes(a, offset=0):
    width = a.shape[-1]
    return jnp.pad(a, [(0, 0)] * (a.ndim - 1) + [(offset, LANES - offset - width)])


def _layer(x2d, mem2d, lw, *, batch, seq, mem_len):
    t, d = x2d.shape
    gla_qk = lw["gla_alpha_up"].shape[1]
    gla_rank = lw["gla_alpha_up"].shape[0]
    gla_v = lw["w_branch_gla"].shape[0]
    fox_w = lw["w_branch_fox"].shape[0]
    dk, dv = gla_qk // GLA_HEADS, gla_v // GLA_HEADS
    dh = fox_w // FOX_HEADS
    mem_w = lw["mem_w_q"].shape[1]
    mem_dh = mem_w // MEM_HEADS
    n_experts = lw["w_router"].shape[1]
    ff = lw["w_expert_down"].shape[1]

    sizes = (gla_qk, gla_qk, gla_v, gla_v, gla_rank, fox_w, fox_w, fox_w, FOX_HEADS, d, d)
    starts = [0]
    for sz in sizes:
        starts.append(starts[-1] + sz)
    w_in = lw["w_in"]
    seg = lambda i: w_in[:, starts[i]:starts[i + 1]]
    big_ids = (0, 1, 2, 3, 5, 6, 7, 9, 10)
    w_big = jnp.concatenate([seg(i).astype(BF16) for i in big_ids], axis=1)
    w_small = _pad_lanes(jnp.concatenate([seg(4), seg(8)], axis=1)).astype(BF16)
    off = {}
    acc = 0
    for i in big_ids:
        off[i] = acc
        acc += sizes[i]
    n_big = acc
    tn1 = 1024
    colscale = jnp.ones((1, n_big), F32)
    colscale = colscale.at[:, off[5]:off[5] + fox_w].set(
        jnp.tile(lw["fox_q_norm_g"] * (dh ** -0.5 * LOG2E), FOX_HEADS)[None])
    colscale = colscale.at[:, off[6]:off[6] + fox_w].set(jnp.tile(lw["fox_k_norm_g"], FOX_HEADS)[None])
    proj, small = _norm_matmul(
        x2d, lw["mix_norm_g"][None], w_big, colscale, w_small,
        tm=1024, tn=tn1, norm_lo=off[5] // tn1, norm_hi=(off[6] + fox_w) // tn1)

    alpha_up_pad = jnp.pad(lw["gla_alpha_up"], ((0, LANES - gla_rank), (0, 0))).astype(BF16)
    o_gla = _gla(proj, small, alpha_up_pad, lw["gla_alpha_bias"][None], lw["gla_out_norm_g"][None],
                 batch=batch, seq=seq, dk=dk, dv=dv, offs=(off[0], off[1], off[2], off[3]), rows=256)

    c = _fcum(small, _pad_lanes(lw["fox_f_bias"][None], gla_rank), batch=batch, seq=seq, tb=512)
    c_row = c[:, gla_rank:gla_rank + FOX_HEADS].reshape(batch, seq, FOX_HEADS)
    c_row = c_row.transpose(0, 2, 1).reshape(batch * FOX_HEADS, 1, seq) * LOG2E
    o_fox = _fox(proj, c_row, batch=batch, seq=seq, dh=dh, offs=(off[5], off[6], off[7]),
                 tq=min(seq, 1024))

    merged = _merge(o_gla, o_fox, proj, lw["w_branch_gla"].astype(BF16),
                    lw["w_branch_fox"].astype(BF16), offs=(off[9], off[10]), tm=512, tn=1024)
    x1 = _resid_matmul(merged, lw["w_out"].astype(BF16), x2d, tm=1024, tn=1024)

    kv_scale = jnp.concatenate([jnp.tile(lw["mem_k_norm_g"], MEM_HEADS),
                                jnp.ones((mem_w,), F32)])[None]
    kv = _norm_matmul(mem2d, lw["mem_kv_norm_g"][None], lw["mem_w_kv"].astype(BF16), kv_scale, None,
                      tm=mem2d.shape[0], tn=mem_w, norm_lo=0, norm_hi=1)
    x2, rlogits = _memory(
        x1, lw["mem_norm_g"][None], lw["mem_w_q"].astype(BF16),
        (lw["mem_q_norm_g"] * (mem_dh ** -0.5))[None], kv,
        lw["mem_w_o"].astype(BF16), lw["moe_norm_g"][None],
        _pad_lanes(lw["w_router"]).astype(BF16), _pad_lanes(lw["b_router"][None]),
        batch=batch, seq=seq, mem_len=mem_len, tm=512)

    route, counts_f = _route(rlogits, n_experts=n_experts, tr=512)
    tm_e = 512
    a_total = t * TOP_K
    n_blocks = a_total // tm_e + n_experts
    p_rows = n_blocks * tm_e
    counts = counts_f[0, :n_experts].astype(jnp.int32)
    padded = (counts + tm_e - 1) // tm_e * tm_e
    pstart = jnp.cumsum(padded) - padded
    idx = route[:, :TOP_K].astype(jnp.int32)
    rank = route[:, 2 * TOP_K:3 * TOP_K].astype(jnp.int32)
    hit = idx[:, :, None] == jnp.arange(n_experts, dtype=jnp.int32)
    slot = (jnp.sum(jnp.where(hit, pstart, 0), axis=-1) + rank).reshape(a_total)
    slot_tok = (jnp.arange(p_rows, dtype=jnp.int32) % t).at[slot].set(
        jnp.arange(a_total, dtype=jnp.int32) // TOP_K)

    tf = 1024
    g_rows = 256
    nused = (jnp.sum(padded) // g_rows).astype(jnp.int32).reshape(1)
    xg = _gather_norm(slot_tok, nused, x2, lw["moe_norm_g"][None], rows=g_rows)
    up_tables = _expert_schedule(counts, tm=tm_e, n_tiles=ff // tf, n_blocks=n_blocks)
    act = _expert_up(up_tables, xg, lw["w_expert_up"], lw["b_expert_up"][:, None, :], tm=tm_e, tf=tf)
    tn_d = 1024
    down_tables = _expert_schedule(counts, tm=tm_e, n_tiles=d // tn_d, n_blocks=n_blocks)
    y = _expert_down(down_tables, act, lw["w_expert_down"], lw["b_expert_down"][:, None, :],
                     tm=tm_e, tn=tn_d)
    return _combine(slot, x2, route, y, tc=128)


def kernel(x, mem, mix_norm_g, w_in, gla_alpha_up, gla_alpha_bias, gla_out_norm_g, fox_f_bias, fox_q_norm_g, fox_k_norm_g, w_branch_gla, w_branch_fox, w_out, mem_norm_g, mem_kv_norm_g, mem_w_q, mem_w_kv, mem_q_norm_g, mem_k_norm_g, mem_w_o, moe_norm_g, w_router, b_router, w_expert_up, b_expert_up, w_expert_down, b_expert_down):
    weights = dict(
        mix_norm_g=mix_norm_g, w_in=w_in, gla_alpha_up=gla_alpha_up, gla_alpha_bias=gla_alpha_bias,
        gla_out_norm_g=gla_out_norm_g, fox_f_bias=fox_f_bias, fox_q_norm_g=fox_q_norm_g,
        fox_k_norm_g=fox_k_norm_g, w_branch_gla=w_branch_gla, w_branch_fox=w_branch_fox, w_out=w_out,
        mem_norm_g=mem_norm_g, mem_kv_norm_g=mem_kv_norm_g, mem_w_q=mem_w_q, mem_w_kv=mem_w_kv,
        mem_q_norm_g=mem_q_norm_g, mem_k_norm_g=mem_k_norm_g, mem_w_o=mem_w_o, moe_norm_g=moe_norm_g,
        w_router=w_router, b_router=b_router, w_expert_up=w_expert_up, b_expert_up=b_expert_up,
        w_expert_down=w_expert_down, b_expert_down=b_expert_down)
    batch, seq, d = x.shape
    mem_len = mem.shape[1]
    x2d = x.reshape(batch * seq, d)
    mem2d = mem.reshape(batch * mem_len, d)
    for layer in range(mix_norm_g.shape[0]):
        lw = {name: w[layer] for name, w in weights.items()}
        x2d = _layer(x2d, mem2d, lw, batch=batch, seq=seq, mem_len=mem_len)
    return x2d.reshape(batch, seq, d)
```

```python
import functools

import jax
import jax.numpy as jnp
from jax import lax
from jax.experimental import pallas as pl
from jax.experimental.pallas import tpu as pltpu

F32 = jnp.float32
BF16 = jnp.bfloat16

EPS = 1e-6
LANES = 128
VMEM_LIMIT = 56 * 1024 * 1024

GLA_HEADS = 4
GLA_GATE_NORM = 16.0
GLA_CHUNK = 64
GLA_SUB = 16
FOX_HEADS = 16
FOX_STRIP = 16
FOX_GROUP = 128
LOG2E = 1.4426950408889634
MEM_HEADS = 4
TOP_K = 4
SWIGLU_LIMIT = 7.0
SWIGLU_ALPHA = 1.702

NT_DIMS = (((1,), (1,)), ((), ()))
TN_DIMS = (((0,), (0,)), ((), ()))


def _params(sem, vmem=VMEM_LIMIT):
    return pltpu.CompilerParams(dimension_semantics=sem, vmem_limit_bytes=vmem)


def _rms(xf, g):
    return xf * lax.rsqrt(jnp.mean(xf * xf, axis=-1, keepdims=True) + EPS) * g


def _log_sigmoid(z):
    return jnp.minimum(z, 0.0) - jnp.log1p(jnp.exp(-jnp.abs(z)))


def _norm_matmul_kernel(*refs, norm_lo, norm_hi, tn, has_small):
    if has_small:
        x_ref, g_ref, w_ref, cs_ref, ws_ref, out_ref, small_ref, h_ref = refs
    else:
        x_ref, g_ref, w_ref, cs_ref, out_ref, h_ref = refs
    j = pl.program_id(1)

    @pl.when(j == 0)
    def _():
        hb = _rms(x_ref[...], g_ref[...]).astype(BF16)
        h_ref[...] = hb
        if has_small:
            small_ref[...] = jnp.dot(hb, ws_ref[...], preferred_element_type=F32)

    acc = jnp.dot(h_ref[...], w_ref[...], preferred_element_type=F32)
    is_norm = jnp.logical_and(j >= norm_lo, j < norm_hi)

    @pl.when(is_norm)
    def _():
        for c in range(tn // LANES):
            sl = slice(c * LANES, (c + 1) * LANES)
            out_ref[:, sl] = _rms(acc[:, sl], cs_ref[:, sl]).astype(out_ref.dtype)

    @pl.when(jnp.logical_not(is_norm))
    def _():
        out_ref[...] = acc.astype(out_ref.dtype)


def _norm_matmul(x, g, w, colscale, w_small, *, tm, tn, norm_lo, norm_hi):
    m, d = x.shape
    n = w.shape[1]
    has_small = w_small is not None
    in_specs = [
        pl.BlockSpec((tm, d), lambda i, j: (i, 0)),
        pl.BlockSpec((1, d), lambda i, j: (0, 0)),
        pl.BlockSpec((d, tn), lambda i, j: (0, j)),
        pl.BlockSpec((1, tn), lambda i, j: (0, j)),
    ]
    args = [x, g, w, colscale]
    out_shape = [jax.ShapeDtypeStruct((m, n), BF16)]
    out_specs = [pl.BlockSpec((tm, tn), lambda i, j: (i, j))]
    if has_small:
        in_specs.append(pl.BlockSpec((d, LANES), lambda i, j: (0, 0)))
        args.append(w_small)
        out_shape.append(jax.ShapeDtypeStruct((m, LANES), F32))
        out_specs.append(pl.BlockSpec((tm, LANES), lambda i, j: (i, 0)))
    res = pl.pallas_call(
        functools.partial(_norm_matmul_kernel, norm_lo=norm_lo, norm_hi=norm_hi, tn=tn,
                          has_small=has_small),
        grid=(m // tm, n // tn),
        in_specs=in_specs,
        out_specs=out_specs,
        out_shape=out_shape,
        scratch_shapes=[pltpu.VMEM((tm, d), BF16)],
        compiler_params=_params(("arbitrary", "arbitrary")),
        name="norm_matmul",
    )(*args)
    return res if has_small else res[0]


def _prefix_sum_rows(tri_bf, x):
    p1 = x.astype(BF16)
    r1 = x - p1.astype(F32)
    p2 = r1.astype(BF16)
    p3 = (r1 - p2.astype(F32)).astype(BF16)
    dot = lambda p: jnp.dot(tri_bf, p, preferred_element_type=F32)
    return (dot(p1) + dot(p2)) + dot(p3)


def _gla_kernel(q_ref, k_ref, v_ref, r_ref, sm_ref, au_ref, ab_ref, gn_ref, out_ref, st_ref,
                *, chunk, nsub, scale, hps, dk, dv):
    @pl.when(pl.program_id(2) == 0)
    def _():
        st_ref[...] = jnp.zeros_like(st_ref)

    row = lax.broadcasted_iota(jnp.int32, (chunk, chunk), 0)
    col = lax.broadcasted_iota(jnp.int32, (chunk, chunk), 1)
    causal = row >= col
    tri = causal.astype(BF16)
    key_row = lax.broadcasted_iota(jnp.int32, (chunk, 1), 0)
    for s in range(nsub):
        rows = pl.ds(s * chunk, chunk)
        ga = sm_ref[rows, :].astype(BF16)
        for hh in range(hps):
            kc = slice(hh * dk, (hh + 1) * dk)
            vc = slice(hh * dv, (hh + 1) * dv)
            z = jnp.dot(ga, au_ref[:, kc], preferred_element_type=F32)
            la = _log_sigmoid(z + ab_ref[:, kc]) * (1.0 / GLA_GATE_NORM)
            b = _prefix_sum_rows(tri, la)
            b_last = b[chunk - 1:chunk, :]
            q = q_ref[rows, kc].astype(F32)
            k = k_ref[rows, kc].astype(F32)
            v = v_ref[rows, vc]
            qd = (q * (scale * jnp.exp(b))).astype(BF16)
            kl = (k * jnp.exp(b_last - b)).astype(BF16)
            blocks = []
            for gi in range(chunk // GLA_SUB):
                grp = slice(gi * GLA_SUB, (gi + 1) * GLA_SUB)
                if gi == 0:
                    q_fac, k_arg = b[grp], -b
                else:
                    r = b[gi * GLA_SUB - 1:gi * GLA_SUB, :]
                    q_fac, k_arg = b[grp] - r, r - b
                k_arg = jnp.where(key_row < (gi + 1) * GLA_SUB, k_arg, -jnp.inf)
                qg = (q[grp] * (scale * jnp.exp(q_fac))).astype(BF16)
                kg = (k * jnp.exp(k_arg)).astype(BF16)
                blocks.append(lax.dot_general(qg, kg, NT_DIMS, preferred_element_type=F32))
            sc = jnp.where(causal, jnp.concatenate(blocks, axis=0), 0.0).astype(BF16)
            st = st_ref[hh]
            o = jnp.dot(sc, v, preferred_element_type=F32) + lax.dot_general(
                qd, st.astype(BF16), NT_DIMS, preferred_element_type=F32)
            st_ref[hh] = st * jnp.exp(b_last) + lax.dot_general(
                v, kl, TN_DIMS, preferred_element_type=F32)
            r = r_ref[rows, vc].astype(F32)
            out_ref[rows, vc] = (_rms(o, gn_ref[...]) * (r * jax.nn.sigmoid(r))).astype(out_ref.dtype)


def _gla(proj, small, alpha_up_pad, alpha_bias, out_norm_g, *, batch, seq, dk, dv, offs, rows, hps):
    t = batch * seq
    nblk = seq // rows
    h = GLA_HEADS
    wk, wv = hps * dk, hps * dv
    qo, ko, vo, ro = (offs[0] // wk, offs[1] // wk, offs[2] // wv, offs[3] // wv)
    rmap = lambda b, hh, c: b * nblk + c
    return pl.pallas_call(
        functools.partial(_gla_kernel, chunk=GLA_CHUNK, nsub=rows // GLA_CHUNK, scale=dk ** -0.5,
                          hps=hps, dk=dk, dv=dv),
        grid=(batch, h // hps, nblk),
        in_specs=[
            pl.BlockSpec((rows, wk), lambda b, hh, c: (rmap(b, hh, c), qo + hh)),
            pl.BlockSpec((rows, wk), lambda b, hh, c: (rmap(b, hh, c), ko + hh)),
            pl.BlockSpec((rows, wv), lambda b, hh, c: (rmap(b, hh, c), vo + hh)),
            pl.BlockSpec((rows, wv), lambda b, hh, c: (rmap(b, hh, c), ro + hh)),
            pl.BlockSpec((rows, LANES), lambda b, hh, c: (rmap(b, hh, c), 0)),
            pl.BlockSpec((LANES, wk), lambda b, hh, c: (0, hh)),
            pl.BlockSpec((1, wk), lambda b, hh, c: (0, hh)),
            pl.BlockSpec((1, dv), lambda b, hh, c: (0, 0)),
        ],
        out_specs=pl.BlockSpec((rows, wv), lambda b, hh, c: (rmap(b, hh, c), hh)),
        out_shape=jax.ShapeDtypeStruct((t, h * dv), BF16),
        scratch_shapes=[pltpu.VMEM((hps, dv, dk), F32)],
        compiler_params=_params(("arbitrary", "arbitrary", "arbitrary")),
        name="gla",
    )(proj, proj, proj, proj, small, alpha_up_pad, alpha_bias, out_norm_g)


def _fcum_kernel(sm_ref, bias_ref, out_ref, carry_ref, *, tb):
    @pl.when(pl.program_id(1) == 0)
    def _():
        carry_ref[...] = jnp.zeros_like(carry_ref)

    row = lax.broadcasted_iota(jnp.int32, (tb, tb), 0)
    col = lax.broadcasted_iota(jnp.int32, (tb, tb), 1)
    tri = (row >= col).astype(BF16)
    lf = _log_sigmoid(sm_ref[...] + bias_ref[...])
    c = _prefix_sum_rows(tri, lf) + carry_ref[...]
    out_ref[...] = c
    carry_ref[...] = c[tb - 1:tb, :]


def _fcum(small, bias_pad, *, batch, seq, tb):
    nb = seq // tb
    return pl.pallas_call(
        functools.partial(_fcum_kernel, tb=tb),
        grid=(batch, nb),
        in_specs=[pl.BlockSpec((tb, LANES), lambda b, i: (b * nb + i, 0)),
                  pl.BlockSpec((1, LANES), lambda b, i: (0, 0))],
        out_specs=pl.BlockSpec((tb, LANES), lambda b, i: (b * nb + i, 0)),
        out_shape=jax.ShapeDtypeStruct(small.shape, F32),
        scratch_shapes=[pltpu.VMEM((1, LANES), F32)],
        compiler_params=_params(("arbitrary", "arbitrary")),
        name="forget_cumsum",
    )(small, bias_pad)


def _fox_kernel(qt_ref, kt_ref, q_ref, k_ref, v_ref, c_ref, out_ref, m_ref, mn_ref, a_ref, l_ref,
                acc_ref, s_ref, p_ref, *, tq, rb, grp):
    p = pl.program_id(2)
    qi = qt_ref[p]
    ki = kt_ref[p]

    @pl.when(ki == 0)
    def _():
        m_ref[...] = jnp.full_like(m_ref, -jnp.inf)
        l_ref[...] = jnp.zeros_like(l_ref)
        acc_ref[...] = jnp.zeros_like(acc_ref)

    def update(diag):
        c = c_ref[0]

        def ncols(g):
            return (g + 1) * grp if diag else tq

        def qk(g):
            n = ncols(g)
            s_ref[g % 2, :, 0:n] = lax.dot_general(q_ref[g * grp:(g + 1) * grp, :], k_ref[0:n, :],
                                                   NT_DIMS, preferred_element_type=F32)

        def pv(g):
            n = ncols(g)
            rows = slice(g * grp, (g + 1) * grp)
            acc_ref[rows, :] += jnp.dot(p_ref[g % 2, :, 0:n], v_ref[0:n, :],
                                        preferred_element_type=F32)

        def softmax(g):
            buf = g % 2
            strips = []
            for st in range(grp // rb):
                r0 = g * grp + st * rb
                visible = (r0 + rb - 1) // LANES + 1 if diag else tq // LANES
                strips.append((slice(st * rb, (st + 1) * rb), slice(r0, r0 + rb), r0, visible))

            def logits(rows_l, r0, t):
                cols = slice(t * LANES, (t + 1) * LANES)
                s = s_ref[buf, rows_l, cols] - c[:, cols]
                if diag and (t + 1) * LANES - 1 > r0:
                    row = r0 + lax.broadcasted_iota(jnp.int32, (rb, LANES), 0)
                    col = t * LANES + lax.broadcasted_iota(jnp.int32, (rb, LANES), 1)
                    s = jnp.where(row >= col, s, -jnp.inf)
                return s

            for rows_l, rows, r0, visible in strips:
                part = logits(rows_l, r0, 0)
                for t in range(1, visible):
                    part = jnp.maximum(part, logits(rows_l, r0, t))
                m_old = m_ref[rows, :]
                m_new = jnp.maximum(m_old, jnp.max(part, axis=-1, keepdims=True))
                a_ref[rows, :] = jnp.exp2(m_old - m_new)
                mn_ref[rows, :] = m_new
            for rows_l, rows, r0, visible in strips:
                m_new = mn_ref[rows, :]
                m_ref[rows, :] = m_new
                part = jnp.zeros((rb, LANES), F32)
                for t in range(ncols(g) // LANES):
                    cols = slice(t * LANES, (t + 1) * LANES)
                    if t < visible:
                        pr = jnp.exp2(logits(rows_l, r0, t) - m_new)
                        part = part + pr
                        p_ref[buf, rows_l, cols] = pr.astype(BF16)
                    else:
                        p_ref[buf, rows_l, cols] = jnp.zeros((rb, LANES), BF16)
                alpha = a_ref[rows, :]
                l_ref[rows, :] = alpha * l_ref[rows, :] + part
                acc_ref[rows, :] = alpha * acc_ref[rows, :]

        ngrp = tq // grp
        qk(0)
        for g in range(ngrp):
            if g + 1 < ngrp:
                qk(g + 1)
            softmax(g)
            if g >= 1:
                pv(g - 1)
        pv(ngrp - 1)

    @pl.when(ki < qi)
    def _():
        update(False)

    @pl.when(ki == qi)
    def _():
        update(True)
        denom = jnp.sum(l_ref[...], axis=-1, keepdims=True)
        out_ref[...] = (acc_ref[...] / denom).astype(out_ref.dtype)


def _fox(proj, c_row, *, batch, seq, dh, offs, tq):
    t = batch * seq
    nq = seq // tq
    pairs = [(a, b) for a in range(nq) for b in range(a + 1)]
    qt = jnp.asarray([a for a, _ in pairs], jnp.int32)
    kt = jnp.asarray([b for _, b in pairs], jnp.int32)
    qo, ko, vo = (o // dh for o in offs)
    grp = min(tq, FOX_GROUP)
    grid_spec = pltpu.PrefetchScalarGridSpec(
        num_scalar_prefetch=2,
        grid=(batch, FOX_HEADS, len(pairs)),
        in_specs=[
            pl.BlockSpec((tq, dh), lambda b, h, p, qt, kt: (b * nq + qt[p], qo + h)),
            pl.BlockSpec((tq, dh), lambda b, h, p, qt, kt: (b * nq + kt[p], ko + h)),
            pl.BlockSpec((tq, dh), lambda b, h, p, qt, kt: (b * nq + kt[p], vo + h)),
            pl.BlockSpec((1, 1, tq), lambda b, h, p, qt, kt: (b * FOX_HEADS + h, 0, kt[p])),
        ],
        out_specs=pl.BlockSpec((tq, dh), lambda b, h, p, qt, kt: (b * nq + qt[p], h)),
        scratch_shapes=[pltpu.VMEM((tq, LANES), F32)] * 4 + [
                        pltpu.VMEM((tq, dh), F32), pltpu.VMEM((2, grp, tq), F32),
                        pltpu.VMEM((2, grp, tq), BF16)],
    )
    return pl.pallas_call(
        functools.partial(_fox_kernel, tq=tq, rb=FOX_STRIP, grp=grp),
        grid_spec=grid_spec,
        out_shape=jax.ShapeDtypeStruct((t, FOX_HEADS * dh), BF16),
        compiler_params=_params(("arbitrary", "arbitrary", "arbitrary")),
        name="fox_attention",
    )(qt, kt, proj, proj, proj, c_row)


def _merge_kernel(a_ref, f_ref, ga_ref, gf_ref, wa_ref, wf_ref, out_ref):
    ya = jnp.dot(a_ref[...], wa_ref[...], preferred_element_type=F32)
    yf = jnp.dot(f_ref[...], wf_ref[...], preferred_element_type=F32)
    out_ref[...] = (jax.nn.sigmoid(ga_ref[...].astype(F32)) * ya
                    + jax.nn.sigmoid(gf_ref[...].astype(F32)) * yf).astype(out_ref.dtype)


def _merge(o_gla, o_fox, proj, w_gla, w_fox, *, offs, tm, tn):
    m, kdim = o_gla.shape
    n = w_gla.shape[1]
    go, fo = offs[0] // tn, offs[1] // tn
    return pl.pallas_call(
        _merge_kernel,
        grid=(m // tm, n // tn),
        in_specs=[
            pl.BlockSpec((tm, kdim), lambda i, j: (i, 0)),
            pl.BlockSpec((tm, o_fox.shape[1]), lambda i, j: (i, 0)),
            pl.BlockSpec((tm, tn), lambda i, j: (i, go + j)),
            pl.BlockSpec((tm, tn), lambda i, j: (i, fo + j)),
            pl.BlockSpec((kdim, tn), lambda i, j: (0, j)),
            pl.BlockSpec((o_fox.shape[1], tn), lambda i, j: (0, j)),
        ],
        out_specs=pl.BlockSpec((tm, tn), lambda i, j: (i, j)),
        out_shape=jax.ShapeDtypeStruct((m, n), BF16),
        compiler_params=_params(("arbitrary", "arbitrary")),
        name="gated_merge",
    )(o_gla, o_fox, proj, proj, w_gla, w_fox)


def _resid_matmul_kernel(a_ref, w_ref, res_ref, out_ref):
    out_ref[...] = res_ref[...] + jnp.dot(a_ref[...], w_ref[...], preferred_element_type=F32)


def _resid_matmul(a, w, res, *, tm, tn):
    m, kdim = a.shape
    n = w.shape[1]
    return pl.pallas_call(
        _resid_matmul_kernel,
        grid=(m // tm, n // tn),
        in_specs=[pl.BlockSpec((tm, kdim), lambda i, j: (i, 0)),
                  pl.BlockSpec((kdim, tn), lambda i, j: (0, j)),
                  pl.BlockSpec((tm, tn), lambda i, j: (i, j))],
        out_specs=pl.BlockSpec((tm, tn), lambda i, j: (i, j)),
        out_shape=jax.ShapeDtypeStruct((m, n), F32),
        compiler_params=_params(("arbitrary", "arbitrary")),
        name="resid_matmul",
    )(a, w, res)


def _memory_kernel(x_ref, gn_ref, wq_ref, gq_ref, k_ref, v_ref, wo_ref, gm_ref, wr_ref, br_ref,
                   x2_ref, rl_ref, *, dh):
    x1 = x_ref[...]
    hb = _rms(x1, gn_ref[...]).astype(BF16)
    q = jnp.dot(hb, wq_ref[...], preferred_element_type=F32)
    outs = []
    for h in range(MEM_HEADS):
        sl = slice(h * dh, (h + 1) * dh)
        qn = _rms(q[:, sl], gq_ref[...]).astype(BF16)
        s = lax.dot_general(qn, k_ref[:, sl], NT_DIMS, preferred_element_type=F32)
        s = s - jnp.max(s, axis=-1, keepdims=True)
        p = jnp.exp(s)
        p = p / jnp.sum(p, axis=-1, keepdims=True)
        outs.append(jnp.dot(p.astype(BF16), v_ref[:, sl], preferred_element_type=F32).astype(BF16))
    o = jnp.concatenate(outs, axis=-1)
    x2 = x1 + jnp.dot(o, wo_ref[...], preferred_element_type=F32)
    x2_ref[...] = x2
    xs = _rms(x2, gm_ref[...]).astype(BF16)
    rl_ref[...] = jnp.dot(xs, wr_ref[...], preferred_element_type=F32) + br_ref[...]


def _memory(x1, mem_norm_g, wq, gq, kv, wo, moe_norm_g, wr_pad, br_pad, *, batch, seq,
            mem_len, tm):
    t, d = x1.shape
    w = wq.shape[1]
    dh = w // MEM_HEADS
    nblk = seq // tm
    const = lambda i: (0, 0)
    return pl.pallas_call(
        functools.partial(_memory_kernel, dh=dh),
        grid=(t // tm,),
        in_specs=[
            pl.BlockSpec((tm, d), lambda i: (i, 0)),
            pl.BlockSpec((1, d), const),
            pl.BlockSpec((d, w), const),
            pl.BlockSpec((1, dh), const),
            pl.BlockSpec((mem_len, w), lambda i: (i // nblk, 0)),
            pl.BlockSpec((mem_len, w), lambda i: (i // nblk, 1)),
            pl.BlockSpec((w, d), const),
            pl.BlockSpec((1, d), const),
            pl.BlockSpec((d, LANES), const),
            pl.BlockSpec((1, LANES), const),
        ],
        out_specs=[pl.BlockSpec((tm, d), lambda i: (i, 0)),
                   pl.BlockSpec((tm, LANES), lambda i: (i, 0))],
        out_shape=[jax.ShapeDtypeStruct((t, d), F32), jax.ShapeDtypeStruct((t, LANES), F32)],
        compiler_params=_params(("arbitrary",)),
        name="memory_block",
    )(x1, mem_norm_g, wq, gq, kv, kv, wo, moe_norm_g, wr_pad, br_pad)


def _route_kernel(rl_ref, route_ref, counts_ref, carry_ref, *, tr, n_experts):
    @pl.when(pl.program_id(0) == 0)
    def _():
        carry_ref[...] = jnp.zeros_like(carry_ref)

    lane = lax.broadcasted_iota(jnp.int32, (tr, LANES), 1)
    lg = jnp.where(lane < n_experts, rl_ref[...], -jnp.inf)
    vals, hots = [], []
    for _ in range(TOP_K):
        mx = jnp.max(lg, axis=-1, keepdims=True)
        idx = jnp.min(jnp.where(lg == mx, lane, LANES), axis=-1, keepdims=True)
        hot = lane == idx
        vals.append(mx)
        hots.append(hot)
        lg = jnp.where(hot, -jnp.inf, lg)
    exps = [jnp.exp(v - vals[0]) for v in vals]
    denom = exps[0]
    for e in exps[1:]:
        denom = denom + e
    onehot = hots[0]
    for hsel in hots[1:]:
        onehot = jnp.logical_or(onehot, hsel)
    onehot_f = onehot.astype(F32)
    row = lax.broadcasted_iota(jnp.int32, (tr, tr), 0)
    col = lax.broadcasted_iota(jnp.int32, (tr, tr), 1)
    strict = (row > col).astype(BF16)
    before = jnp.dot(strict, onehot_f.astype(BF16), preferred_element_type=F32) + carry_ref[...]
    route = jnp.zeros((tr, LANES), F32)
    for k in range(TOP_K):
        idx_f = jnp.sum(jnp.where(hots[k], lane, 0), axis=-1, keepdims=True).astype(F32)
        rank = jnp.sum(jnp.where(hots[k], before, 0.0), axis=-1, keepdims=True)
        route = jnp.where(lane == k, idx_f, route)
        route = jnp.where(lane == TOP_K + k, exps[k] / denom, route)
        route = jnp.where(lane == 2 * TOP_K + k, rank, route)
    route_ref[...] = route
    total = carry_ref[...] + jnp.sum(onehot_f, axis=0, keepdims=True)
    carry_ref[...] = total
    counts_ref[...] = jnp.broadcast_to(total, counts_ref.shape)


def _route(rlogits, *, n_experts, tr):
    t = rlogits.shape[0]
    return pl.pallas_call(
        functools.partial(_route_kernel, tr=tr, n_experts=n_experts),
        grid=(t // tr,),
        in_specs=[pl.BlockSpec((tr, LANES), lambda i: (i, 0))],
        out_specs=[pl.BlockSpec((tr, LANES), lambda i: (i, 0)),
                   pl.BlockSpec((8, LANES), lambda i: (0, 0))],
        out_shape=[jax.ShapeDtypeStruct((t, LANES), F32), jax.ShapeDtypeStruct((8, LANES), F32)],
        scratch_shapes=[pltpu.VMEM((1, LANES), F32)],
        compiler_params=_params(("arbitrary",)),
        name="route_topk",
    )(rlogits)


def _gather_kernel(tok_ref, nused_ref, x_hbm, g_ref, out_ref, buf, sem, *, rows):
    i = pl.program_id(0)
    nused = nused_ref[0]

    def issue(blk, slot):
        def body(r, carry):
            tok = tok_ref[blk * rows + r]
            pltpu.make_async_copy(x_hbm.at[pl.ds(tok, 1)], buf.at[slot, pl.ds(r, 1)],
                                  sem.at[slot]).start()
            return carry
        lax.fori_loop(0, rows, body, 0, unroll=8)

    @pl.when(i == 0)
    def _():
        issue(0, 0)

    @pl.when(i + 1 < nused)
    def _():
        issue(i + 1, (i + 1) % 2)

    @pl.when(i < nused)
    def _():
        slot = i % 2
        pltpu.make_async_copy(x_hbm.at[pl.ds(0, rows)], buf.at[slot], sem.at[slot]).wait()
        out_ref[...] = _rms(buf[slot], g_ref[...]).astype(out_ref.dtype)

    @pl.when(i >= nused)
    def _():
        out_ref[...] = jnp.zeros_like(out_ref)


def _gather_norm(slot_tok, nused, x2, g, *, rows):
    p = slot_tok.shape[0]
    d = x2.shape[1]
    grid_spec = pltpu.PrefetchScalarGridSpec(
        num_scalar_prefetch=2,
        grid=(p // rows,),
        in_specs=[pl.BlockSpec(memory_space=pl.ANY),
                  pl.BlockSpec((1, d), lambda i, tok, nu: (0, 0))],
        out_specs=pl.BlockSpec((rows, d), lambda i, tok, nu: (i, 0)),
        scratch_shapes=[pltpu.VMEM((2, rows, d), F32), pltpu.SemaphoreType.DMA((2,))],
    )
    return pl.pallas_call(
        functools.partial(_gather_kernel, rows=rows),
        grid_spec=grid_spec,
        out_shape=jax.ShapeDtypeStruct((p, d), BF16),
        compiler_params=_params(("arbitrary",)),
        name="moe_gather",
    )(slot_tok, nused, x2, g)


def _expert_up_kernel(blk_ref, tile_ref, exp_ref, flag_ref, x_ref, wg_ref, wl_ref, bg_ref, bl_ref,
                      out_ref, wg_bf, wl_bf):
    flag = flag_ref[pl.program_id(0)]

    @pl.when(flag >= 2)
    def _():
        wg_bf[...] = wg_ref[...].astype(BF16)
        wl_bf[...] = wl_ref[...].astype(BF16)

    @pl.when(flag % 2 == 1)
    def _():
        x = x_ref[...]
        glu = jnp.dot(x, wg_bf[...], preferred_element_type=F32) + bg_ref[...]
        lin = jnp.dot(x, wl_bf[...], preferred_element_type=F32) + bl_ref[...]
        glu = jnp.minimum(glu, SWIGLU_LIMIT)
        lin = jnp.clip(lin, -SWIGLU_LIMIT, SWIGLU_LIMIT)
        out_ref[...] = (glu * jax.nn.sigmoid(SWIGLU_ALPHA * glu) * (lin + 1.0)).astype(out_ref.dtype)

    @pl.when(flag % 2 == 0)
    def _():
        out_ref[...] = jnp.zeros_like(out_ref)


def _expert_up(tables, xg, w_up, b_up, *, tm, tf):
    p, d = xg.shape
    n_e, _, ff2 = w_up.shape
    ff = ff2 // 2
    nt = ff // tf
    nsteps = tables[0].shape[0]
    grid_spec = pltpu.PrefetchScalarGridSpec(
        num_scalar_prefetch=4,
        grid=(nsteps,),
        in_specs=[
            pl.BlockSpec((tm, d), lambda s, blk, tile, ex, fl: (blk[s], 0)),
            pl.BlockSpec((None, d, tf), lambda s, blk, tile, ex, fl: (ex[s], 0, tile[s])),
            pl.BlockSpec((None, d, tf), lambda s, blk, tile, ex, fl: (ex[s], 0, nt + tile[s])),
            pl.BlockSpec((None, 1, tf), lambda s, blk, tile, ex, fl: (ex[s], 0, tile[s])),
            pl.BlockSpec((None, 1, tf), lambda s, blk, tile, ex, fl: (ex[s], 0, nt + tile[s])),
        ],
        out_specs=pl.BlockSpec((tm, tf), lambda s, blk, tile, ex, fl: (blk[s], tile[s])),
        scratch_shapes=[pltpu.VMEM((d, tf), BF16), pltpu.VMEM((d, tf), BF16)],
    )
    return pl.pallas_call(
        _expert_up_kernel,
        grid_spec=grid_spec,
        out_shape=jax.ShapeDtypeStruct((p, ff), BF16),
        compiler_params=_params(("arbitrary",)),
        name="expert_up",
    )(*tables, xg, w_up, w_up, b_up, b_up)


def _expert_down_kernel(blk_ref, tile_ref, exp_ref, flag_ref, a_ref, w_ref, b_ref, out_ref, w_bf):
    flag = flag_ref[pl.program_id(0)]

    @pl.when(flag >= 2)
    def _():
        w_bf[...] = w_ref[...].astype(BF16)

    @pl.when(flag % 2 == 1)
    def _():
        out_ref[...] = jnp.dot(a_ref[...], w_bf[...], preferred_element_type=F32) + b_ref[...]

    @pl.when(flag % 2 == 0)
    def _():
        out_ref[...] = jnp.zeros_like(out_ref)


def _expert_down(tables, act, w_down, b_down, *, tm, tn):
    p, ff = act.shape
    d = w_down.shape[2]
    nsteps = tables[0].shape[0]
    grid_spec = pltpu.PrefetchScalarGridSpec(
        num_scalar_prefetch=4,
        grid=(nsteps,),
        in_specs=[
            pl.BlockSpec((tm, ff), lambda s, blk, tile, ex, fl: (blk[s], 0)),
            pl.BlockSpec((None, ff, tn), lambda s, blk, tile, ex, fl: (ex[s], 0, tile[s])),
            pl.BlockSpec((None, 1, tn), lambda s, blk, tile, ex, fl: (ex[s], 0, tile[s])),
        ],
        out_specs=pl.BlockSpec((tm, tn), lambda s, blk, tile, ex, fl: (blk[s], tile[s])),
        scratch_shapes=[pltpu.VMEM((ff, tn), BF16)],
    )
    return pl.pallas_call(
        _expert_down_kernel,
        grid_spec=grid_spec,
        out_shape=jax.ShapeDtypeStruct((p, d), F32),
        compiler_params=_params(("arbitrary",)),
        name="expert_down",
    )(*tables, act, w_down, b_down)


def _expert_schedule(counts, *, tm, n_tiles, n_blocks):
    nb = (counts + tm - 1) // tm
    bend = jnp.cumsum(nb)
    bstart = bend - nb
    steps_e = nb * n_tiles
    cs = jnp.cumsum(steps_e)
    total = cs[-1]
    s = jnp.arange(n_blocks * n_tiles, dtype=jnp.int32)
    sc = jnp.minimum(s, total - 1)
    e = jnp.sum((cs[None, :] <= sc[:, None]).astype(jnp.int32), axis=1)
    e = jnp.minimum(e, counts.shape[0] - 1)
    r = sc - (cs[e] - steps_e[e])
    nbe = jnp.maximum(nb[e], 1)
    valid = s < total
    n_unused = jnp.maximum(n_blocks - bend[-1], 1)
    u = jnp.maximum(s - total, 0)
    tile = jnp.where(valid, r // nbe, u // n_unused)
    blk = jnp.where(valid, bstart[e] + r % nbe, bend[-1] + u % n_unused)
    first = jnp.logical_and(valid, r % nbe == 0)
    flag = valid.astype(jnp.int32) + 2 * first.astype(jnp.int32)
    return (blk.astype(jnp.int32), tile.astype(jnp.int32), e, flag)


def _combine_kernel(slot_ref, x_ref, route_ref, y_hbm, out_ref, buf, sem, *, tc):
    i = pl.program_id(0)
    n = pl.num_programs(0)

    def issue(blk, bslot):
        def body(r, carry):
            for k in range(TOP_K):
                src = slot_ref[(blk * tc + r) * TOP_K + k]
                pltpu.make_async_copy(y_hbm.at[pl.ds(src, 1)], buf.at[bslot, k, pl.ds(r, 1)],
                                      sem.at[bslot]).start()
            return carry
        lax.fori_loop(0, tc, body, 0, unroll=4)

    @pl.when(i == 0)
    def _():
        issue(0, 0)

    @pl.when(i + 1 < n)
    def _():
        issue(i + 1, (i + 1) % 2)

    bslot = i % 2
    for k in range(TOP_K):
        pltpu.make_async_copy(y_hbm.at[pl.ds(0, tc)], buf.at[bslot, k], sem.at[bslot]).wait()
    acc = x_ref[...]
    for k in range(TOP_K):
        acc = acc + route_ref[:, TOP_K + k:TOP_K + k + 1] * buf[bslot, k]
    out_ref[...] = acc


def _combine(slot_flat, x2, route, y, *, tc):
    t, d = x2.shape
    grid_spec = pltpu.PrefetchScalarGridSpec(
        num_scalar_prefetch=1,
        grid=(t // tc,),
        in_specs=[pl.BlockSpec((tc, d), lambda i, sl: (i, 0)),
                  pl.BlockSpec((tc, LANES), lambda i, sl: (i, 0)),
                  pl.BlockSpec(memory_space=pl.ANY)],
        out_specs=pl.BlockSpec((tc, d), lambda i, sl: (i, 0)),
        scratch_shapes=[pltpu.VMEM((2, TOP_K, tc, d), F32), pltpu.SemaphoreType.DMA((2,))],
    )
    return pl.pallas_call(
        functools.partial(_combine_kernel, tc=tc),
        grid_spec=grid_spec,
        out_shape=jax.ShapeDtypeStruct((t, d), F32),
        compiler_params=_params(("arbitrary",)),
        name="moe_combine",
    )(slot_flat, x2, route, y)


def _pad_lanes(a, offset=0):
    width = a.shape[-1]
    return jnp.pad(a, [(0, 0)] * (a.ndim - 1) + [(offset, LANES - offset - width)])


def _layer(x2d, mem2d, lw, *, batch, seq, mem_len):
    t, d = x2d.shape
    gla_qk = lw["gla_alpha_up"].shape[1]
    gla_rank = lw["gla_alpha_up"].shape[0]
    gla_v = lw["w_branch_gla"].shape[0]
    fox_w = lw["w_branch_fox"].shape[0]
    dk, dv = gla_qk // GLA_HEADS, gla_v // GLA_HEADS
    dh = fox_w // FOX_HEADS
    mem_w = lw["mem_w_q"].shape[1]
    mem_dh = mem_w // MEM_HEADS
    n_experts = lw["w_router"].shape[1]
    ff = lw["w_expert_down"].shape[1]

    sizes = (gla_qk, gla_qk, gla_v, gla_v, gla_rank, fox_w, fox_w, fox_w, FOX_HEADS, d, d)
    starts = [0]
    for sz in sizes:
        starts.append(starts[-1] + sz)
    w_in = lw["w_in"]
    seg = lambda i: w_in[:, starts[i]:starts[i + 1]]
    big_ids = (0, 1, 2, 3, 5, 6, 7, 9, 10)
    w_big = jnp.concatenate([seg(i).astype(BF16) for i in big_ids], axis=1)
    w_small = _pad_lanes(jnp.concatenate([seg(4), seg(8)], axis=1)).astype(BF16)
    off = {}
    acc = 0
    for i in big_ids:
        off[i] = acc
        acc += sizes[i]
    n_big = acc
    tn1 = 1024
    colscale = jnp.ones((1, n_big), F32)
    colscale = colscale.at[:, off[5]:off[5] + fox_w].set(
        jnp.tile(lw["fox_q_norm_g"] * (dh ** -0.5 * LOG2E), FOX_HEADS)[None])
    colscale = colscale.at[:, off[6]:off[6] + fox_w].set(jnp.tile(lw["fox_k_norm_g"], FOX_HEADS)[None])
    proj, small = _norm_matmul(
        x2d, lw["mix_norm_g"][None], w_big, colscale, w_small,
        tm=1024, tn=tn1, norm_lo=off[5] // tn1, norm_hi=(off[6] + fox_w) // tn1)

    alpha_up_pad = jnp.pad(lw["gla_alpha_up"], ((0, LANES - gla_rank), (0, 0))).astype(BF16)
    o_gla = _gla(proj, small, alpha_up_pad, lw["gla_alpha_bias"][None], lw["gla_out_norm_g"][None],
                 batch=batch, seq=seq, dk=dk, dv=dv, offs=(off[0], off[1], off[2], off[3]), rows=256, hps=4)

    c = _fcum(small, _pad_lanes(lw["fox_f_bias"][None], gla_rank), batch=batch, seq=seq, tb=512)
    c_row = c[:, gla_rank:gla_rank + FOX_HEADS].reshape(batch, seq, FOX_HEADS)
    c_row = c_row.transpose(0, 2, 1).reshape(batch * FOX_HEADS, 1, seq) * LOG2E
    o_fox = _fox(proj, c_row, batch=batch, seq=seq, dh=dh, offs=(off[5], off[6], off[7]),
                 tq=min(seq, 2048))

    merged = _merge(o_gla, o_fox, proj, lw["w_branch_gla"].astype(BF16),
                    lw["w_branch_fox"].astype(BF16), offs=(off[9], off[10]), tm=512, tn=1024)
    x1 = _resid_matmul(merged, lw["w_out"].astype(BF16), x2d, tm=1024, tn=1024)

    kv_scale = jnp.concatenate([jnp.tile(lw["mem_k_norm_g"], MEM_HEADS),
                                jnp.ones((mem_w,), F32)])[None]
    kv = _norm_matmul(mem2d, lw["mem_kv_norm_g"][None], lw["mem_w_kv"].astype(BF16), kv_scale, None,
                      tm=mem2d.shape[0], tn=mem_w, norm_lo=0, norm_hi=1)
    x2, rlogits = _memory(
        x1, lw["mem_norm_g"][None], lw["mem_w_q"].astype(BF16),
        (lw["mem_q_norm_g"] * (mem_dh ** -0.5))[None], kv,
        lw["mem_w_o"].astype(BF16), lw["moe_norm_g"][None],
        _pad_lanes(lw["w_router"]).astype(BF16), _pad_lanes(lw["b_router"][None]),
        batch=batch, seq=seq, mem_len=mem_len, tm=512)

    route, counts_f = _route(rlogits, n_experts=n_experts, tr=512)
    tm_e = 512
    a_total = t * TOP_K
    n_blocks = a_total // tm_e + n_experts
    p_rows = n_blocks * tm_e
    counts = counts_f[0, :n_experts].astype(jnp.int32)
    padded = (counts + tm_e - 1) // tm_e * tm_e
    pstart = jnp.cumsum(padded) - padded
    idx = route[:, :TOP_K].astype(jnp.int32)
    rank = route[:, 2 * TOP_K:3 * TOP_K].astype(jnp.int32)
    hit = idx[:, :, None] == jnp.arange(n_experts, dtype=jnp.int32)
    slot = (jnp.sum(jnp.where(hit, pstart, 0), axis=-1) + rank).reshape(a_total)
    slot_tok = (jnp.arange(p_rows, dtype=jnp.int32) % t).at[slot].set(
        jnp.arange(a_total, dtype=jnp.int32) // TOP_K)

    tf = 1024
    g_rows = 256
    nused = (jnp.sum(padded) // g_rows).astype(jnp.int32).reshape(1)
    xg = _gather_norm(slot_tok, nused, x2, lw["moe_norm_g"][None], rows=g_rows)
    up_tables = _expert_schedule(counts, tm=tm_e, n_tiles=ff // tf, n_blocks=n_blocks)
    act = _expert_up(up_tables, xg, lw["w_expert_up"], lw["b_expert_up"][:, None, :], tm=tm_e, tf=tf)
    tn_d = 1024
    down_tables = _expert_schedule(counts, tm=tm_e, n_tiles=d // tn_d, n_blocks=n_blocks)
    y = _expert_down(down_tables, act, lw["w_expert_down"], lw["b_expert_down"][:, None, :],
                     tm=tm_e, tn=tn_d)
    return _combine(slot, x2, route, y, tc=128)


def kernel(x, mem, mix_norm_g, w_in, gla_alpha_up, gla_alpha_bias, gla_out_norm_g, fox_f_bias, fox_q_norm_g, fox_k_norm_g, w_branch_gla, w_branch_fox, w_out, mem_norm_g, mem_kv_norm_g, mem_w_q, mem_w_kv, mem_q_norm_g, mem_k_norm_g, mem_w_o, moe_norm_g, w_router, b_router, w_expert_up, b_expert_up, w_expert_down, b_expert_down):
    weights = dict(
        mix_norm_g=mix_norm_g, w_in=w_in, gla_alpha_up=gla_alpha_up, gla_alpha_bias=gla_alpha_bias,
        gla_out_norm_g=gla_out_norm_g, fox_f_bias=fox_f_bias, fox_q_norm_g=fox_q_norm_g,
        fox_k_norm_g=fox_k_norm_g, w_branch_gla=w_branch_gla, w_branch_fox=w_branch_fox, w_out=w_out,
        mem_norm_g=mem_norm_g, mem_kv_norm_g=mem_kv_norm_g, mem_w_q=mem_w_q, mem_w_kv=mem_w_kv,
        mem_q_norm_g=mem_q_norm_g, mem_k_norm_g=mem_k_norm_g, mem_w_o=mem_w_o, moe_norm_g=moe_norm_g,
        w_router=w_router, b_router=b_router, w_expert_up=w_expert_up, b_expert_up=b_expert_up,
        w_expert_down=w_expert_down, b_expert_down=b_expert_down)
    batch, seq, d = x.shape
    mem_len = mem.shape[1]
    x2d = x.reshape(batch * seq, d)
    mem2d = mem.reshape(batch * mem_len, d)
    for layer in range(mix_norm_g.shape[0]):
        lw = {name: w[layer] for name, w in weights.items()}
        x2d = _layer(x2d, mem2d, lw, batch=batch, seq=seq, mem_len=mem_len)
    return x2d.reshape(batch, seq, d)
```

```python
import functools

import jax
import jax.numpy as jnp
from jax import lax
from jax.experimental import pallas as pl
from jax.experimental.pallas import tpu as pltpu

F32 = jnp.float32
BF16 = jnp.bfloat16

EPS = 1e-6
LANES = 128
SUBLANES = 8
VMEM_LIMIT = 56 * 1024 * 1024

GLA_HEADS = 4
GLA_GATE_NORM = 16.0
GLA_CHUNK = 64
GLA_SUB = 16
FOX_HEADS = 16
FOX_STRIP = 16
FOX_GROUP = 128
LOG2E = 1.4426950408889634
MEM_HEADS = 4
TOP_K = 4
SWIGLU_LIMIT = 7.0
SWIGLU_ALPHA = 1.702

NT_DIMS = (((1,), (1,)), ((), ()))
TN_DIMS = (((0,), (0,)), ((), ()))


def _params(sem, vmem=VMEM_LIMIT):
    return pltpu.CompilerParams(dimension_semantics=sem, vmem_limit_bytes=vmem)


def _rms(xf, g):
    return xf * lax.rsqrt(jnp.mean(xf * xf, axis=-1, keepdims=True) + EPS) * g


def _pack_bf16_pairs(a):
    half = a.shape[1] // 2
    bits = lambda v: lax.bitcast_convert_type(v.astype(BF16).astype(F32), jnp.uint32)
    return (bits(a[:, half:]) & jnp.uint32(0xFFFF0000)) | (bits(a[:, :half]) >> 16)


def _unpack_bf16_pairs(u):
    lo = lax.bitcast_convert_type(u << 16, F32)
    hi = lax.bitcast_convert_type(u & jnp.uint32(0xFFFF0000), F32)
    return lo, hi


def _log_sigmoid(z):
    return jnp.minimum(z, 0.0) - jnp.log1p(jnp.exp(-jnp.abs(z)))


def _norm_matmul_kernel(*refs, norm_lo, norm_hi, tn, has_small):
    if has_small:
        x_ref, g_ref, w_ref, cs_ref, ws_ref, out_ref, small_ref, h_ref = refs
    else:
        x_ref, g_ref, w_ref, cs_ref, out_ref, h_ref = refs
    j = pl.program_id(1)

    @pl.when(j == 0)
    def _():
        hb = _rms(x_ref[...], g_ref[...]).astype(BF16)
        h_ref[...] = hb
        if has_small:
            small_ref[...] = jnp.dot(hb, ws_ref[...], preferred_element_type=F32)

    acc = jnp.dot(h_ref[...], w_ref[...], preferred_element_type=F32)
    is_norm = jnp.logical_and(j >= norm_lo, j < norm_hi)

    @pl.when(is_norm)
    def _():
        for c in range(tn // LANES):
            sl = slice(c * LANES, (c + 1) * LANES)
            out_ref[:, sl] = _rms(acc[:, sl], cs_ref[:, sl]).astype(out_ref.dtype)

    @pl.when(jnp.logical_not(is_norm))
    def _():
        out_ref[...] = acc.astype(out_ref.dtype)


def _norm_matmul(x, g, w, colscale, w_small, *, tm, tn, norm_lo, norm_hi):
    m, d = x.shape
    n = w.shape[1]
    has_small = w_small is not None
    in_specs = [
        pl.BlockSpec((tm, d), lambda i, j: (i, 0)),
        pl.BlockSpec((1, d), lambda i, j: (0, 0)),
        pl.BlockSpec((d, tn), lambda i, j: (0, j)),
        pl.BlockSpec((1, tn), lambda i, j: (0, j)),
    ]
    args = [x, g, w, colscale]
    out_shape = [jax.ShapeDtypeStruct((m, n), BF16)]
    out_specs = [pl.BlockSpec((tm, tn), lambda i, j: (i, j))]
    if has_small:
        in_specs.append(pl.BlockSpec((d, LANES), lambda i, j: (0, 0)))
        args.append(w_small)
        out_shape.append(jax.ShapeDtypeStruct((m, LANES), F32))
        out_specs.append(pl.BlockSpec((tm, LANES), lambda i, j: (i, 0)))
    res = pl.pallas_call(
        functools.partial(_norm_matmul_kernel, norm_lo=norm_lo, norm_hi=norm_hi, tn=tn,
                          has_small=has_small),
        grid=(m // tm, n // tn),
        in_specs=in_specs,
        out_specs=out_specs,
        out_shape=out_shape,
        scratch_shapes=[pltpu.VMEM((tm, d), BF16)],
        compiler_params=_params(("arbitrary", "arbitrary")),
        name="norm_matmul",
    )(*args)
    return res if has_small else res[0]


def _prefix_sum_rows(tri_bf, x):
    p1 = x.astype(BF16)
    r1 = x - p1.astype(F32)
    p2 = r1.astype(BF16)
    p3 = (r1 - p2.astype(F32)).astype(BF16)
    dot = lambda p: jnp.dot(tri_bf, p, preferred_element_type=F32)
    return (dot(p1) + dot(p2)) + dot(p3)


def _gla_kernel(q_ref, k_ref, v_ref, r_ref, sm_ref, au_ref, ab_ref, gn_ref, out_ref, st_ref,
                *, chunk, nsub, scale, hps, dk, dv):
    @pl.when(pl.program_id(2) == 0)
    def _():
        st_ref[...] = jnp.zeros_like(st_ref)

    row = lax.broadcasted_iota(jnp.int32, (chunk, chunk), 0)
    col = lax.broadcasted_iota(jnp.int32, (chunk, chunk), 1)
    causal = row >= col
    tri = causal.astype(BF16)
    key_row = lax.broadcasted_iota(jnp.int32, (chunk, 1), 0)
    for s in range(nsub):
        rows = pl.ds(s * chunk, chunk)
        ga = sm_ref[rows, :].astype(BF16)
        for hh in range(hps):
            kc = slice(hh * dk, (hh + 1) * dk)
            vc = slice(hh * dv, (hh + 1) * dv)
            z = jnp.dot(ga, au_ref[:, kc], preferred_element_type=F32)
            la = _log_sigmoid(z + ab_ref[:, kc]) * (1.0 / GLA_GATE_NORM)
            b = _prefix_sum_rows(tri, la)
            b_last = b[chunk - 1:chunk, :]
            q = q_ref[rows, kc].astype(F32)
            k = k_ref[rows, kc].astype(F32)
            v = v_ref[rows, vc]
            qd = (q * (scale * jnp.exp(b))).astype(BF16)
            kl = (k * jnp.exp(b_last - b)).astype(BF16)
            blocks = []
            for gi in range(chunk // GLA_SUB):
                grp = slice(gi * GLA_SUB, (gi + 1) * GLA_SUB)
                if gi == 0:
                    q_fac, k_arg = b[grp], -b
                else:
                    r = b[gi * GLA_SUB - 1:gi * GLA_SUB, :]
                    q_fac, k_arg = b[grp] - r, r - b
                k_arg = jnp.where(key_row < (gi + 1) * GLA_SUB, k_arg, -jnp.inf)
                qg = (q[grp] * (scale * jnp.exp(q_fac))).astype(BF16)
                kg = (k * jnp.exp(k_arg)).astype(BF16)
                blocks.append(lax.dot_general(qg, kg, NT_DIMS, preferred_element_type=F32))
            sc = jnp.where(causal, jnp.concatenate(blocks, axis=0), 0.0).astype(BF16)
            st = st_ref[hh]
            o = jnp.dot(sc, v, preferred_element_type=F32) + lax.dot_general(
                qd, st.astype(BF16), NT_DIMS, preferred_element_type=F32)
            st_ref[hh] = st * jnp.exp(b_last) + lax.dot_general(
                v, kl, TN_DIMS, preferred_element_type=F32)
            r = r_ref[rows, vc].astype(F32)
            out_ref[rows, vc] = (_rms(o, gn_ref[...]) * (r * jax.nn.sigmoid(r))).astype(out_ref.dtype)


def _gla(proj, small, alpha_up_pad, alpha_bias, out_norm_g, *, batch, seq, dk, dv, offs, rows, hps):
    t = batch * seq
    nblk = seq // rows
    h = GLA_HEADS
    wk, wv = hps * dk, hps * dv
    qo, ko, vo, ro = (offs[0] // wk, offs[1] // wk, offs[2] // wv, offs[3] // wv)
    rmap = lambda b, hh, c: b * nblk + c
    return pl.pallas_call(
        functools.partial(_gla_kernel, chunk=GLA_CHUNK, nsub=rows // GLA_CHUNK, scale=dk ** -0.5,
                          hps=hps, dk=dk, dv=dv),
        grid=(batch, h // hps, nblk),
        in_specs=[
            pl.BlockSpec((rows, wk), lambda b, hh, c: (rmap(b, hh, c), qo + hh)),
            pl.BlockSpec((rows, wk), lambda b, hh, c: (rmap(b, hh, c), ko + hh)),
            pl.BlockSpec((rows, wv), lambda b, hh, c: (rmap(b, hh, c), vo + hh)),
            pl.BlockSpec((rows, wv), lambda b, hh, c: (rmap(b, hh, c), ro + hh)),
            pl.BlockSpec((rows, LANES), lambda b, hh, c: (rmap(b, hh, c), 0)),
            pl.BlockSpec((LANES, wk), lambda b, hh, c: (0, hh)),
            pl.BlockSpec((1, wk), lambda b, hh, c: (0, hh)),
            pl.BlockSpec((1, dv), lambda b, hh, c: (0, 0)),
        ],
        out_specs=pl.BlockSpec((rows, wv), lambda b, hh, c: (rmap(b, hh, c), hh)),
        out_shape=jax.ShapeDtypeStruct((t, h * dv), BF16),
        scratch_shapes=[pltpu.VMEM((hps, dv, dk), F32)],
        compiler_params=_params(("arbitrary", "arbitrary", "arbitrary")),
        name="gla",
    )(proj, proj, proj, proj, small, alpha_up_pad, alpha_bias, out_norm_g)


def _fcum_kernel(sm_ref, bias_ref, out_ref, carry_ref, *, tb):
    @pl.when(pl.program_id(1) == 0)
    def _():
        carry_ref[...] = jnp.zeros_like(carry_ref)

    row = lax.broadcasted_iota(jnp.int32, (tb, tb), 0)
    col = lax.broadcasted_iota(jnp.int32, (tb, tb), 1)
    tri = (row >= col).astype(BF16)
    lf = _log_sigmoid(sm_ref[...] + bias_ref[...])
    c = _prefix_sum_rows(tri, lf) + carry_ref[...]
    out_ref[...] = c
    carry_ref[...] = c[tb - 1:tb, :]


def _fcum(small, bias_pad, *, batch, seq, tb):
    nb = seq // tb
    return pl.pallas_call(
        functools.partial(_fcum_kernel, tb=tb),
        grid=(batch, nb),
        in_specs=[pl.BlockSpec((tb, LANES), lambda b, i: (b * nb + i, 0)),
                  pl.BlockSpec((1, LANES), lambda b, i: (0, 0))],
        out_specs=pl.BlockSpec((tb, LANES), lambda b, i: (b * nb + i, 0)),
        out_shape=jax.ShapeDtypeStruct(small.shape, F32),
        scratch_shapes=[pltpu.VMEM((1, LANES), F32)],
        compiler_params=_params(("arbitrary", "arbitrary")),
        name="forget_cumsum",
    )(small, bias_pad)


def _fox_kernel(qt_ref, kt_ref, q_ref, k_ref, v_ref, c_ref, out_ref, m_ref, mn_ref, a_ref, l_ref,
                acc_ref, s_ref, p_ref, *, tq, rb, grp):
    p = pl.program_id(2)
    qi = qt_ref[p]
    ki = kt_ref[p]

    @pl.when(ki == 0)
    def _():
        m_ref[...] = jnp.full_like(m_ref, -jnp.inf)
        l_ref[...] = jnp.zeros_like(l_ref)
        acc_ref[...] = jnp.zeros_like(acc_ref)

    def update(diag):
        c = c_ref[0]

        def ncols(g):
            return (g + 1) * grp if diag else tq

        def qk(g):
            n = ncols(g)
            s_ref[g % 2, :, 0:n] = lax.dot_general(q_ref[g * grp:(g + 1) * grp, :], k_ref[0:n, :],
                                                   NT_DIMS, preferred_element_type=F32)

        def pv(g):
            n = ncols(g)
            rows = slice(g * grp, (g + 1) * grp)
            acc_ref[rows, :] += jnp.dot(p_ref[g % 2, :, 0:n], v_ref[0:n, :],
                                        preferred_element_type=F32)

        def softmax(g):
            buf = g % 2
            strips = []
            for st in range(grp // rb):
                r0 = g * grp + st * rb
                visible = (r0 + rb - 1) // LANES + 1 if diag else tq // LANES
                strips.append((slice(st * rb, (st + 1) * rb), slice(r0, r0 + rb), r0, visible))

            def logits(rows_l, r0, t):
                cols = slice(t * LANES, (t + 1) * LANES)
                s = s_ref[buf, rows_l, cols] - c[:, cols]
                if diag and (t + 1) * LANES - 1 > r0:
                    row = r0 + lax.broadcasted_iota(jnp.int32, (rb, LANES), 0)
                    col = t * LANES + lax.broadcasted_iota(jnp.int32, (rb, LANES), 1)
                    s = jnp.where(row >= col, s, -jnp.inf)
                return s

            for rows_l, rows, r0, visible in strips:
                part = logits(rows_l, r0, 0)
                for t in range(1, visible):
                    part = jnp.maximum(part, logits(rows_l, r0, t))
                m_old = m_ref[rows, :]
                m_new = jnp.maximum(m_old, jnp.max(part, axis=-1, keepdims=True))
                a_ref[rows, :] = jnp.exp2(m_old - m_new)
                mn_ref[rows, :] = m_new
            for rows_l, rows, r0, visible in strips:
                m_new = mn_ref[rows, :]
                m_ref[rows, :] = m_new
                part = jnp.zeros((rb, LANES), F32)
                for t in range(ncols(g) // LANES):
                    cols = slice(t * LANES, (t + 1) * LANES)
                    if t < visible:
                        pr = jnp.exp2(logits(rows_l, r0, t) - m_new)
                        part = part + pr
                        p_ref[buf, rows_l, cols] = pr.astype(BF16)
                    else:
                        p_ref[buf, rows_l, cols] = jnp.zeros((rb, LANES), BF16)
                alpha = a_ref[rows, :]
                l_ref[rows, :] = alpha * l_ref[rows, :] + part
                acc_ref[rows, :] = alpha * acc_ref[rows, :]

        ngrp = tq // grp
        qk(0)
        for g in range(ngrp):
            if g + 1 < ngrp:
                qk(g + 1)
            softmax(g)
            if g >= 1:
                pv(g - 1)
        pv(ngrp - 1)

    @pl.when(ki < qi)
    def _():
        update(False)

    @pl.when(ki == qi)
    def _():
        update(True)
        denom = jnp.sum(l_ref[...], axis=-1, keepdims=True)
        out_ref[...] = (acc_ref[...] / denom).astype(out_ref.dtype)


def _fox(proj, c_row, *, batch, seq, dh, offs, tq):
    t = batch * seq
    nq = seq // tq
    pairs = [(a, b) for a in range(nq) for b in range(a + 1)]
    qt = jnp.asarray([a for a, _ in pairs], jnp.int32)
    kt = jnp.asarray([b for _, b in pairs], jnp.int32)
    qo, ko, vo = (o // dh for o in offs)
    grp = min(tq, FOX_GROUP)
    grid_spec = pltpu.PrefetchScalarGridSpec(
        num_scalar_prefetch=2,
        grid=(batch, FOX_HEADS, len(pairs)),
        in_specs=[
            pl.BlockSpec((tq, dh), lambda b, h, p, qt, kt: (b * nq + qt[p], qo + h)),
            pl.BlockSpec((tq, dh), lambda b, h, p, qt, kt: (b * nq + kt[p], ko + h)),
            pl.BlockSpec((tq, dh), lambda b, h, p, qt, kt: (b * nq + kt[p], vo + h)),
            pl.BlockSpec((1, 1, tq), lambda b, h, p, qt, kt: (b * FOX_HEADS + h, 0, kt[p])),
        ],
        out_specs=pl.BlockSpec((tq, dh), lambda b, h, p, qt, kt: (b * nq + qt[p], h)),
        scratch_shapes=[pltpu.VMEM((tq, LANES), F32)] * 4 + [
                        pltpu.VMEM((tq, dh), F32), pltpu.VMEM((2, grp, tq), F32),
                        pltpu.VMEM((2, grp, tq), BF16)],
    )
    return pl.pallas_call(
        functools.partial(_fox_kernel, tq=tq, rb=FOX_STRIP, grp=grp),
        grid_spec=grid_spec,
        out_shape=jax.ShapeDtypeStruct((t, FOX_HEADS * dh), BF16),
        compiler_params=_params(("arbitrary", "arbitrary", "arbitrary")),
        name="fox_attention",
    )(qt, kt, proj, proj, proj, c_row)


def _merge_kernel(a_ref, f_ref, ga_ref, gf_ref, wa_ref, wf_ref, out_ref):
    ya = jnp.dot(a_ref[...], wa_ref[...], preferred_element_type=F32)
    yf = jnp.dot(f_ref[...], wf_ref[...], preferred_element_type=F32)
    out_ref[...] = (jax.nn.sigmoid(ga_ref[...].astype(F32)) * ya
                    + jax.nn.sigmoid(gf_ref[...].astype(F32)) * yf).astype(out_ref.dtype)


def _merge(o_gla, o_fox, proj, w_gla, w_fox, *, offs, tm, tn):
    m, kdim = o_gla.shape
    n = w_gla.shape[1]
    go, fo = offs[0] // tn, offs[1] // tn
    return pl.pallas_call(
        _merge_kernel,
        grid=(m // tm, n // tn),
        in_specs=[
            pl.BlockSpec((tm, kdim), lambda i, j: (i, 0)),
            pl.BlockSpec((tm, o_fox.shape[1]), lambda i, j: (i, 0)),
            pl.BlockSpec((tm, tn), lambda i, j: (i, go + j)),
            pl.BlockSpec((tm, tn), lambda i, j: (i, fo + j)),
            pl.BlockSpec((kdim, tn), lambda i, j: (0, j)),
            pl.BlockSpec((o_fox.shape[1], tn), lambda i, j: (0, j)),
        ],
        out_specs=pl.BlockSpec((tm, tn), lambda i, j: (i, j)),
        out_shape=jax.ShapeDtypeStruct((m, n), BF16),
        compiler_params=_params(("arbitrary", "arbitrary")),
        name="gated_merge",
    )(o_gla, o_fox, proj, proj, w_gla, w_fox)


def _resid_matmul_kernel(a_ref, w_ref, res_ref, out_ref):
    out_ref[...] = res_ref[...] + jnp.dot(a_ref[...], w_ref[...], preferred_element_type=F32)


def _resid_matmul(a, w, res, *, tm, tn):
    m, kdim = a.shape
    n = w.shape[1]
    return pl.pallas_call(
        _resid_matmul_kernel,
        grid=(m // tm, n // tn),
        in_specs=[pl.BlockSpec((tm, kdim), lambda i, j: (i, 0)),
                  pl.BlockSpec((kdim, tn), lambda i, j: (0, j)),
                  pl.BlockSpec((tm, tn), lambda i, j: (i, j))],
        out_specs=pl.BlockSpec((tm, tn), lambda i, j: (i, j)),
        out_shape=jax.ShapeDtypeStruct((m, n), F32),
        compiler_params=_params(("arbitrary", "arbitrary")),
        name="resid_matmul",
    )(a, w, res)


def _memory_kernel(x_ref, gn_ref, wq_ref, gq_ref, k_ref, v_ref, wo_ref, gm_ref, wr_ref, br_ref,
                   x2_ref, rl_ref, xsp_ref, *, dh):
    x1 = x_ref[...]
    hb = _rms(x1, gn_ref[...]).astype(BF16)
    q = jnp.dot(hb, wq_ref[...], preferred_element_type=F32)
    outs = []
    for h in range(MEM_HEADS):
        sl = slice(h * dh, (h + 1) * dh)
        qn = _rms(q[:, sl], gq_ref[...]).astype(BF16)
        s = lax.dot_general(qn, k_ref[:, sl], NT_DIMS, preferred_element_type=F32)
        s = s - jnp.max(s, axis=-1, keepdims=True)
        p = jnp.exp(s)
        p = p / jnp.sum(p, axis=-1, keepdims=True)
        outs.append(jnp.dot(p.astype(BF16), v_ref[:, sl], preferred_element_type=F32).astype(BF16))
    o = jnp.concatenate(outs, axis=-1)
    x2 = x1 + jnp.dot(o, wo_ref[...], preferred_element_type=F32)
    x2_ref[...] = x2
    xs = _rms(x2, gm_ref[...])
    rl_ref[...] = jnp.dot(xs.astype(BF16), wr_ref[...], preferred_element_type=F32) + br_ref[...]
    xsp_ref[...] = _pack_bf16_pairs(xs)


def _memory(x1, mem_norm_g, wq, gq, kv, wo, moe_norm_g, wr_pad, br_pad, *, batch, seq,
            mem_len, tm):
    t, d = x1.shape
    w = wq.shape[1]
    dh = w // MEM_HEADS
    nblk = seq // tm
    const = lambda i: (0, 0)
    return pl.pallas_call(
        functools.partial(_memory_kernel, dh=dh),
        grid=(t // tm,),
        in_specs=[
            pl.BlockSpec((tm, d), lambda i: (i, 0)),
            pl.BlockSpec((1, d), const),
            pl.BlockSpec((d, w), const),
            pl.BlockSpec((1, dh), const),
            pl.BlockSpec((mem_len, w), lambda i: (i // nblk, 0)),
            pl.BlockSpec((mem_len, w), lambda i: (i // nblk, 1)),
            pl.BlockSpec((w, d), const),
            pl.BlockSpec((1, d), const),
            pl.BlockSpec((d, LANES), const),
            pl.BlockSpec((1, LANES), const),
        ],
        out_specs=[pl.BlockSpec((tm, d), lambda i: (i, 0)),
                   pl.BlockSpec((tm, LANES), lambda i: (i, 0)),
                   pl.BlockSpec((tm, d // 2), lambda i: (i, 0))],
        out_shape=[jax.ShapeDtypeStruct((t, d), F32), jax.ShapeDtypeStruct((t, LANES), F32),
                   jax.ShapeDtypeStruct((t, d // 2), jnp.uint32)],
        compiler_params=_params(("arbitrary",)),
        name="memory_block",
    )(x1, mem_norm_g, wq, gq, kv, kv, wo, moe_norm_g, wr_pad, br_pad)


def _route_kernel(rl_ref, route_ref, counts_ref, carry_ref, *, tr, n_experts):
    @pl.when(pl.program_id(0) == 0)
    def _():
        carry_ref[...] = jnp.zeros_like(carry_ref)

    lane = lax.broadcasted_iota(jnp.int32, (tr, LANES), 1)
    lg = jnp.where(lane < n_experts, rl_ref[...], -jnp.inf)
    vals, hots = [], []
    for _ in range(TOP_K):
        mx = jnp.max(lg, axis=-1, keepdims=True)
        idx = jnp.min(jnp.where(lg == mx, lane, LANES), axis=-1, keepdims=True)
        hot = lane == idx
        vals.append(mx)
        hots.append(hot)
        lg = jnp.where(hot, -jnp.inf, lg)
    exps = [jnp.exp(v - vals[0]) for v in vals]
    denom = exps[0]
    for e in exps[1:]:
        denom = denom + e
    onehot = hots[0]
    for hsel in hots[1:]:
        onehot = jnp.logical_or(onehot, hsel)
    onehot_f = onehot.astype(F32)
    row = lax.broadcasted_iota(jnp.int32, (tr, tr), 0)
    col = lax.broadcasted_iota(jnp.int32, (tr, tr), 1)
    strict = (row > col).astype(BF16)
    before = jnp.dot(strict, onehot_f.astype(BF16), preferred_element_type=F32) + carry_ref[...]
    route = jnp.zeros((tr, LANES), F32)
    for k in range(TOP_K):
        idx_f = jnp.sum(jnp.where(hots[k], lane, 0), axis=-1, keepdims=True).astype(F32)
        rank = jnp.sum(jnp.where(hots[k], before, 0.0), axis=-1, keepdims=True)
        route = jnp.where(lane == k, idx_f, route)
        route = jnp.where(lane == TOP_K + k, exps[k] / denom, route)
        route = jnp.where(lane == 2 * TOP_K + k, rank, route)
    route_ref[...] = route
    total = carry_ref[...] + jnp.sum(onehot_f, axis=0, keepdims=True)
    carry_ref[...] = total
    counts_ref[...] = jnp.broadcast_to(total, counts_ref.shape)


def _route(rlogits, *, n_experts, tr):
    t = rlogits.shape[0]
    return pl.pallas_call(
        functools.partial(_route_kernel, tr=tr, n_experts=n_experts),
        grid=(t // tr,),
        in_specs=[pl.BlockSpec((tr, LANES), lambda i: (i, 0))],
        out_specs=[pl.BlockSpec((tr, LANES), lambda i: (i, 0)),
                   pl.BlockSpec((8, LANES), lambda i: (0, 0))],
        out_shape=[jax.ShapeDtypeStruct((t, LANES), F32), jax.ShapeDtypeStruct((8, LANES), F32)],
        scratch_shapes=[pltpu.VMEM((1, LANES), F32)],
        compiler_params=_params(("arbitrary",)),
        name="route_topk",
    )(rlogits)


def _gather_kernel(tok_ref, nused_ref, x_hbm, out_ref, buf, sem, *, rows):
    i = pl.program_id(0)
    nused = nused_ref[0]

    def issue(blk, slot):
        def body(r8, carry):
            for j in range(SUBLANES):
                tok = tok_ref[blk * rows + r8 * SUBLANES + j]
                pltpu.make_async_copy(
                    x_hbm.at[tok >> 3, pl.ds(tok & (SUBLANES - 1), 1)],
                    buf.at[slot, r8, pl.ds(j, 1)], sem.at[slot]).start()
            return carry
        lax.fori_loop(0, rows // SUBLANES, body, 0)

    @pl.when(i == 0)
    def _():
        issue(0, 0)

    @pl.when(i + 1 < nused)
    def _():
        issue(i + 1, (i + 1) % 2)

    @pl.when(i < nused)
    def _():
        slot = i % 2
        pltpu.make_async_copy(x_hbm.at[pl.ds(0, rows // SUBLANES)], buf.at[slot],
                              sem.at[slot]).wait()
        lo, hi = _unpack_bf16_pairs(buf[slot].reshape(rows, -1))
        half = lo.shape[1]
        out_ref[:, :half] = lo.astype(out_ref.dtype)
        out_ref[:, half:] = hi.astype(out_ref.dtype)

    @pl.when(i >= nused)
    def _():
        out_ref[...] = jnp.zeros_like(out_ref)


def _gather_rows(slot_tok, nused, xs_packed, *, rows):
    p = slot_tok.shape[0]
    half = xs_packed.shape[1]
    grid_spec = pltpu.PrefetchScalarGridSpec(
        num_scalar_prefetch=2,
        grid=(p // rows,),
        in_specs=[pl.BlockSpec(memory_space=pl.ANY)],
        out_specs=pl.BlockSpec((rows, 2 * half), lambda i, tok, nu: (i, 0)),
        scratch_shapes=[pltpu.VMEM((2, rows // SUBLANES, SUBLANES, half), jnp.uint32),
                        pltpu.SemaphoreType.DMA((2,))],
    )
    return pl.pallas_call(
        functools.partial(_gather_kernel, rows=rows),
        grid_spec=grid_spec,
        out_shape=jax.ShapeDtypeStruct((p, 2 * half), BF16),
        compiler_params=_params(("arbitrary",)),
        name="moe_gather",
    )(slot_tok, nused, xs_packed.reshape(-1, SUBLANES, half))


def _expert_up_kernel(blk_ref, tile_ref, exp_ref, flag_ref, x_ref, wg_ref, wl_ref, bg_ref, bl_ref,
                      out_ref, wg_bf, wl_bf):
    flag = flag_ref[pl.program_id(0)]

    @pl.when(flag >= 2)
    def _():
        wg_bf[...] = wg_ref[...].astype(BF16)
        wl_bf[...] = wl_ref[...].astype(BF16)

    @pl.when(flag % 2 == 1)
    def _():
        x = x_ref[...]
        glu = jnp.dot(x, wg_bf[...], preferred_element_type=F32) + bg_ref[...]
        lin = jnp.dot(x, wl_bf[...], preferred_element_type=F32) + bl_ref[...]
        glu = jnp.minimum(glu, SWIGLU_LIMIT)
        lin = jnp.clip(lin, -SWIGLU_LIMIT, SWIGLU_LIMIT)
        out_ref[...] = (glu * jax.nn.sigmoid(SWIGLU_ALPHA * glu) * (lin + 1.0)).astype(out_ref.dtype)

    @pl.when(flag % 2 == 0)
    def _():
        out_ref[...] = jnp.zeros_like(out_ref)


def _expert_up(tables, xg, w_up, b_up, *, tm, tf):
    p, d = xg.shape
    n_e, _, ff2 = w_up.shape
    ff = ff2 // 2
    nt = ff // tf
    nsteps = tables[0].shape[0]
    grid_spec = pltpu.PrefetchScalarGridSpec(
        num_scalar_prefetch=4,
        grid=(nsteps,),
        in_specs=[
            pl.BlockSpec((tm, d), lambda s, blk, tile, ex, fl: (blk[s], 0)),
            pl.BlockSpec((None, d, tf), lambda s, blk, tile, ex, fl: (ex[s], 0, tile[s])),
            pl.BlockSpec((None, d, tf), lambda s, blk, tile, ex, fl: (ex[s], 0, nt + tile[s])),
            pl.BlockSpec((None, 1, tf), lambda s, blk, tile, ex, fl: (ex[s], 0, tile[s])),
            pl.BlockSpec((None, 1, tf), lambda s, blk, tile, ex, fl: (ex[s], 0, nt + tile[s])),
        ],
        out_specs=pl.BlockSpec((tm, tf), lambda s, blk, tile, ex, fl: (blk[s], tile[s])),
        scratch_shapes=[pltpu.VMEM((d, tf), BF16), pltpu.VMEM((d, tf), BF16)],
    )
    return pl.pallas_call(
        _expert_up_kernel,
        grid_spec=grid_spec,
        out_shape=jax.ShapeDtypeStruct((p, ff), BF16),
        compiler_params=_params(("arbitrary",)),
        name="expert_up",
    )(*tables, xg, w_up, w_up, b_up, b_up)


def _expert_down_kernel(blk_ref, tile_ref, exp_ref, flag_ref, a_ref, w_ref, b_ref, out_ref, w_bf):
    flag = flag_ref[pl.program_id(0)]

    @pl.when(flag >= 2)
    def _():
        w_bf[...] = w_ref[...].astype(BF16)

    @pl.when(flag % 2 == 1)
    def _():
        y = jnp.dot(a_ref[...], w_bf[...], preferred_element_type=F32) + b_ref[...]
        out_ref[...] = _pack_bf16_pairs(y)

    @pl.when(flag % 2 == 0)
    def _():
        out_ref[...] = jnp.zeros_like(out_ref)


def _expert_down(tables, act, w_down, b_down, *, tm, tn):
    p, ff = act.shape
    d = w_down.shape[2]
    nsteps = tables[0].shape[0]
    grid_spec = pltpu.PrefetchScalarGridSpec(
        num_scalar_prefetch=4,
        grid=(nsteps,),
        in_specs=[
            pl.BlockSpec((tm, ff), lambda s, blk, tile, ex, fl: (blk[s], 0)),
            pl.BlockSpec((None, ff, tn), lambda s, blk, tile, ex, fl: (ex[s], 0, tile[s])),
            pl.BlockSpec((None, 1, tn), lambda s, blk, tile, ex, fl: (ex[s], 0, tile[s])),
        ],
        out_specs=pl.BlockSpec((tm, tn // 2), lambda s, blk, tile, ex, fl: (blk[s], tile[s])),
        scratch_shapes=[pltpu.VMEM((ff, tn), BF16)],
    )
    return pl.pallas_call(
        _expert_down_kernel,
        grid_spec=grid_spec,
        out_shape=jax.ShapeDtypeStruct((p, d // 2), jnp.uint32),
        compiler_params=_params(("arbitrary",)),
        name="expert_down",
    )(*tables, act, w_down, b_down)


def _expert_schedule(counts, *, tm, n_tiles, n_blocks):
    nb = (counts + tm - 1) // tm
    bend = jnp.cumsum(nb)
    bstart = bend - nb
    steps_e = nb * n_tiles
    cs = jnp.cumsum(steps_e)
    total = cs[-1]
    s = jnp.arange(n_blocks * n_tiles, dtype=jnp.int32)
    sc = jnp.minimum(s, total - 1)
    e = jnp.sum((cs[None, :] <= sc[:, None]).astype(jnp.int32), axis=1)
    e = jnp.minimum(e, counts.shape[0] - 1)
    r = sc - (cs[e] - steps_e[e])
    nbe = jnp.maximum(nb[e], 1)
    valid = s < total
    n_unused = jnp.maximum(n_blocks - bend[-1], 1)
    u = jnp.maximum(s - total, 0)
    tile = jnp.where(valid, r // nbe, u // n_unused)
    blk = jnp.where(valid, bstart[e] + r % nbe, bend[-1] + u % n_unused)
    first = jnp.logical_and(valid, r % nbe == 0)
    flag = valid.astype(jnp.int32) + 2 * first.astype(jnp.int32)
    return (blk.astype(jnp.int32), tile.astype(jnp.int32), e, flag)


def _combine_kernel(slot_ref, x_ref, route_ref, y_hbm, out_ref, buf, sem, *, tc, tn):
    i = pl.program_id(0)
    n = pl.num_programs(0)

    def issue(blk, bslot):
        def body(r, carry):
            for k in range(TOP_K):
                src = slot_ref[(blk * tc + r) * TOP_K + k]
                pltpu.make_async_copy(y_hbm.at[pl.ds(src, 1)], buf.at[bslot, k, pl.ds(r, 1)],
                                      sem.at[bslot]).start()
            return carry
        lax.fori_loop(0, tc, body, 0, unroll=4)

    @pl.when(i == 0)
    def _():
        issue(0, 0)

    @pl.when(i + 1 < n)
    def _():
        issue(i + 1, (i + 1) % 2)

    bslot = i % 2
    for k in range(TOP_K):
        pltpu.make_async_copy(y_hbm.at[pl.ds(0, tc)], buf.at[bslot, k], sem.at[bslot]).wait()
    gates = [route_ref[:, TOP_K + k:TOP_K + k + 1] for k in range(TOP_K)]
    hw = tn // 2
    for j in range(out_ref.shape[1] // tn):
        parts = [_unpack_bf16_pairs(buf[bslot, k, :, j * hw:(j + 1) * hw]) for k in range(TOP_K)]
        for half in range(2):
            cols = slice(j * tn + half * hw, j * tn + (half + 1) * hw)
            acc = x_ref[:, cols]
            for k in range(TOP_K):
                acc = acc + gates[k] * parts[k][half]
            out_ref[:, cols] = acc


def _combine(slot_flat, x2, route, y, *, tc, tn):
    t, d = x2.shape
    grid_spec = pltpu.PrefetchScalarGridSpec(
        num_scalar_prefetch=1,
        grid=(t // tc,),
        in_specs=[pl.BlockSpec((tc, d), lambda i, sl: (i, 0)),
                  pl.BlockSpec((tc, LANES), lambda i, sl: (i, 0)),
                  pl.BlockSpec(memory_space=pl.ANY)],
        out_specs=pl.BlockSpec((tc, d), lambda i, sl: (i, 0)),
        scratch_shapes=[pltpu.VMEM((2, TOP_K, tc, d // 2), jnp.uint32),
                        pltpu.SemaphoreType.DMA((2,))],
    )
    return pl.pallas_call(
        functools.partial(_combine_kernel, tc=tc, tn=tn),
        grid_spec=grid_spec,
        out_shape=jax.ShapeDtypeStruct((t, d), F32),
        compiler_params=_params(("arbitrary",)),
        name="moe_combine",
    )(slot_flat, x2, route, y)


def _pad_lanes(a, offset=0):
    width = a.shape[-1]
    return jnp.pad(a, [(0, 0)] * (a.ndim - 1) + [(offset, LANES - offset - width)])


def _layer(x2d, mem2d, lw, *, batch, seq, mem_len):
    t, d = x2d.shape
    gla_qk = lw["gla_alpha_up"].shape[1]
    gla_rank = lw["gla_alpha_up"].shape[0]
    gla_v = lw["w_branch_gla"].shape[0]
    fox_w = lw["w_branch_fox"].shape[0]
    dk, dv = gla_qk // GLA_HEADS, gla_v // GLA_HEADS
    dh = fox_w // FOX_HEADS
    mem_w = lw["mem_w_q"].shape[1]
    mem_dh = mem_w // MEM_HEADS
    n_experts = lw["w_router"].shape[1]
    ff = lw["w_expert_down"].shape[1]

    sizes = (gla_qk, gla_qk, gla_v, gla_v, gla_rank, fox_w, fox_w, fox_w, FOX_HEADS, d, d)
    starts = [0]
    for sz in sizes:
        starts.append(starts[-1] + sz)
    w_in = lw["w_in"]
    seg = lambda i: w_in[:, starts[i]:starts[i + 1]]
    big_ids = (0, 1, 2, 3, 5, 6, 7, 9, 10)
    w_big = jnp.concatenate([seg(i).astype(BF16) for i in big_ids], axis=1)
    w_small = _pad_lanes(jnp.concatenate([seg(4), seg(8)], axis=1)).astype(BF16)
    off = {}
    acc = 0
    for i in big_ids:
        off[i] = acc
        acc += sizes[i]
    n_big = acc
    tn1 = 1024
    colscale = jnp.ones((1, n_big), F32)
    colscale = colscale.at[:, off[5]:off[5] + fox_w].set(
        jnp.tile(lw["fox_q_norm_g"] * (dh ** -0.5 * LOG2E), FOX_HEADS)[None])
    colscale = colscale.at[:, off[6]:off[6] + fox_w].set(jnp.tile(lw["fox_k_norm_g"], FOX_HEADS)[None])
    proj, small = _norm_matmul(
        x2d, lw["mix_norm_g"][None], w_big, colscale, w_small,
        tm=1024, tn=tn1, norm_lo=off[5] // tn1, norm_hi=(off[6] + fox_w) // tn1)

    alpha_up_pad = jnp.pad(lw["gla_alpha_up"], ((0, LANES - gla_rank), (0, 0))).astype(BF16)
    o_gla = _gla(proj, small, alpha_up_pad, lw["gla_alpha_bias"][None], lw["gla_out_norm_g"][None],
                 batch=batch, seq=seq, dk=dk, dv=dv, offs=(off[0], off[1], off[2], off[3]), rows=256, hps=4)

    c = _fcum(small, _pad_lanes(lw["fox_f_bias"][None], gla_rank), batch=batch, seq=seq, tb=512)
    c_row = c[:, gla_rank:gla_rank + FOX_HEADS].reshape(batch, seq, FOX_HEADS)
    c_row = c_row.transpose(0, 2, 1).reshape(batch * FOX_HEADS, 1, seq) * LOG2E
    o_fox = _fox(proj, c_row, batch=batch, seq=seq, dh=dh, offs=(off[5], off[6], off[7]),
                 tq=min(seq, 2048))

    merged = _merge(o_gla, o_fox, proj, lw["w_branch_gla"].astype(BF16),
                    lw["w_branch_fox"].astype(BF16), offs=(off[9], off[10]), tm=512, tn=1024)
    x1 = _resid_matmul(merged, lw["w_out"].astype(BF16), x2d, tm=1024, tn=1024)

    kv_scale = jnp.concatenate([jnp.tile(lw["mem_k_norm_g"], MEM_HEADS),
                                jnp.ones((mem_w,), F32)])[None]
    kv = _norm_matmul(mem2d, lw["mem_kv_norm_g"][None], lw["mem_w_kv"].astype(BF16), kv_scale, None,
                      tm=mem2d.shape[0], tn=mem_w, norm_lo=0, norm_hi=1)
    x2, rlogits, xs_packed = _memory(
        x1, lw["mem_norm_g"][None], lw["mem_w_q"].astype(BF16),
        (lw["mem_q_norm_g"] * (mem_dh ** -0.5))[None], kv,
        lw["mem_w_o"].astype(BF16), lw["moe_norm_g"][None],
        _pad_lanes(lw["w_router"]).astype(BF16), _pad_lanes(lw["b_router"][None]),
        batch=batch, seq=seq, mem_len=mem_len, tm=512)

    route, counts_f = _route(rlogits, n_experts=n_experts, tr=512)
    tm_e = 512
    a_total = t * TOP_K
    n_blocks = a_total // tm_e + n_experts
    p_rows = n_blocks * tm_e
    counts = counts_f[0, :n_experts].astype(jnp.int32)
    padded = (counts + tm_e - 1) // tm_e * tm_e
    pstart = jnp.cumsum(padded) - padded
    idx = route[:, :TOP_K].astype(jnp.int32)
    rank = route[:, 2 * TOP_K:3 * TOP_K].astype(jnp.int32)
    hit = idx[:, :, None] == jnp.arange(n_experts, dtype=jnp.int32)
    slot = (jnp.sum(jnp.where(hit, pstart, 0), axis=-1) + rank).reshape(a_total)
    slot_tok = (jnp.arange(p_rows, dtype=jnp.int32) % t).at[slot].set(
        jnp.arange(a_total, dtype=jnp.int32) // TOP_K)

    tf = 1024
    g_rows = 512
    nused = (jnp.sum(padded) // g_rows).astype(jnp.int32).reshape(1)
    xg = _gather_rows(slot_tok, nused, xs_packed, rows=g_rows)
    up_tables = _expert_schedule(counts, tm=tm_e, n_tiles=ff // tf, n_blocks=n_blocks)
    act = _expert_up(up_tables, xg, lw["w_expert_up"], lw["b_expert_up"][:, None, :], tm=tm_e, tf=tf)
    tn_d = 1024
    down_tables = _expert_schedule(counts, tm=tm_e, n_tiles=d // tn_d, n_blocks=n_blocks)
    y = _expert_down(down_tables, act, lw["w_expert_down"], lw["b_expert_down"][:, None, :],
                     tm=tm_e, tn=tn_d)
    return _combine(slot, x2, route, y, tc=256, tn=tn_d)


def kernel(x, mem, mix_norm_g, w_in, gla_alpha_up, gla_alpha_bias, gla_out_norm_g, fox_f_bias, fox_q_norm_g, fox_k_norm_g, w_branch_gla, w_branch_fox, w_out, mem_norm_g, mem_kv_norm_g, mem_w_q, mem_w_kv, mem_q_norm_g, mem_k_norm_g, mem_w_o, moe_norm_g, w_router, b_router, w_expert_up, b_expert_up, w_expert_down, b_expert_down):
    weights = dict(
        mix_norm_g=mix_norm_g, w_in=w_in, gla_alpha_up=gla_alpha_up, gla_alpha_bias=gla_alpha_bias,
        gla_out_norm_g=gla_out_norm_g, fox_f_bias=fox_f_bias, fox_q_norm_g=fox_q_norm_g,
        fox_k_norm_g=fox_k_norm_g, w_branch_gla=w_branch_gla, w_branch_fox=w_branch_fox, w_out=w_out,
        mem_norm_g=mem_norm_g, mem_kv_norm_g=mem_kv_norm_g, mem_w_q=mem_w_q, mem_w_kv=mem_w_kv,
        mem_q_norm_g=mem_q_norm_g, mem_k_norm_g=mem_k_norm_g, mem_w_o=mem_w_o, moe_norm_g=moe_norm_g,
        w_router=w_router, b_router=b_router, w_expert_up=w_expert_up, b_expert_up=b_expert_up,
        w_expert_down=w_expert_down, b_expert_down=b_expert_down)
    batch, seq, d = x.shape
    mem_len = mem.shape[1]
    x2d = x.reshape(batch * seq, d)
    mem2d = mem.reshape(batch * mem_len, d)
    for layer in range(mix_norm_g.shape[0]):
        lw = {name: w[layer] for name, w in weights.items()}
        x2d = _layer(x2d, mem2d, lw, batch=batch, seq=seq, mem_len=mem_len)
    return x2d.reshape(batch, seq, d)
```

```python
import functools

import jax
import jax.numpy as jnp
from jax import lax
from jax.experimental import pallas as pl
from jax.experimental.pallas import tpu as pltpu

F32 = jnp.float32
BF16 = jnp.bfloat16

EPS = 1e-6
LANES = 128
SUBLANES = 8
VMEM_LIMIT = 56 * 1024 * 1024
VMEM_LIMIT_MAX = 61 * 1024 * 1024

GLA_HEADS = 4
GLA_GATE_NORM = 16.0
GLA_CHUNK = 64
GLA_SUB = 16
FOX_HEADS = 16
FOX_STRIP = 16
FOX_GROUP = 128
LOG2E = 1.4426950408889634
MEM_HEADS = 4
TOP_K = 4
SWIGLU_LIMIT = 7.0
SWIGLU_ALPHA = 1.702

NT_DIMS = (((1,), (1,)), ((), ()))
TN_DIMS = (((0,), (0,)), ((), ()))


def _params(sem, vmem=VMEM_LIMIT):
    return pltpu.CompilerParams(dimension_semantics=sem, vmem_limit_bytes=vmem)


def _rms(xf, g):
    return xf * lax.rsqrt(jnp.mean(xf * xf, axis=-1, keepdims=True) + EPS) * g


def _pack_bf16_pairs(a):
    half = a.shape[1] // 2
    bits = lambda v: lax.bitcast_convert_type(v.astype(BF16).astype(F32), jnp.uint32)
    return (bits(a[:, half:]) & jnp.uint32(0xFFFF0000)) | (bits(a[:, :half]) >> 16)


def _unpack_bf16_pairs(u):
    lo = lax.bitcast_convert_type(u << 16, F32)
    hi = lax.bitcast_convert_type(u & jnp.uint32(0xFFFF0000), F32)
    return lo, hi


def _log_sigmoid(z):
    return jnp.minimum(z, 0.0) - jnp.log1p(jnp.exp(-jnp.abs(z)))


def _norm_matmul_kernel(*refs, norm_lo, norm_hi, tn, has_small):
    if has_small:
        x_ref, g_ref, w_ref, cs_ref, ws_ref, out_ref, small_ref, h_ref = refs
    else:
        x_ref, g_ref, w_ref, cs_ref, out_ref, h_ref = refs
    j = pl.program_id(1)

    @pl.when(j == 0)
    def _():
        hb = _rms(x_ref[...], g_ref[...]).astype(BF16)
        h_ref[...] = hb
        if has_small:
            small_ref[...] = jnp.dot(hb, ws_ref[...], preferred_element_type=F32)

    acc = jnp.dot(h_ref[...], w_ref[...], preferred_element_type=F32)
    is_norm = jnp.logical_and(j >= norm_lo, j < norm_hi)

    @pl.when(is_norm)
    def _():
        for c in range(tn // LANES):
            sl = slice(c * LANES, (c + 1) * LANES)
            out_ref[:, sl] = _rms(acc[:, sl], cs_ref[:, sl]).astype(out_ref.dtype)

    @pl.when(jnp.logical_not(is_norm))
    def _():
        out_ref[...] = acc.astype(out_ref.dtype)


def _norm_matmul(x, g, w, colscale, w_small, *, tm, tn, norm_lo, norm_hi):
    m, d = x.shape
    n = w.shape[1]
    has_small = w_small is not None
    in_specs = [
        pl.BlockSpec((tm, d), lambda i, j: (i, 0)),
        pl.BlockSpec((1, d), lambda i, j: (0, 0)),
        pl.BlockSpec((d, tn), lambda i, j: (0, j)),
        pl.BlockSpec((1, tn), lambda i, j: (0, j)),
    ]
    args = [x, g, w, colscale]
    out_shape = [jax.ShapeDtypeStruct((m, n), BF16)]
    out_specs = [pl.BlockSpec((tm, tn), lambda i, j: (i, j))]
    if has_small:
        in_specs.append(pl.BlockSpec((d, LANES), lambda i, j: (0, 0)))
        args.append(w_small)
        out_shape.append(jax.ShapeDtypeStruct((m, LANES), F32))
        out_specs.append(pl.BlockSpec((tm, LANES), lambda i, j: (i, 0)))
    res = pl.pallas_call(
        functools.partial(_norm_matmul_kernel, norm_lo=norm_lo, norm_hi=norm_hi, tn=tn,
                          has_small=has_small),
        grid=(m // tm, n // tn),
        in_specs=in_specs,
        out_specs=out_specs,
        out_shape=out_shape,
        scratch_shapes=[pltpu.VMEM((tm, d), BF16)],
        compiler_params=_params(("arbitrary", "arbitrary")),
        name="norm_matmul",
    )(*args)
    return res if has_small else res[0]


def _prefix_sum_rows(tri_bf, x):
    p1 = x.astype(BF16)
    r1 = x - p1.astype(F32)
    p2 = r1.astype(BF16)
    p3 = (r1 - p2.astype(F32)).astype(BF16)
    dot = lambda p: jnp.dot(tri_bf, p, preferred_element_type=F32)
    return (dot(p1) + dot(p2)) + dot(p3)


def _gla_kernel(q_ref, k_ref, v_ref, r_ref, sm_ref, au_ref, ab_ref, gn_ref, out_ref, st_ref,
                *, chunk, nsub, scale, hps, dk, dv):
    @pl.when(pl.program_id(2) == 0)
    def _():
        st_ref[...] = jnp.zeros_like(st_ref)

    row = lax.broadcasted_iota(jnp.int32, (chunk, chunk), 0)
    col = lax.broadcasted_iota(jnp.int32, (chunk, chunk), 1)
    causal = row >= col
    tri = causal.astype(BF16)
    key_row = lax.broadcasted_iota(jnp.int32, (chunk, 1), 0)
    for s in range(nsub):
        rows = pl.ds(s * chunk, chunk)
        ga = sm_ref[rows, :].astype(BF16)
        for hh in range(hps):
            kc = slice(hh * dk, (hh + 1) * dk)
            vc = slice(hh * dv, (hh + 1) * dv)
            z = jnp.dot(ga, au_ref[:, kc], preferred_element_type=F32)
            la = _log_sigmoid(z + ab_ref[:, kc]) * (1.0 / GLA_GATE_NORM)
            b = _prefix_sum_rows(tri, la)
            b_last = b[chunk - 1:chunk, :]
            q = q_ref[rows, kc].astype(F32)
            k = k_ref[rows, kc].astype(F32)
            v = v_ref[rows, vc]
            qd = (q * (scale * jnp.exp(b))).astype(BF16)
            kl = (k * jnp.exp(b_last - b)).astype(BF16)
            blocks = []
            for gi in range(chunk // GLA_SUB):
                grp = slice(gi * GLA_SUB, (gi + 1) * GLA_SUB)
                if gi == 0:
                    q_fac, k_arg = b[grp], -b
                else:
                    r = b[gi * GLA_SUB - 1:gi * GLA_SUB, :]
                    q_fac, k_arg = b[grp] - r, r - b
                k_arg = jnp.where(key_row < (gi + 1) * GLA_SUB, k_arg, -jnp.inf)
                qg = (q[grp] * (scale * jnp.exp(q_fac))).astype(BF16)
                kg = (k * jnp.exp(k_arg)).astype(BF16)
                blocks.append(lax.dot_general(qg, kg, NT_DIMS, preferred_element_type=F32))
            sc = jnp.where(causal, jnp.concatenate(blocks, axis=0), 0.0).astype(BF16)
            st = st_ref[hh]
            o = jnp.dot(sc, v, preferred_element_type=F32) + lax.dot_general(
                qd, st.astype(BF16), NT_DIMS, preferred_element_type=F32)
            st_ref[hh] = st * jnp.exp(b_last) + lax.dot_general(
                v, kl, TN_DIMS, preferred_element_type=F32)
            r = r_ref[rows, vc].astype(F32)
            out_ref[rows, vc] = (_rms(o, gn_ref[...]) * (r * jax.nn.sigmoid(r))).astype(out_ref.dtype)


def _gla(proj, small, alpha_up_pad, alpha_bias, out_norm_g, *, batch, seq, dk, dv, offs, rows, hps):
    t = batch * seq
    nblk = seq // rows
    h = GLA_HEADS
    wk, wv = hps * dk, hps * dv
    qo, ko, vo, ro = (offs[0] // wk, offs[1] // wk, offs[2] // wv, offs[3] // wv)
    rmap = lambda b, hh, c: b * nblk + c
    return pl.pallas_call(
        functools.partial(_gla_kernel, chunk=GLA_CHUNK, nsub=rows // GLA_CHUNK, scale=dk ** -0.5,
                          hps=hps, dk=dk, dv=dv),
        grid=(batch, h // hps, nblk),
        in_specs=[
            pl.BlockSpec((rows, wk), lambda b, hh, c: (rmap(b, hh, c), qo + hh)),
            pl.BlockSpec((rows, wk), lambda b, hh, c: (rmap(b, hh, c), ko + hh)),
            pl.BlockSpec((rows, wv), lambda b, hh, c: (rmap(b, hh, c), vo + hh)),
            pl.BlockSpec((rows, wv), lambda b, hh, c: (rmap(b, hh, c), ro + hh)),
            pl.BlockSpec((rows, LANES), lambda b, hh, c: (rmap(b, hh, c), 0)),
            pl.BlockSpec((LANES, wk), lambda b, hh, c: (0, hh)),
            pl.BlockSpec((1, wk), lambda b, hh, c: (0, hh)),
            pl.BlockSpec((1, dv), lambda b, hh, c: (0, 0)),
        ],
        out_specs=pl.BlockSpec((rows, wv), lambda b, hh, c: (rmap(b, hh, c), hh)),
        out_shape=jax.ShapeDtypeStruct((t, h * dv), BF16),
        scratch_shapes=[pltpu.VMEM((hps, dv, dk), F32)],
        compiler_params=_params(("arbitrary", "arbitrary", "arbitrary")),
        name="gla",
    )(proj, proj, proj, proj, small, alpha_up_pad, alpha_bias, out_norm_g)


def _fcum_kernel(sm_ref, bias_ref, out_ref, carry_ref, *, tb):
    @pl.when(pl.program_id(1) == 0)
    def _():
        carry_ref[...] = jnp.zeros_like(carry_ref)

    row = lax.broadcasted_iota(jnp.int32, (tb, tb), 0)
    col = lax.broadcasted_iota(jnp.int32, (tb, tb), 1)
    tri = (row >= col).astype(BF16)
    lf = _log_sigmoid(sm_ref[...] + bias_ref[...])
    c = _prefix_sum_rows(tri, lf) + carry_ref[...]
    out_ref[...] = c
    carry_ref[...] = c[tb - 1:tb, :]


def _fcum(small, bias_pad, *, batch, seq, tb):
    nb = seq // tb
    return pl.pallas_call(
        functools.partial(_fcum_kernel, tb=tb),
        grid=(batch, nb),
        in_specs=[pl.BlockSpec((tb, LANES), lambda b, i: (b * nb + i, 0)),
                  pl.BlockSpec((1, LANES), lambda b, i: (0, 0))],
        out_specs=pl.BlockSpec((tb, LANES), lambda b, i: (b * nb + i, 0)),
        out_shape=jax.ShapeDtypeStruct(small.shape, F32),
        scratch_shapes=[pltpu.VMEM((1, LANES), F32)],
        compiler_params=_params(("arbitrary", "arbitrary")),
        name="forget_cumsum",
    )(small, bias_pad)


def _fox_kernel(qt_ref, kt_ref, q_ref, k_ref, v_ref, c_ref, out_ref, m_ref, mn_ref, a_ref, l_ref,
                acc_ref, s_ref, p_ref, *, tq, rb, grp):
    p = pl.program_id(2)
    qi = qt_ref[p]
    ki = kt_ref[p]

    @pl.when(ki == 0)
    def _():
        m_ref[...] = jnp.full_like(m_ref, -jnp.inf)
        l_ref[...] = jnp.zeros_like(l_ref)
        acc_ref[...] = jnp.zeros_like(acc_ref)

    def update(diag):
        c = c_ref[0]

        def ncols(g):
            return (g + 1) * grp if diag else tq

        def qk(g):
            n = ncols(g)
            s_ref[g % 2, :, 0:n] = lax.dot_general(q_ref[g * grp:(g + 1) * grp, :], k_ref[0:n, :],
                                                   NT_DIMS, preferred_element_type=F32)

        def pv(g):
            n = ncols(g)
            rows = slice(g * grp, (g + 1) * grp)
            acc_ref[rows, :] += jnp.dot(p_ref[g % 2, :, 0:n], v_ref[0:n, :],
                                        preferred_element_type=F32)

        def softmax(g):
            buf = g % 2
            strips = []
            for st in range(grp // rb):
                r0 = g * grp + st * rb
                visible = (r0 + rb - 1) // LANES + 1 if diag else tq // LANES
                strips.append((slice(st * rb, (st + 1) * rb), slice(r0, r0 + rb), r0, visible))

            def logits(rows_l, r0, t):
                cols = slice(t * LANES, (t + 1) * LANES)
                s = s_ref[buf, rows_l, cols] - c[:, cols]
                if diag and (t + 1) * LANES - 1 > r0:
                    row = r0 + lax.broadcasted_iota(jnp.int32, (rb, LANES), 0)
                    col = t * LANES + lax.broadcasted_iota(jnp.int32, (rb, LANES), 1)
                    s = jnp.where(row >= col, s, -jnp.inf)
                return s

            for rows_l, rows, r0, visible in strips:
                part = logits(rows_l, r0, 0)
                for t in range(1, visible):
                    part = jnp.maximum(part, logits(rows_l, r0, t))
                m_old = m_ref[rows, :]
                m_new = jnp.maximum(m_old, jnp.max(part, axis=-1, keepdims=True))
                a_ref[rows, :] = jnp.exp2(m_old - m_new)
                mn_ref[rows, :] = m_new
            for rows_l, rows, r0, visible in strips:
                m_new = mn_ref[rows, :]
                m_ref[rows, :] = m_new
                part = jnp.zeros((rb, LANES), F32)
                for t in range(ncols(g) // LANES):
                    cols = slice(t * LANES, (t + 1) * LANES)
                    if t < visible:
                        pr = jnp.exp2(logits(rows_l, r0, t) - m_new)
                        part = part + pr
                        p_ref[buf, rows_l, cols] = pr.astype(BF16)
                    else:
                        p_ref[buf, rows_l, cols] = jnp.zeros((rb, LANES), BF16)
                alpha = a_ref[rows, :]
                l_ref[rows, :] = alpha * l_ref[rows, :] + part
                acc_ref[rows, :] = alpha * acc_ref[rows, :]

        ngrp = tq // grp
        qk(0)
        for g in range(ngrp):
            if g + 1 < ngrp:
                qk(g + 1)
            softmax(g)
            if g >= 1:
                pv(g - 1)
        pv(ngrp - 1)

    @pl.when(ki < qi)
    def _():
        update(False)

    @pl.when(ki == qi)
    def _():
        update(True)
        denom = jnp.sum(l_ref[...], axis=-1, keepdims=True)
        out_ref[...] = (acc_ref[...] / denom).astype(out_ref.dtype)


def _fox(proj, c_row, *, batch, seq, dh, offs, tq):
    t = batch * seq
    nq = seq // tq
    pairs = [(a, b) for a in range(nq) for b in range(a + 1)]
    qt = jnp.asarray([a for a, _ in pairs], jnp.int32)
    kt = jnp.asarray([b for _, b in pairs], jnp.int32)
    qo, ko, vo = (o // dh for o in offs)
    grp = min(tq, FOX_GROUP)
    grid_spec = pltpu.PrefetchScalarGridSpec(
        num_scalar_prefetch=2,
        grid=(batch, FOX_HEADS, len(pairs)),
        in_specs=[
            pl.BlockSpec((tq, dh), lambda b, h, p, qt, kt: (b * nq + qt[p], qo + h)),
            pl.BlockSpec((tq, dh), lambda b, h, p, qt, kt: (b * nq + kt[p], ko + h)),
            pl.BlockSpec((tq, dh), lambda b, h, p, qt, kt: (b * nq + kt[p], vo + h)),
            pl.BlockSpec((1, 1, tq), lambda b, h, p, qt, kt: (b * FOX_HEADS + h, 0, kt[p])),
        ],
        out_specs=pl.BlockSpec((tq, dh), lambda b, h, p, qt, kt: (b * nq + qt[p], h)),
        scratch_shapes=[pltpu.VMEM((tq, LANES), F32)] * 4 + [
                        pltpu.VMEM((tq, dh), F32), pltpu.VMEM((2, grp, tq), F32),
                        pltpu.VMEM((2, grp, tq), BF16)],
    )
    return pl.pallas_call(
        functools.partial(_fox_kernel, tq=tq, rb=FOX_STRIP, grp=grp),
        grid_spec=grid_spec,
        out_shape=jax.ShapeDtypeStruct((t, FOX_HEADS * dh), BF16),
        compiler_params=_params(("arbitrary", "arbitrary", "arbitrary")),
        name="fox_attention",
    )(qt, kt, proj, proj, proj, c_row)


def _merge_kernel(a_ref, f_ref, ga_ref, gf_ref, wa_ref, wf_ref, out_ref):
    ya = jnp.dot(a_ref[...], wa_ref[...], preferred_element_type=F32)
    yf = jnp.dot(f_ref[...], wf_ref[...], preferred_element_type=F32)
    out_ref[...] = (jax.nn.sigmoid(ga_ref[...].astype(F32)) * ya
                    + jax.nn.sigmoid(gf_ref[...].astype(F32)) * yf).astype(out_ref.dtype)


def _merge(o_gla, o_fox, proj, w_gla, w_fox, *, offs, tm, tn):
    m, kdim = o_gla.shape
    n = w_gla.shape[1]
    go, fo = offs[0] // tn, offs[1] // tn
    return pl.pallas_call(
        _merge_kernel,
        grid=(m // tm, n // tn),
        in_specs=[
            pl.BlockSpec((tm, kdim), lambda i, j: (i, 0)),
            pl.BlockSpec((tm, o_fox.shape[1]), lambda i, j: (i, 0)),
            pl.BlockSpec((tm, tn), lambda i, j: (i, go + j)),
            pl.BlockSpec((tm, tn), lambda i, j: (i, fo + j)),
            pl.BlockSpec((kdim, tn), lambda i, j: (0, j)),
            pl.BlockSpec((o_fox.shape[1], tn), lambda i, j: (0, j)),
        ],
        out_specs=pl.BlockSpec((tm, tn), lambda i, j: (i, j)),
        out_shape=jax.ShapeDtypeStruct((m, n), BF16),
        compiler_params=_params(("arbitrary", "arbitrary")),
        name="gated_merge",
    )(o_gla, o_fox, proj, proj, w_gla, w_fox)


def _resid_matmul_kernel(a_ref, w_ref, res_ref, out_ref):
    out_ref[...] = res_ref[...] + jnp.dot(a_ref[...], w_ref[...], preferred_element_type=F32)


def _resid_matmul(a, w, res, *, tm, tn):
    m, kdim = a.shape
    n = w.shape[1]
    return pl.pallas_call(
        _resid_matmul_kernel,
        grid=(m // tm, n // tn),
        in_specs=[pl.BlockSpec((tm, kdim), lambda i, j: (i, 0)),
                  pl.BlockSpec((kdim, tn), lambda i, j: (0, j)),
                  pl.BlockSpec((tm, tn), lambda i, j: (i, j))],
        out_specs=pl.BlockSpec((tm, tn), lambda i, j: (i, j)),
        out_shape=jax.ShapeDtypeStruct((m, n), F32),
        compiler_params=_params(("arbitrary", "arbitrary")),
        name="resid_matmul",
    )(a, w, res)


def _memory_kernel(x_ref, gn_ref, wq_ref, gq_ref, k_ref, v_ref, wo_ref, gm_ref, wr_ref, br_ref,
                   x2_ref, rl_ref, xsp_ref, *, dh):
    x1 = x_ref[...]
    hb = _rms(x1, gn_ref[...]).astype(BF16)
    q = jnp.dot(hb, wq_ref[...], preferred_element_type=F32)
    outs = []
    for h in range(MEM_HEADS):
        sl = slice(h * dh, (h + 1) * dh)
        qn = _rms(q[:, sl], gq_ref[...]).astype(BF16)
        s = lax.dot_general(qn, k_ref[:, sl], NT_DIMS, preferred_element_type=F32)
        s = s - jnp.max(s, axis=-1, keepdims=True)
        p = jnp.exp(s)
        p = p / jnp.sum(p, axis=-1, keepdims=True)
        outs.append(jnp.dot(p.astype(BF16), v_ref[:, sl], preferred_element_type=F32).astype(BF16))
    o = jnp.concatenate(outs, axis=-1)
    x2 = x1 + jnp.dot(o, wo_ref[...], preferred_element_type=F32)
    x2_ref[...] = x2
    xs = _rms(x2, gm_ref[...])
    rl_ref[...] = jnp.dot(xs.astype(BF16), wr_ref[...], preferred_element_type=F32) + br_ref[...]
    xsp_ref[...] = _pack_bf16_pairs(xs)


def _memory(x1, mem_norm_g, wq, gq, kv, wo, moe_norm_g, wr_pad, br_pad, *, batch, seq,
            mem_len, tm):
    t, d = x1.shape
    w = wq.shape[1]
    dh = w // MEM_HEADS
    nblk = seq // tm
    const = lambda i: (0, 0)
    return pl.pallas_call(
        functools.partial(_memory_kernel, dh=dh),
        grid=(t // tm,),
        in_specs=[
            pl.BlockSpec((tm, d), lambda i: (i, 0)),
            pl.BlockSpec((1, d), const),
            pl.BlockSpec((d, w), const),
            pl.BlockSpec((1, dh), const),
            pl.BlockSpec((mem_len, w), lambda i: (i // nblk, 0)),
            pl.BlockSpec((mem_len, w), lambda i: (i // nblk, 1)),
            pl.BlockSpec((w, d), const),
            pl.BlockSpec((1, d), const),
            pl.BlockSpec((d, LANES), const),
            pl.BlockSpec((1, LANES), const),
        ],
        out_specs=[pl.BlockSpec((tm, d), lambda i: (i, 0)),
                   pl.BlockSpec((tm, LANES), lambda i: (i, 0)),
                   pl.BlockSpec((tm, d // 2), lambda i: (i, 0))],
        out_shape=[jax.ShapeDtypeStruct((t, d), F32), jax.ShapeDtypeStruct((t, LANES), F32),
                   jax.ShapeDtypeStruct((t, d // 2), jnp.uint32)],
        compiler_params=_params(("arbitrary",)),
        name="memory_block",
    )(x1, mem_norm_g, wq, gq, kv, kv, wo, moe_norm_g, wr_pad, br_pad)


def _route_kernel(rl_ref, route_ref, counts_ref, carry_ref, *, tr, n_experts):
    @pl.when(pl.program_id(0) == 0)
    def _():
        carry_ref[...] = jnp.zeros_like(carry_ref)

    lane = lax.broadcasted_iota(jnp.int32, (tr, LANES), 1)
    lg = jnp.where(lane < n_experts, rl_ref[...], -jnp.inf)
    vals, hots = [], []
    for _ in range(TOP_K):
        mx = jnp.max(lg, axis=-1, keepdims=True)
        idx = jnp.min(jnp.where(lg == mx, lane, LANES), axis=-1, keepdims=True)
        hot = lane == idx
        vals.append(mx)
        hots.append(hot)
        lg = jnp.where(hot, -jnp.inf, lg)
    exps = [jnp.exp(v - vals[0]) for v in vals]
    denom = exps[0]
    for e in exps[1:]:
        denom = denom + e
    onehot = hots[0]
    for hsel in hots[1:]:
        onehot = jnp.logical_or(onehot, hsel)
    onehot_f = onehot.astype(F32)
    row = lax.broadcasted_iota(jnp.int32, (tr, tr), 0)
    col = lax.broadcasted_iota(jnp.int32, (tr, tr), 1)
    strict = (row > col).astype(BF16)
    before = jnp.dot(strict, onehot_f.astype(BF16), preferred_element_type=F32) + carry_ref[...]
    route = jnp.zeros((tr, LANES), F32)
    for k in range(TOP_K):
        idx_f = jnp.sum(jnp.where(hots[k], lane, 0), axis=-1, keepdims=True).astype(F32)
        rank = jnp.sum(jnp.where(hots[k], before, 0.0), axis=-1, keepdims=True)
        route = jnp.where(lane == k, idx_f, route)
        route = jnp.where(lane == TOP_K + k, exps[k] / denom, route)
        route = jnp.where(lane == 2 * TOP_K + k, rank, route)
    route_ref[...] = route
    total = carry_ref[...] + jnp.sum(onehot_f, axis=0, keepdims=True)
    carry_ref[...] = total
    counts_ref[...] = jnp.broadcast_to(total, counts_ref.shape)


def _route(rlogits, *, n_experts, tr):
    t = rlogits.shape[0]
    return pl.pallas_call(
        functools.partial(_route_kernel, tr=tr, n_experts=n_experts),
        grid=(t // tr,),
        in_specs=[pl.BlockSpec((tr, LANES), lambda i: (i, 0))],
        out_specs=[pl.BlockSpec((tr, LANES), lambda i: (i, 0)),
                   pl.BlockSpec((8, LANES), lambda i: (0, 0))],
        out_shape=[jax.ShapeDtypeStruct((t, LANES), F32), jax.ShapeDtypeStruct((8, LANES), F32)],
        scratch_shapes=[pltpu.VMEM((1, LANES), F32)],
        compiler_params=_params(("arbitrary",)),
        name="route_topk",
    )(rlogits)


def _gather_kernel(tok_ref, nused_ref, x_hbm, out_ref, buf, sem, *, rows):
    i = pl.program_id(0)
    nused = nused_ref[0]

    def issue(blk, slot):
        def body(r8, carry):
            for j in range(SUBLANES):
                tok = tok_ref[blk * rows + r8 * SUBLANES + j]
                pltpu.make_async_copy(
                    x_hbm.at[tok >> 3, pl.ds(tok & (SUBLANES - 1), 1)],
                    buf.at[slot, r8, pl.ds(j, 1)], sem.at[slot]).start()
            return carry
        lax.fori_loop(0, rows // SUBLANES, body, 0)

    @pl.when(i == 0)
    def _():
        issue(0, 0)

    @pl.when(i + 1 < nused)
    def _():
        issue(i + 1, (i + 1) % 2)

    @pl.when(i < nused)
    def _():
        slot = i % 2
        pltpu.make_async_copy(x_hbm.at[pl.ds(0, rows // SUBLANES)], buf.at[slot],
                              sem.at[slot]).wait()
        lo, hi = _unpack_bf16_pairs(buf[slot].reshape(rows, -1))
        half = lo.shape[1]
        out_ref[:, :half] = lo.astype(out_ref.dtype)
        out_ref[:, half:] = hi.astype(out_ref.dtype)

    @pl.when(i >= nused)
    def _():
        out_ref[...] = jnp.zeros_like(out_ref)


def _gather_rows(slot_tok, nused, xs_packed, *, rows):
    p = slot_tok.shape[0]
    half = xs_packed.shape[1]
    grid_spec = pltpu.PrefetchScalarGridSpec(
        num_scalar_prefetch=2,
        grid=(p // rows,),
        in_specs=[pl.BlockSpec(memory_space=pl.ANY)],
        out_specs=pl.BlockSpec((rows, 2 * half), lambda i, tok, nu: (i, 0)),
        scratch_shapes=[pltpu.VMEM((2, rows // SUBLANES, SUBLANES, half), jnp.uint32),
                        pltpu.SemaphoreType.DMA((2,))],
    )
    return pl.pallas_call(
        functools.partial(_gather_kernel, rows=rows),
        grid_spec=grid_spec,
        out_shape=jax.ShapeDtypeStruct((p, 2 * half), BF16),
        compiler_params=_params(("arbitrary",)),
        name="moe_gather",
    )(slot_tok, nused, xs_packed.reshape(-1, SUBLANES, half))


def _start_all(copies):
    for c in copies:
        c.start()


def _wait_all(copies):
    for c in copies:
        c.wait()


def _weight_pipeline(step, flag, tabs, copies_for, convert):
    exp_ref, tile_ref, gslot_ref, nexp_ref, ntile_ref = tabs

    @pl.when((flag & 2) != 0)
    def _():
        slot = gslot_ref[step]
        mine = copies_for(exp_ref[step], tile_ref[step], slot)

        @pl.when(step == 0)
        def _():
            _start_all(mine)

        _wait_all(mine)
        convert(slot)

        @pl.when((flag & 4) != 0)
        def _():
            _start_all(copies_for(nexp_ref[step], ntile_ref[step], 1 - slot))


def _expert_up_kernel(blk_ref, tile_ref, exp_ref, flag_ref, gslot_ref, nexp_ref, ntile_ref,
                      x_ref, w_hbm, bg_ref, bl_ref, out_ref, wbuf, wg_bf, wl_bf, sem, *, tf, nt):
    step = pl.program_id(0)
    flag = flag_ref[step]

    def copies_for(e, tile, slot):
        return [pltpu.make_async_copy(
            w_hbm.at[e, :, pl.ds(pl.multiple_of((half * nt + tile) * tf, tf), tf)],
            wbuf.at[slot, half], sem.at[slot, half]) for half in range(2)]

    def convert(slot):
        wg_bf[...] = wbuf[slot, 0].astype(BF16)
        wl_bf[...] = wbuf[slot, 1].astype(BF16)

    _weight_pipeline(step, flag, (exp_ref, tile_ref, gslot_ref, nexp_ref, ntile_ref), copies_for,
                     convert)

    @pl.when((flag & 1) != 0)
    def _():
        x = x_ref[...]
        glu = jnp.dot(x, wg_bf[...], preferred_element_type=F32) + bg_ref[...]
        lin = jnp.dot(x, wl_bf[...], preferred_element_type=F32) + bl_ref[...]
        glu = jnp.minimum(glu, SWIGLU_LIMIT)
        lin = jnp.clip(lin, -SWIGLU_LIMIT, SWIGLU_LIMIT)
        out_ref[...] = (glu * jax.nn.sigmoid(SWIGLU_ALPHA * glu) * (lin + 1.0)).astype(out_ref.dtype)

    @pl.when((flag & 1) == 0)
    def _():
        out_ref[...] = jnp.zeros_like(out_ref)


def _expert_up(tables, xg, w_up, b_up, *, tm, tf):
    p, d = xg.shape
    n_e, _, ff2 = w_up.shape
    ff = ff2 // 2
    nt = ff // tf
    nsteps = tables[0].shape[0]
    bias = lambda half: pl.BlockSpec((None, 1, tf),
                                     lambda s, blk, tile, ex, *_: (ex[s], 0, half * nt + tile[s]))
    grid_spec = pltpu.PrefetchScalarGridSpec(
        num_scalar_prefetch=len(tables),
        grid=(nsteps,),
        in_specs=[
            pl.BlockSpec((tm, d), lambda s, blk, *_: (blk[s], 0)),
            pl.BlockSpec(memory_space=pl.ANY),
            bias(0), bias(1),
        ],
        out_specs=pl.BlockSpec((tm, tf), lambda s, blk, tile, *_: (blk[s], tile[s])),
        scratch_shapes=[pltpu.VMEM((2, 2, d, tf), F32), pltpu.VMEM((d, tf), BF16),
                        pltpu.VMEM((d, tf), BF16), pltpu.SemaphoreType.DMA((2, 2))],
    )
    return pl.pallas_call(
        functools.partial(_expert_up_kernel, tf=tf, nt=nt),
        grid_spec=grid_spec,
        out_shape=jax.ShapeDtypeStruct((p, ff), BF16),
        compiler_params=_params(("arbitrary",), VMEM_LIMIT_MAX),
        name="expert_up",
    )(*tables, xg, w_up, b_up, b_up)


def _expert_down_kernel(blk_ref, tile_ref, exp_ref, flag_ref, gslot_ref, nexp_ref, ntile_ref,
                        a_ref, w_hbm, b_ref, out_ref, wbuf, w_bf, sem, *, tn):
    step = pl.program_id(0)
    flag = flag_ref[step]

    def copies_for(e, tile, slot):
        return [pltpu.make_async_copy(w_hbm.at[e, :, pl.ds(pl.multiple_of(tile * tn, tn), tn)],
                                      wbuf.at[slot], sem.at[slot])]

    def convert(slot):
        w_bf[...] = wbuf[slot].astype(BF16)

    _weight_pipeline(step, flag, (exp_ref, tile_ref, gslot_ref, nexp_ref, ntile_ref), copies_for,
                     convert)

    @pl.when((flag & 1) != 0)
    def _():
        y = jnp.dot(a_ref[...], w_bf[...], preferred_element_type=F32) + b_ref[...]
        out_ref[...] = _pack_bf16_pairs(y)

    @pl.when((flag & 1) == 0)
    def _():
        out_ref[...] = jnp.zeros_like(out_ref)


def _expert_down(tables, act, w_down, b_down, *, tm, tn):
    p, ff = act.shape
    d = w_down.shape[2]
    nsteps = tables[0].shape[0]
    grid_spec = pltpu.PrefetchScalarGridSpec(
        num_scalar_prefetch=len(tables),
        grid=(nsteps,),
        in_specs=[
            pl.BlockSpec((tm, ff), lambda s, blk, *_: (blk[s], 0)),
            pl.BlockSpec(memory_space=pl.ANY),
            pl.BlockSpec((None, 1, tn), lambda s, blk, tile, ex, *_: (ex[s], 0, tile[s])),
        ],
        out_specs=pl.BlockSpec((tm, tn // 2), lambda s, blk, tile, *_: (blk[s], tile[s])),
        scratch_shapes=[pltpu.VMEM((2, ff, tn), F32), pltpu.VMEM((ff, tn), BF16),
                        pltpu.SemaphoreType.DMA((2,))],
    )
    return pl.pallas_call(
        functools.partial(_expert_down_kernel, tn=tn),
        grid_spec=grid_spec,
        out_shape=jax.ShapeDtypeStruct((p, d // 2), jnp.uint32),
        compiler_params=_params(("arbitrary",)),
        name="expert_down",
    )(*tables, act, w_down, b_down)


def _expert_schedule(counts, *, tm, n_tiles, n_blocks):
    n_e = counts.shape[0]
    nb = (counts + tm - 1) // tm
    bend = jnp.cumsum(nb)
    bstart = bend - nb
    steps_e = nb * n_tiles
    cs = jnp.cumsum(steps_e)
    total = cs[-1]
    s = jnp.arange(n_blocks * n_tiles, dtype=jnp.int32)
    sc = jnp.minimum(s, total - 1)
    e = jnp.sum((cs[None, :] <= sc[:, None]).astype(jnp.int32), axis=1)
    e = jnp.minimum(e, n_e - 1)
    r = sc - (cs[e] - steps_e[e])
    nbe = jnp.maximum(nb[e], 1)
    valid = s < total
    n_unused = jnp.maximum(n_blocks - bend[-1], 1)
    u = jnp.maximum(s - total, 0)
    live_tile = r // nbe
    tile = jnp.where(valid, live_tile, u // n_unused)
    blk = jnp.where(valid, bstart[e] + r % nbe, bend[-1] + u % n_unused)
    first = jnp.logical_and(valid, r % nbe == 0)
    ids = jnp.where(nb > 0, jnp.arange(n_e, dtype=jnp.int32), n_e)
    later = jnp.concatenate([lax.cummin(ids, reverse=True)[1:], jnp.full((1,), n_e, jnp.int32)])
    wraps = live_tile + 1 >= n_tiles
    nexp = jnp.where(wraps, later[e], e)
    ntile = jnp.where(wraps, 0, live_tile + 1)
    flag = (valid.astype(jnp.int32) + 2 * first.astype(jnp.int32)
            + 4 * jnp.logical_and(first, nexp < n_e).astype(jnp.int32))
    gslot = (jnp.cumsum(first.astype(jnp.int32)) - 1) % 2
    i32 = lambda a: a.astype(jnp.int32)
    return (i32(blk), i32(tile), i32(e), flag, i32(gslot), i32(jnp.minimum(nexp, n_e - 1)),
            i32(ntile))


def _combine_kernel(slot_ref, x_ref, route_ref, y_hbm, out_ref, buf, sem, *, tc, tn):
    i = pl.program_id(0)
    n = pl.num_programs(0)

    def issue(blk, bslot):
        def body(r, carry):
            for k in range(TOP_K):
                src = slot_ref[(blk * tc + r) * TOP_K + k]
                pltpu.make_async_copy(y_hbm.at[pl.ds(src, 1)], buf.at[bslot, k, pl.ds(r, 1)],
                                      sem.at[bslot]).start()
            return carry
        lax.fori_loop(0, tc, body, 0, unroll=4)

    @pl.when(i == 0)
    def _():
        issue(0, 0)

    @pl.when(i + 1 < n)
    def _():
        issue(i + 1, (i + 1) % 2)

    bslot = i % 2
    for k in range(TOP_K):
        pltpu.make_async_copy(y_hbm.at[pl.ds(0, tc)], buf.at[bslot, k], sem.at[bslot]).wait()
    gates = [route_ref[:, TOP_K + k:TOP_K + k + 1] for k in range(TOP_K)]
    hw = tn // 2
    for j in range(out_ref.shape[1] // tn):
        parts = [_unpack_bf16_pairs(buf[bslot, k, :, j * hw:(j + 1) * hw]) for k in range(TOP_K)]
        for half in range(2):
            cols = slice(j * tn + half * hw, j * tn + (half + 1) * hw)
            acc = x_ref[:, cols]
            for k in range(TOP_K):
                acc = acc + gates[k] * parts[k][half]
            out_ref[:, cols] = acc


def _combine(slot_flat, x2, route, y, *, tc, tn):
    t, d = x2.shape
    grid_spec = pltpu.PrefetchScalarGridSpec(
        num_scalar_prefetch=1,
        grid=(t // tc,),
        in_specs=[pl.BlockSpec((tc, d), lambda i, sl: (i, 0)),
                  pl.BlockSpec((tc, LANES), lambda i, sl: (i, 0)),
                  pl.BlockSpec(memory_space=pl.ANY)],
        out_specs=pl.BlockSpec((tc, d), lambda i, sl: (i, 0)),
        scratch_shapes=[pltpu.VMEM((2, TOP_K, tc, d // 2), jnp.uint32),
                        pltpu.SemaphoreType.DMA((2,))],
    )
    return pl.pallas_call(
        functools.partial(_combine_kernel, tc=tc, tn=tn),
        grid_spec=grid_spec,
        out_shape=jax.ShapeDtypeStruct((t, d), F32),
        compiler_params=_params(("arbitrary",)),
        name="moe_combine",
    )(slot_flat, x2, route, y)


def _pad_lanes(a, offset=0):
    width = a.shape[-1]
    return jnp.pad(a, [(0, 0)] * (a.ndim - 1) + [(offset, LANES - offset - width)])


def _layer(x2d, mem2d, lw, *, batch, seq, mem_len):
    t, d = x2d.shape
    gla_qk = lw["gla_alpha_up"].shape[1]
    gla_rank = lw["gla_alpha_up"].shape[0]
    gla_v = lw["w_branch_gla"].shape[0]
    fox_w = lw["w_branch_fox"].shape[0]
    dk, dv = gla_qk // GLA_HEADS, gla_v // GLA_HEADS
    dh = fox_w // FOX_HEADS
    mem_w = lw["mem_w_q"].shape[1]
    mem_dh = mem_w // MEM_HEADS
    n_experts = lw["w_router"].shape[1]
    ff = lw["w_expert_down"].shape[1]

    sizes = (gla_qk, gla_qk, gla_v, gla_v, gla_rank, fox_w, fox_w, fox_w, FOX_HEADS, d, d)
    starts = [0]
    for sz in sizes:
        starts.append(starts[-1] + sz)
    w_in = lw["w_in"]
    seg = lambda i: w_in[:, starts[i]:starts[i + 1]]
    big_ids = (0, 1, 2, 3, 5, 6, 7, 9, 10)
    w_big = jnp.concatenate([seg(i).astype(BF16) for i in big_ids], axis=1)
    w_small = _pad_lanes(jnp.concatenate([seg(4), seg(8)], axis=1)).astype(BF16)
    off = {}
    acc = 0
    for i in big_ids:
        off[i] = acc
        acc += sizes[i]
    n_big = acc
    tn1 = 1024
    colscale = jnp.ones((1, n_big), F32)
    colscale = colscale.at[:, off[5]:off[5] + fox_w].set(
        jnp.tile(lw["fox_q_norm_g"] * (dh ** -0.5 * LOG2E), FOX_HEADS)[None])
    colscale = colscale.at[:, off[6]:off[6] + fox_w].set(jnp.tile(lw["fox_k_norm_g"], FOX_HEADS)[None])
    proj, small = _norm_matmul(
        x2d, lw["mix_norm_g"][None], w_big, colscale, w_small,
        tm=1024, tn=tn1, norm_lo=off[5] // tn1, norm_hi=(off[6] + fox_w) // tn1)

    alpha_up_pad = jnp.pad(lw["gla_alpha_up"], ((0, LANES - gla_rank), (0, 0))).astype(BF16)
    o_gla = _gla(proj, small, alpha_up_pad, lw["gla_alpha_bias"][None], lw["gla_out_norm_g"][None],
                 batch=batch, seq=seq, dk=dk, dv=dv, offs=(off[0], off[1], off[2], off[3]), rows=256, hps=4)

    c = _fcum(small, _pad_lanes(lw["fox_f_bias"][None], gla_rank), batch=batch, seq=seq, tb=512)
    c_row = c[:, gla_rank:gla_rank + FOX_HEADS].reshape(batch, seq, FOX_HEADS)
    c_row = c_row.transpose(0, 2, 1).reshape(batch * FOX_HEADS, 1, seq) * LOG2E
    o_fox = _fox(proj, c_row, batch=batch, seq=seq, dh=dh, offs=(off[5], off[6], off[7]),
                 tq=min(seq, 2048))

    merged = _merge(o_gla, o_fox, proj, lw["w_branch_gla"].astype(BF16),
                    lw["w_branch_fox"].astype(BF16), offs=(off[9], off[10]), tm=512, tn=1024)
    x1 = _resid_matmul(merged, lw["w_out"].astype(BF16), x2d, tm=1024, tn=1024)

    kv_scale = jnp.concatenate([jnp.tile(lw["mem_k_norm_g"], MEM_HEADS),
                                jnp.ones((mem_w,), F32)])[None]
    kv = _norm_matmul(mem2d, lw["mem_kv_norm_g"][None], lw["mem_w_kv"].astype(BF16), kv_scale, None,
                      tm=mem2d.shape[0], tn=mem_w, norm_lo=0, norm_hi=1)
    x2, rlogits, xs_packed = _memory(
        x1, lw["mem_norm_g"][None], lw["mem_w_q"].astype(BF16),
        (lw["mem_q_norm_g"] * (mem_dh ** -0.5))[None], kv,
        lw["mem_w_o"].astype(BF16), lw["moe_norm_g"][None],
        _pad_lanes(lw["w_router"]).astype(BF16), _pad_lanes(lw["b_router"][None]),
        batch=batch, seq=seq, mem_len=mem_len, tm=512)

    route, counts_f = _route(rlogits, n_experts=n_experts, tr=512)
    tm_e = 512
    a_total = t * TOP_K
    n_blocks = a_total // tm_e + n_experts
    p_rows = n_blocks * tm_e
    counts = counts_f[0, :n_experts].astype(jnp.int32)
    padded = (counts + tm_e - 1) // tm_e * tm_e
    pstart = jnp.cumsum(padded) - padded
    idx = route[:, :TOP_K].astype(jnp.int32)
    rank = route[:, 2 * TOP_K:3 * TOP_K].astype(jnp.int32)
    hit = idx[:, :, None] == jnp.arange(n_experts, dtype=jnp.int32)
    slot = (jnp.sum(jnp.where(hit, pstart, 0), axis=-1) + rank).reshape(a_total)
    slot_tok = (jnp.arange(p_rows, dtype=jnp.int32) % t).at[slot].set(
        jnp.arange(a_total, dtype=jnp.int32) // TOP_K, unique_indices=True,
        mode="promise_in_bounds")

    tf = 1024
    g_rows = 512
    nused = (jnp.sum(padded) // g_rows).astype(jnp.int32).reshape(1)
    xg = _gather_rows(slot_tok, nused, xs_packed, rows=g_rows)
    up_tables = _expert_schedule(counts, tm=tm_e, n_tiles=ff // tf, n_blocks=n_blocks)
    act = _expert_up(up_tables, xg, lw["w_expert_up"], lw["b_expert_up"][:, None, :], tm=tm_e, tf=tf)
    tn_d = 1024
    down_tables = _expert_schedule(counts, tm=tm_e, n_tiles=d // tn_d, n_blocks=n_blocks)
    y = _expert_down(down_tables, act, lw["w_expert_down"], lw["b_expert_down"][:, None, :],
                     tm=tm_e, tn=tn_d)
    return _combine(slot, x2, route, y, tc=256, tn=tn_d)


def kernel(x, mem, mix_norm_g, w_in, gla_alpha_up, gla_alpha_bias, gla_out_norm_g, fox_f_bias, fox_q_norm_g, fox_k_norm_g, w_branch_gla, w_branch_fox, w_out, mem_norm_g, mem_kv_norm_g, mem_w_q, mem_w_kv, mem_q_norm_g, mem_k_norm_g, mem_w_o, moe_norm_g, w_router, b_router, w_expert_up, b_expert_up, w_expert_down, b_expert_down):
    weights = dict(
        mix_norm_g=mix_norm_g, w_in=w_in, gla_alpha_up=gla_alpha_up, gla_alpha_bias=gla_alpha_bias,
        gla_out_norm_g=gla_out_norm_g, fox_f_bias=fox_f_bias, fox_q_norm_g=fox_q_norm_g,
        fox_k_norm_g=fox_k_norm_g, w_branch_gla=w_branch_gla, w_branch_fox=w_branch_fox, w_out=w_out,
        mem_norm_g=mem_norm_g, mem_kv_norm_g=mem_kv_norm_g, mem_w_q=mem_w_q, mem_w_kv=mem_w_kv,
        mem_q_norm_g=mem_q_norm_g, mem_k_norm_g=mem_k_norm_g, mem_w_o=mem_w_o, moe_norm_g=moe_norm_g,
        w_router=w_router, b_router=b_router, w_expert_up=w_expert_up, b_expert_up=b_expert_up,
        w_expert_down=w_expert_down, b_expert_down=b_expert_down)
    batch, seq, d = x.shape
    mem_len = mem.shape[1]
    x2d = x.reshape(batch * seq, d)
    mem2d = mem.reshape(batch * mem_len, d)
    for layer in range(mix_norm_g.shape[0]):
        lw = {name: w[layer] for name, w in weights.items()}
        x2d = _layer(x2d, mem2d, lw, batch=batch, seq=seq, mem_len=mem_len)
    return x2d.reshape(batch, seq, d)
```

```python
import functools

import jax
import jax.numpy as jnp
from jax import lax
from jax.experimental import pallas as pl
from jax.experimental.pallas import tpu as pltpu

F32 = jnp.float32
BF16 = jnp.bfloat16

EPS = 1e-6
LANES = 128
SUBLANES = 8
VMEM_LIMIT = 56 * 1024 * 1024
VMEM_LIMIT_MAX = 61 * 1024 * 1024

GLA_HEADS = 4
GLA_GATE_NORM = 16.0
GLA_CHUNK = 64
GLA_SUB = 16
FOX_HEADS = 16
FOX_STRIP = 16
FOX_GROUP = 128
LOG2E = 1.4426950408889634
MEM_HEADS = 4
TOP_K = 4
SWIGLU_LIMIT = 7.0
SWIGLU_ALPHA = 1.702

NT_DIMS = (((1,), (1,)), ((), ()))
TN_DIMS = (((0,), (0,)), ((), ()))


def _params(sem, vmem=VMEM_LIMIT):
    return pltpu.CompilerParams(dimension_semantics=sem, vmem_limit_bytes=vmem)


def _rms(xf, g):
    return xf * lax.rsqrt(jnp.mean(xf * xf, axis=-1, keepdims=True) + EPS) * g


def _pack_bf16_pairs(a):
    half = a.shape[1] // 2
    bits = lambda v: lax.bitcast_convert_type(v.astype(BF16).astype(F32), jnp.uint32)
    return (bits(a[:, half:]) & jnp.uint32(0xFFFF0000)) | (bits(a[:, :half]) >> 16)


def _unpack_bf16_pairs(u):
    lo = lax.bitcast_convert_type(u << 16, F32)
    hi = lax.bitcast_convert_type(u & jnp.uint32(0xFFFF0000), F32)
    return lo, hi


def _log_sigmoid(z):
    return jnp.minimum(z, 0.0) - jnp.log1p(jnp.exp(-jnp.abs(z)))


def _norm_matmul_kernel(*refs, norm_lo, norm_hi, tn, has_small):
    if has_small:
        x_ref, g_ref, w_ref, cs_ref, ws_ref, out_ref, small_ref, h_ref = refs
    else:
        x_ref, g_ref, w_ref, cs_ref, out_ref, h_ref = refs
    j = pl.program_id(1)

    @pl.when(j == 0)
    def _():
        hb = _rms(x_ref[...], g_ref[...]).astype(BF16)
        h_ref[...] = hb
        if has_small:
            small_ref[...] = jnp.dot(hb, ws_ref[...], preferred_element_type=F32)

    acc = jnp.dot(h_ref[...], w_ref[...], preferred_element_type=F32)
    is_norm = jnp.logical_and(j >= norm_lo, j < norm_hi)

    @pl.when(is_norm)
    def _():
        for c in range(tn // LANES):
            sl = slice(c * LANES, (c + 1) * LANES)
            out_ref[:, sl] = _rms(acc[:, sl], cs_ref[:, sl]).astype(out_ref.dtype)

    @pl.when(jnp.logical_not(is_norm))
    def _():
        out_ref[...] = acc.astype(out_ref.dtype)


def _norm_matmul(x, g, w, colscale, w_small, *, tm, tn, norm_lo, norm_hi):
    m, d = x.shape
    n = w.shape[1]
    has_small = w_small is not None
    in_specs = [
        pl.BlockSpec((tm, d), lambda i, j: (i, 0)),
        pl.BlockSpec((1, d), lambda i, j: (0, 0)),
        pl.BlockSpec((d, tn), lambda i, j: (0, j)),
        pl.BlockSpec((1, tn), lambda i, j: (0, j)),
    ]
    args = [x, g, w, colscale]
    out_shape = [jax.ShapeDtypeStruct((m, n), BF16)]
    out_specs = [pl.BlockSpec((tm, tn), lambda i, j: (i, j))]
    if has_small:
        in_specs.append(pl.BlockSpec((d, LANES), lambda i, j: (0, 0)))
        args.append(w_small)
        out_shape.append(jax.ShapeDtypeStruct((m, LANES), F32))
        out_specs.append(pl.BlockSpec((tm, LANES), lambda i, j: (i, 0)))
    res = pl.pallas_call(
        functools.partial(_norm_matmul_kernel, norm_lo=norm_lo, norm_hi=norm_hi, tn=tn,
                          has_small=has_small),
        grid=(m // tm, n // tn),
        in_specs=in_specs,
        out_specs=out_specs,
        out_shape=out_shape,
        scratch_shapes=[pltpu.VMEM((tm, d), BF16)],
        compiler_params=_params(("arbitrary", "arbitrary")),
        name="norm_matmul",
    )(*args)
    return res if has_small else res[0]


def _prefix_sum_rows(tri_bf, x):
    p1 = x.astype(BF16)
    r1 = x - p1.astype(F32)
    p2 = r1.astype(BF16)
    p3 = (r1 - p2.astype(F32)).astype(BF16)
    dot = lambda p: jnp.dot(tri_bf, p, preferred_element_type=F32)
    return (dot(p1) + dot(p2)) + dot(p3)


def _gla_kernel(q_ref, k_ref, v_ref, r_ref, sm_ref, au_ref, ab_ref, gn_ref, out_ref, st_ref,
                *, chunk, nsub, scale, hps, dk, dv):
    @pl.when(pl.program_id(2) == 0)
    def _():
        st_ref[...] = jnp.zeros_like(st_ref)

    row = lax.broadcasted_iota(jnp.int32, (chunk, chunk), 0)
    col = lax.broadcasted_iota(jnp.int32, (chunk, chunk), 1)
    causal = row >= col
    tri = causal.astype(BF16)
    key_row = lax.broadcasted_iota(jnp.int32, (chunk, 1), 0)
    for s in range(nsub):
        rows = pl.ds(s * chunk, chunk)
        ga = sm_ref[rows, :].astype(BF16)
        for hh in range(hps):
            kc = slice(hh * dk, (hh + 1) * dk)
            vc = slice(hh * dv, (hh + 1) * dv)
            z = jnp.dot(ga, au_ref[:, kc], preferred_element_type=F32)
            la = _log_sigmoid(z + ab_ref[:, kc]) * (1.0 / GLA_GATE_NORM)
            b = _prefix_sum_rows(tri, la)
            b_last = b[chunk - 1:chunk, :]
            q = q_ref[rows, kc].astype(F32)
            k = k_ref[rows, kc].astype(F32)
            v = v_ref[rows, vc]
            qd = (q * (scale * jnp.exp(b))).astype(BF16)
            kl = (k * jnp.exp(b_last - b)).astype(BF16)
            blocks = []
            for gi in range(chunk // GLA_SUB):
                grp = slice(gi * GLA_SUB, (gi + 1) * GLA_SUB)
                if gi == 0:
                    q_fac, k_arg = b[grp], -b
                else:
                    r = b[gi * GLA_SUB - 1:gi * GLA_SUB, :]
                    q_fac, k_arg = b[grp] - r, r - b
                k_arg = jnp.where(key_row < (gi + 1) * GLA_SUB, k_arg, -jnp.inf)
                qg = (q[grp] * (scale * jnp.exp(q_fac))).astype(BF16)
                kg = (k * jnp.exp(k_arg)).astype(BF16)
                blocks.append(lax.dot_general(qg, kg, NT_DIMS, preferred_element_type=F32))
            sc = jnp.where(causal, jnp.concatenate(blocks, axis=0), 0.0).astype(BF16)
            st = st_ref[hh]
            o = jnp.dot(sc, v, preferred_element_type=F32) + lax.dot_general(
                qd, st.astype(BF16), NT_DIMS, preferred_element_type=F32)
            st_ref[hh] = st * jnp.exp(b_last) + lax.dot_general(
                v, kl, TN_DIMS, preferred_element_type=F32)
            r = r_ref[rows, vc].astype(F32)
            out_ref[rows, vc] = (_rms(o, gn_ref[...]) * (r * jax.nn.sigmoid(r))).astype(out_ref.dtype)


def _gla(proj, small, alpha_up_pad, alpha_bias, out_norm_g, *, batch, seq, dk, dv, offs, rows, hps):
    t = batch * seq
    nblk = seq // rows
    h = GLA_HEADS
    wk, wv = hps * dk, hps * dv
    qo, ko, vo, ro = (offs[0] // wk, offs[1] // wk, offs[2] // wv, offs[3] // wv)
    rmap = lambda b, hh, c: b * nblk + c
    return pl.pallas_call(
        functools.partial(_gla_kernel, chunk=GLA_CHUNK, nsub=rows // GLA_CHUNK, scale=dk ** -0.5,
                          hps=hps, dk=dk, dv=dv),
        grid=(batch, h // hps, nblk),
        in_specs=[
            pl.BlockSpec((rows, wk), lambda b, hh, c: (rmap(b, hh, c), qo + hh)),
            pl.BlockSpec((rows, wk), lambda b, hh, c: (rmap(b, hh, c), ko + hh)),
            pl.BlockSpec((rows, wv), lambda b, hh, c: (rmap(b, hh, c), vo + hh)),
            pl.BlockSpec((rows, wv), lambda b, hh, c: (rmap(b, hh, c), ro + hh)),
            pl.BlockSpec((rows, LANES), lambda b, hh, c: (rmap(b, hh, c), 0)),
            pl.BlockSpec((LANES, wk), lambda b, hh, c: (0, hh)),
            pl.BlockSpec((1, wk), lambda b, hh, c: (0, hh)),
            pl.BlockSpec((1, dv), lambda b, hh, c: (0, 0)),
        ],
        out_specs=pl.BlockSpec((rows, wv), lambda b, hh, c: (rmap(b, hh, c), hh)),
        out_shape=jax.ShapeDtypeStruct((t, h * dv), BF16),
        scratch_shapes=[pltpu.VMEM((hps, dv, dk), F32)],
        compiler_params=_params(("arbitrary", "arbitrary", "arbitrary")),
        name="gla",
    )(proj, proj, proj, proj, small, alpha_up_pad, alpha_bias, out_norm_g)


def _fcum_kernel(sm_ref, bias_ref, out_ref, carry_ref, *, tb):
    @pl.when(pl.program_id(1) == 0)
    def _():
        carry_ref[...] = jnp.zeros_like(carry_ref)

    row = lax.broadcasted_iota(jnp.int32, (tb, tb), 0)
    col = lax.broadcasted_iota(jnp.int32, (tb, tb), 1)
    tri = (row >= col).astype(BF16)
    lf = _log_sigmoid(sm_ref[...] + bias_ref[...])
    c = _prefix_sum_rows(tri, lf) + carry_ref[...]
    out_ref[...] = c
    carry_ref[...] = c[tb - 1:tb, :]


def _fcum(small, bias_pad, *, batch, seq, tb):
    nb = seq // tb
    return pl.pallas_call(
        functools.partial(_fcum_kernel, tb=tb),
        grid=(batch, nb),
        in_specs=[pl.BlockSpec((tb, LANES), lambda b, i: (b * nb + i, 0)),
                  pl.BlockSpec((1, LANES), lambda b, i: (0, 0))],
        out_specs=pl.BlockSpec((tb, LANES), lambda b, i: (b * nb + i, 0)),
        out_shape=jax.ShapeDtypeStruct(small.shape, F32),
        scratch_shapes=[pltpu.VMEM((1, LANES), F32)],
        compiler_params=_params(("arbitrary", "arbitrary")),
        name="forget_cumsum",
    )(small, bias_pad)


def _fox_kernel(qt_ref, kt_ref, q_ref, k_ref, v_ref, c_ref, out_ref, m_ref, mn_ref, a_ref, l_ref,
                acc_ref, s_ref, p_ref, kt_s, *, tq, rb, grp):
    p = pl.program_id(2)
    qi = qt_ref[p]
    ki = kt_ref[p]

    @pl.when(ki == 0)
    def _():
        m_ref[...] = jnp.full_like(m_ref, -jnp.inf)
        l_ref[...] = jnp.zeros_like(l_ref)
        acc_ref[...] = jnp.zeros_like(acc_ref)

    def update(diag):
        c = c_ref[0]

        def ncols(g):
            return (g + 1) * grp if diag else tq

        kt_s[...] = k_ref[...].T

        def qk(g):
            n = ncols(g)
            s_ref[g % 2, :, 0:n] = jnp.dot(q_ref[g * grp:(g + 1) * grp, :], kt_s[:, 0:n],
                                           preferred_element_type=F32)

        def pv(g):
            n = ncols(g)
            rows = slice(g * grp, (g + 1) * grp)
            acc_ref[rows, :] += jnp.dot(p_ref[g % 2, :, 0:n], v_ref[0:n, :],
                                        preferred_element_type=F32)

        def softmax(g):
            buf = g % 2
            strips = []
            for st in range(grp // rb):
                r0 = g * grp + st * rb
                visible = (r0 + rb - 1) // LANES + 1 if diag else tq // LANES
                strips.append((slice(st * rb, (st + 1) * rb), slice(r0, r0 + rb), r0, visible))

            def logits(rows_l, r0, t):
                cols = slice(t * LANES, (t + 1) * LANES)
                s = s_ref[buf, rows_l, cols] - c[:, cols]
                if diag and (t + 1) * LANES - 1 > r0:
                    row = r0 + lax.broadcasted_iota(jnp.int32, (rb, LANES), 0)
                    col = t * LANES + lax.broadcasted_iota(jnp.int32, (rb, LANES), 1)
                    s = jnp.where(row >= col, s, -jnp.inf)
                return s

            for rows_l, rows, r0, visible in strips:
                part = logits(rows_l, r0, 0)
                for t in range(1, visible):
                    part = jnp.maximum(part, logits(rows_l, r0, t))
                m_old = m_ref[rows, :]
                m_new = jnp.maximum(m_old, jnp.max(part, axis=-1, keepdims=True))
                a_ref[rows, :] = jnp.exp2(m_old - m_new)
                mn_ref[rows, :] = m_new
            for rows_l, rows, r0, visible in strips:
                m_new = mn_ref[rows, :]
                m_ref[rows, :] = m_new
                part = jnp.zeros((rb, LANES), F32)
                for t in range(ncols(g) // LANES):
                    cols = slice(t * LANES, (t + 1) * LANES)
                    if t < visible:
                        pr = jnp.exp2(logits(rows_l, r0, t) - m_new)
                        part = part + pr
                        p_ref[buf, rows_l, cols] = pr.astype(BF16)
                    else:
                        p_ref[buf, rows_l, cols] = jnp.zeros((rb, LANES), BF16)
                alpha = a_ref[rows, :]
                l_ref[rows, :] = alpha * l_ref[rows, :] + part
                acc_ref[rows, :] = alpha * acc_ref[rows, :]

        ngrp = tq // grp
        qk(0)
        for g in range(ngrp):
            if g + 1 < ngrp:
                qk(g + 1)
            softmax(g)
            if g >= 1:
                pv(g - 1)
        pv(ngrp - 1)

    @pl.when(ki < qi)
    def _():
        update(False)

    @pl.when(ki == qi)
    def _():
        update(True)
        denom = jnp.sum(l_ref[...], axis=-1, keepdims=True)
        out_ref[...] = (acc_ref[...] / denom).astype(out_ref.dtype)


def _fox(proj, c_row, *, batch, seq, dh, offs, tq):
    t = batch * seq
    nq = seq // tq
    pairs = [(a, b) for a in range(nq) for b in range(a + 1)]
    qt = jnp.asarray([a for a, _ in pairs], jnp.int32)
    kt = jnp.asarray([b for _, b in pairs], jnp.int32)
    qo, ko, vo = (o // dh for o in offs)
    grp = min(tq, FOX_GROUP)
    grid_spec = pltpu.PrefetchScalarGridSpec(
        num_scalar_prefetch=2,
        grid=(batch, FOX_HEADS, len(pairs)),
        in_specs=[
            pl.BlockSpec((tq, dh), lambda b, h, p, qt, kt: (b * nq + qt[p], qo + h)),
            pl.BlockSpec((tq, dh), lambda b, h, p, qt, kt: (b * nq + kt[p], ko + h)),
            pl.BlockSpec((tq, dh), lambda b, h, p, qt, kt: (b * nq + kt[p], vo + h)),
            pl.BlockSpec((1, 1, tq), lambda b, h, p, qt, kt: (b * FOX_HEADS + h, 0, kt[p])),
        ],
        out_specs=pl.BlockSpec((tq, dh), lambda b, h, p, qt, kt: (b * nq + qt[p], h)),
        scratch_shapes=[pltpu.VMEM((tq, LANES), F32)] * 4 + [
                        pltpu.VMEM((tq, dh), F32), pltpu.VMEM((2, grp, tq), F32),
                        pltpu.VMEM((2, grp, tq), BF16), pltpu.VMEM((dh, tq), BF16)],
    )
    return pl.pallas_call(
        functools.partial(_fox_kernel, tq=tq, rb=FOX_STRIP, grp=grp),
        grid_spec=grid_spec,
        out_shape=jax.ShapeDtypeStruct((t, FOX_HEADS * dh), BF16),
        compiler_params=_params(("arbitrary", "arbitrary", "arbitrary")),
        name="fox_attention",
    )(qt, kt, proj, proj, proj, c_row)


def _merge_kernel(a_ref, f_ref, ga_ref, gf_ref, wa_ref, wf_ref, out_ref):
    ya = jnp.dot(a_ref[...], wa_ref[...], preferred_element_type=F32)
    yf = jnp.dot(f_ref[...], wf_ref[...], preferred_element_type=F32)
    out_ref[...] = (jax.nn.sigmoid(ga_ref[...].astype(F32)) * ya
                    + jax.nn.sigmoid(gf_ref[...].astype(F32)) * yf).astype(out_ref.dtype)


def _merge(o_gla, o_fox, proj, w_gla, w_fox, *, offs, tm, tn):
    m, kdim = o_gla.shape
    n = w_gla.shape[1]
    go, fo = offs[0] // tn, offs[1] // tn
    return pl.pallas_call(
        _merge_kernel,
        grid=(m // tm, n // tn),
        in_specs=[
            pl.BlockSpec((tm, kdim), lambda i, j: (i, 0)),
            pl.BlockSpec((tm, o_fox.shape[1]), lambda i, j: (i, 0)),
            pl.BlockSpec((tm, tn), lambda i, j: (i, go + j)),
            pl.BlockSpec((tm, tn), lambda i, j: (i, fo + j)),
            pl.BlockSpec((kdim, tn), lambda i, j: (0, j)),
            pl.BlockSpec((o_fox.shape[1], tn), lambda i, j: (0, j)),
        ],
        out_specs=pl.BlockSpec((tm, tn), lambda i, j: (i, j)),
        out_shape=jax.ShapeDtypeStruct((m, n), BF16),
        compiler_params=_params(("arbitrary", "arbitrary")),
        name="gated_merge",
    )(o_gla, o_fox, proj, proj, w_gla, w_fox)


def _resid_matmul_kernel(a_ref, w_ref, res_ref, out_ref):
    out_ref[...] = res_ref[...] + jnp.dot(a_ref[...], w_ref[...], preferred_element_type=F32)


def _resid_matmul(a, w, res, *, tm, tn):
    m, kdim = a.shape
    n = w.shape[1]
    return pl.pallas_call(
        _resid_matmul_kernel,
        grid=(m // tm, n // tn),
        in_specs=[pl.BlockSpec((tm, kdim), lambda i, j: (i, 0)),
                  pl.BlockSpec((kdim, tn), lambda i, j: (0, j)),
                  pl.BlockSpec((tm, tn), lambda i, j: (i, j))],
        out_specs=pl.BlockSpec((tm, tn), lambda i, j: (i, j)),
        out_shape=jax.ShapeDtypeStruct((m, n), F32),
        compiler_params=_params(("arbitrary", "arbitrary")),
        name="resid_matmul",
    )(a, w, res)


def _memory_kernel(x_ref, gn_ref, wq_ref, gq_ref, k_ref, v_ref, wo_ref, gm_ref, wr_ref, br_ref,
                   x2_ref, rl_ref, xsp_ref, *, dh):
    x1 = x_ref[...]
    hb = _rms(x1, gn_ref[...]).astype(BF16)
    q = jnp.dot(hb, wq_ref[...], preferred_element_type=F32)
    outs = []
    for h in range(MEM_HEADS):
        sl = slice(h * dh, (h + 1) * dh)
        qn = _rms(q[:, sl], gq_ref[...]).astype(BF16)
        s = lax.dot_general(qn, k_ref[:, sl], NT_DIMS, preferred_element_type=F32)
        s = s - jnp.max(s, axis=-1, keepdims=True)
        p = jnp.exp(s)
        p = p / jnp.sum(p, axis=-1, keepdims=True)
        outs.append(jnp.dot(p.astype(BF16), v_ref[:, sl], preferred_element_type=F32).astype(BF16))
    o = jnp.concatenate(outs, axis=-1)
    x2 = x1 + jnp.dot(o, wo_ref[...], preferred_element_type=F32)
    x2_ref[...] = x2
    xs = _rms(x2, gm_ref[...])
    rl_ref[...] = jnp.dot(xs.astype(BF16), wr_ref[...], preferred_element_type=F32) + br_ref[...]
    xsp_ref[...] = _pack_bf16_pairs(xs)


def _memory(x1, mem_norm_g, wq, gq, kv, wo, moe_norm_g, wr_pad, br_pad, *, batch, seq,
            mem_len, tm):
    t, d = x1.shape
    w = wq.shape[1]
    dh = w // MEM_HEADS
    nblk = seq // tm
    const = lambda i: (0, 0)
    return pl.pallas_call(
        functools.partial(_memory_kernel, dh=dh),
        grid=(t // tm,),
        in_specs=[
            pl.BlockSpec((tm, d), lambda i: (i, 0)),
            pl.BlockSpec((1, d), const),
            pl.BlockSpec((d, w), const),
            pl.BlockSpec((1, dh), const),
            pl.BlockSpec((mem_len, w), lambda i: (i // nblk, 0)),
            pl.BlockSpec((mem_len, w), lambda i: (i // nblk, 1)),
            pl.BlockSpec((w, d), const),
            pl.BlockSpec((1, d), const),
            pl.BlockSpec((d, LANES), const),
            pl.BlockSpec((1, LANES), const),
        ],
        out_specs=[pl.BlockSpec((tm, d), lambda i: (i, 0)),
                   pl.BlockSpec((tm, LANES), lambda i: (i, 0)),
                   pl.BlockSpec((tm, d // 2), lambda i: (i, 0))],
        out_shape=[jax.ShapeDtypeStruct((t, d), F32), jax.ShapeDtypeStruct((t, LANES), F32),
                   jax.ShapeDtypeStruct((t, d // 2), jnp.uint32)],
        compiler_params=_params(("arbitrary",)),
        name="memory_block",
    )(x1, mem_norm_g, wq, gq, kv, kv, wo, moe_norm_g, wr_pad, br_pad)


def _route_kernel(rl_ref, route_ref, counts_ref, carry_ref, *, tr, n_experts):
    @pl.when(pl.program_id(0) == 0)
    def _():
        carry_ref[...] = jnp.zeros_like(carry_ref)

    lane = lax.broadcasted_iota(jnp.int32, (tr, LANES), 1)
    lg = jnp.where(lane < n_experts, rl_ref[...], -jnp.inf)
    vals, hots = [], []
    for _ in range(TOP_K):
        mx = jnp.max(lg, axis=-1, keepdims=True)
        idx = jnp.min(jnp.where(lg == mx, lane, LANES), axis=-1, keepdims=True)
        hot = lane == idx
        vals.append(mx)
        hots.append(hot)
        lg = jnp.where(hot, -jnp.inf, lg)
    exps = [jnp.exp(v - vals[0]) for v in vals]
    denom = exps[0]
    for e in exps[1:]:
        denom = denom + e
    onehot = hots[0]
    for hsel in hots[1:]:
        onehot = jnp.logical_or(onehot, hsel)
    onehot_f = onehot.astype(F32)
    row = lax.broadcasted_iota(jnp.int32, (tr, tr), 0)
    col = lax.broadcasted_iota(jnp.int32, (tr, tr), 1)
    strict = (row > col).astype(BF16)
    before = jnp.dot(strict, onehot_f.astype(BF16), preferred_element_type=F32) + carry_ref[...]
    route = jnp.zeros((tr, LANES), F32)
    for k in range(TOP_K):
        idx_f = jnp.sum(jnp.where(hots[k], lane, 0), axis=-1, keepdims=True).astype(F32)
        rank = jnp.sum(jnp.where(hots[k], before, 0.0), axis=-1, keepdims=True)
        route = jnp.where(lane == k, idx_f, route)
        route = jnp.where(lane == TOP_K + k, exps[k] / denom, route)
        route = jnp.where(lane == 2 * TOP_K + k, rank, route)
    route_ref[...] = route
    total = carry_ref[...] + jnp.sum(onehot_f, axis=0, keepdims=True)
    carry_ref[...] = total
    counts_ref[...] = jnp.broadcast_to(total, counts_ref.shape)


def _route(rlogits, *, n_experts, tr):
    t = rlogits.shape[0]
    return pl.pallas_call(
        functools.partial(_route_kernel, tr=tr, n_experts=n_experts),
        grid=(t // tr,),
        in_specs=[pl.BlockSpec((tr, LANES), lambda i: (i, 0))],
        out_specs=[pl.BlockSpec((tr, LANES), lambda i: (i, 0)),
                   pl.BlockSpec((8, LANES), lambda i: (0, 0))],
        out_shape=[jax.ShapeDtypeStruct((t, LANES), F32), jax.ShapeDtypeStruct((8, LANES), F32)],
        scratch_shapes=[pltpu.VMEM((1, LANES), F32)],
        compiler_params=_params(("arbitrary",)),
        name="route_topk",
    )(rlogits)


def _gather_kernel(tok_ref, nused_ref, x_hbm, out_ref, buf, sem, *, rows):
    i = pl.program_id(0)
    nused = nused_ref[0]

    def issue(blk, slot):
        def body(r8, carry):
            for j in range(SUBLANES):
                tok = tok_ref[blk * rows + r8 * SUBLANES + j]
                pltpu.make_async_copy(
                    x_hbm.at[tok >> 3, pl.ds(tok & (SUBLANES - 1), 1)],
                    buf.at[slot, r8, pl.ds(j, 1)], sem.at[slot]).start()
            return carry
        lax.fori_loop(0, rows // SUBLANES, body, 0)

    @pl.when(i == 0)
    def _():
        issue(0, 0)

    @pl.when(i + 1 < nused)
    def _():
        issue(i + 1, (i + 1) % 2)

    @pl.when(i < nused)
    def _():
        slot = i % 2
        pltpu.make_async_copy(x_hbm.at[pl.ds(0, rows // SUBLANES)], buf.at[slot],
                              sem.at[slot]).wait()
        lo, hi = _unpack_bf16_pairs(buf[slot].reshape(rows, -1))
        half = lo.shape[1]
        out_ref[:, :half] = lo.astype(out_ref.dtype)
        out_ref[:, half:] = hi.astype(out_ref.dtype)

    @pl.when(i >= nused)
    def _():
        out_ref[...] = jnp.zeros_like(out_ref)


def _gather_rows(slot_tok, nused, xs_packed, *, rows):
    p = slot_tok.shape[0]
    half = xs_packed.shape[1]
    grid_spec = pltpu.PrefetchScalarGridSpec(
        num_scalar_prefetch=2,
        grid=(p // rows,),
        in_specs=[pl.BlockSpec(memory_space=pl.ANY)],
        out_specs=pl.BlockSpec((rows, 2 * half), lambda i, tok, nu: (i, 0)),
        scratch_shapes=[pltpu.VMEM((2, rows // SUBLANES, SUBLANES, half), jnp.uint32),
                        pltpu.SemaphoreType.DMA((2,))],
    )
    return pl.pallas_call(
        functools.partial(_gather_kernel, rows=rows),
        grid_spec=grid_spec,
        out_shape=jax.ShapeDtypeStruct((p, 2 * half), BF16),
        compiler_params=_params(("arbitrary",)),
        name="moe_gather",
    )(slot_tok, nused, xs_packed.reshape(-1, SUBLANES, half))


def _start_all(copies):
    for c in copies:
        c.start()


def _wait_all(copies):
    for c in copies:
        c.wait()


def _weight_pipeline(step, flag, tabs, copies_for, convert):
    exp_ref, tile_ref, gslot_ref, nexp_ref, ntile_ref = tabs

    @pl.when((flag & 2) != 0)
    def _():
        slot = gslot_ref[step]
        mine = copies_for(exp_ref[step], tile_ref[step], slot)

        @pl.when(step == 0)
        def _():
            _start_all(mine)

        _wait_all(mine)
        convert(slot)

        @pl.when((flag & 4) != 0)
        def _():
            _start_all(copies_for(nexp_ref[step], ntile_ref[step], 1 - slot))


def _expert_up_kernel(blk_ref, tile_ref, exp_ref, flag_ref, gslot_ref, nexp_ref, ntile_ref,
                      x_ref, w_hbm, bg_ref, bl_ref, out_ref, wbuf, wg_bf, wl_bf, sem, *, tf, nt):
    step = pl.program_id(0)
    flag = flag_ref[step]

    def copies_for(e, tile, slot):
        return [pltpu.make_async_copy(
            w_hbm.at[e, :, pl.ds(pl.multiple_of((half * nt + tile) * tf, tf), tf)],
            wbuf.at[slot, half], sem.at[slot, half]) for half in range(2)]

    def convert(slot):
        wg_bf[...] = wbuf[slot, 0].astype(BF16)
        wl_bf[...] = wbuf[slot, 1].astype(BF16)

    _weight_pipeline(step, flag, (exp_ref, tile_ref, gslot_ref, nexp_ref, ntile_ref), copies_for,
                     convert)

    @pl.when((flag & 1) != 0)
    def _():
        x = x_ref[...]
        glu = jnp.dot(x, wg_bf[...], preferred_element_type=F32) + bg_ref[...]
        lin = jnp.dot(x, wl_bf[...], preferred_element_type=F32) + bl_ref[...]
        glu = jnp.minimum(glu, SWIGLU_LIMIT)
        lin = jnp.clip(lin, -SWIGLU_LIMIT, SWIGLU_LIMIT)
        out_ref[...] = (glu * jax.nn.sigmoid(SWIGLU_ALPHA * glu) * (lin + 1.0)).astype(out_ref.dtype)

    @pl.when((flag & 1) == 0)
    def _():
        out_ref[...] = jnp.zeros_like(out_ref)


def _expert_up(tables, xg, w_up, b_up, *, tm, tf):
    p, d = xg.shape
    n_e, _, ff2 = w_up.shape
    ff = ff2 // 2
    nt = ff // tf
    nsteps = tables[0].shape[0]
    bias = lambda half: pl.BlockSpec((None, 1, tf),
                                     lambda s, blk, tile, ex, *_: (ex[s], 0, half * nt + tile[s]))
    grid_spec = pltpu.PrefetchScalarGridSpec(
        num_scalar_prefetch=len(tables),
        grid=(nsteps,),
        in_specs=[
            pl.BlockSpec((tm, d), lambda s, blk, *_: (blk[s], 0)),
            pl.BlockSpec(memory_space=pl.ANY),
            bias(0), bias(1),
        ],
        out_specs=pl.BlockSpec((tm, tf), lambda s, blk, tile, *_: (blk[s], tile[s])),
        scratch_shapes=[pltpu.VMEM((2, 2, d, tf), F32), pltpu.VMEM((d, tf), BF16),
                        pltpu.VMEM((d, tf), BF16), pltpu.SemaphoreType.DMA((2, 2))],
    )
    return pl.pallas_call(
        functools.partial(_expert_up_kernel, tf=tf, nt=nt),
        grid_spec=grid_spec,
        out_shape=jax.ShapeDtypeStruct((p, ff), BF16),
        compiler_params=_params(("arbitrary",), VMEM_LIMIT_MAX),
        name="expert_up",
    )(*tables, xg, w_up, b_up, b_up)


def _expert_down_kernel(blk_ref, tile_ref, exp_ref, flag_ref, gslot_ref, nexp_ref, ntile_ref,
                        a_ref, w_hbm, b_ref, out_ref, wbuf, w_bf, sem, *, tn):
    step = pl.program_id(0)
    flag = flag_ref[step]

    def copies_for(e, tile, slot):
        return [pltpu.make_async_copy(w_hbm.at[e, :, pl.ds(pl.multiple_of(tile * tn, tn), tn)],
                                      wbuf.at[slot], sem.at[slot])]

    def convert(slot):
        w_bf[...] = wbuf[slot].astype(BF16)

    _weight_pipeline(step, flag, (exp_ref, tile_ref, gslot_ref, nexp_ref, ntile_ref), copies_for,
                     convert)

    @pl.when((flag & 1) != 0)
    def _():
        y = jnp.dot(a_ref[...], w_bf[...], preferred_element_type=F32) + b_ref[...]
        out_ref[...] = _pack_bf16_pairs(y)

    @pl.when((flag & 1) == 0)
    def _():
        out_ref[...] = jnp.zeros_like(out_ref)


def _expert_down(tables, act, w_down, b_down, *, tm, tn):
    p, ff = act.shape
    d = w_down.shape[2]
    nsteps = tables[0].shape[0]
    grid_spec = pltpu.PrefetchScalarGridSpec(
        num_scalar_prefetch=len(tables),
        grid=(nsteps,),
        in_specs=[
            pl.BlockSpec((tm, ff), lambda s, blk, *_: (blk[s], 0)),
            pl.BlockSpec(memory_space=pl.ANY),
            pl.BlockSpec((None, 1, tn), lambda s, blk, tile, ex, *_: (ex[s], 0, tile[s])),
        ],
        out_specs=pl.BlockSpec((tm, tn // 2), lambda s, blk, tile, *_: (blk[s], tile[s])),
        scratch_shapes=[pltpu.VMEM((2, ff, tn), F32), pltpu.VMEM((ff, tn), BF16),
                        pltpu.SemaphoreType.DMA((2,))],
    )
    return pl.pallas_call(
        functools.partial(_expert_down_kernel, tn=tn),
        grid_spec=grid_spec,
        out_shape=jax.ShapeDtypeStruct((p, d // 2), jnp.uint32),
        compiler_params=_params(("arbitrary",)),
        name="expert_down",
    )(*tables, act, w_down, b_down)


def _expert_schedule(counts, *, tm, n_tiles, n_blocks):
    n_e = counts.shape[0]
    nb = (counts + tm - 1) // tm
    bend = jnp.cumsum(nb)
    bstart = bend - nb
    steps_e = nb * n_tiles
    cs = jnp.cumsum(steps_e)
    total = cs[-1]
    s = jnp.arange(n_blocks * n_tiles, dtype=jnp.int32)
    sc = jnp.minimum(s, total - 1)
    e = jnp.sum((cs[None, :] <= sc[:, None]).astype(jnp.int32), axis=1)
    e = jnp.minimum(e, n_e - 1)
    r = sc - (cs[e] - steps_e[e])
    nbe = jnp.maximum(nb[e], 1)
    valid = s < total
    n_unused = jnp.maximum(n_blocks - bend[-1], 1)
    u = jnp.maximum(s - total, 0)
    live_tile = r // nbe
    tile = jnp.where(valid, live_tile, u // n_unused)
    blk = jnp.where(valid, bstart[e] + r % nbe, bend[-1] + u % n_unused)
    first = jnp.logical_and(valid, r % nbe == 0)
    ids = jnp.where(nb > 0, jnp.arange(n_e, dtype=jnp.int32), n_e)
    later = jnp.concatenate([lax.cummin(ids, reverse=True)[1:], jnp.full((1,), n_e, jnp.int32)])
    wraps = live_tile + 1 >= n_tiles
    nexp = jnp.where(wraps, later[e], e)
    ntile = jnp.where(wraps, 0, live_tile + 1)
    flag = (valid.astype(jnp.int32) + 2 * first.astype(jnp.int32)
            + 4 * jnp.logical_and(first, nexp < n_e).astype(jnp.int32))
    gslot = (jnp.cumsum(first.astype(jnp.int32)) - 1) % 2
    i32 = lambda a: a.astype(jnp.int32)
    return (i32(blk), i32(tile), i32(e), flag, i32(gslot), i32(jnp.minimum(nexp, n_e - 1)),
            i32(ntile))


def _combine_kernel(slot_ref, x_ref, route_ref, y_hbm, out_ref, buf, sem, *, tc, tn):
    i = pl.program_id(0)
    n = pl.num_programs(0)

    def issue(blk, bslot):
        def body(r, carry):
            for k in range(TOP_K):
                src = slot_ref[(blk * tc + r) * TOP_K + k]
                pltpu.make_async_copy(y_hbm.at[pl.ds(src, 1)], buf.at[bslot, k, pl.ds(r, 1)],
                                      sem.at[bslot]).start()
            return carry
        lax.fori_loop(0, tc, body, 0, unroll=4)

    @pl.when(i == 0)
    def _():
        issue(0, 0)

    @pl.when(i + 1 < n)
    def _():
        issue(i + 1, (i + 1) % 2)

    bslot = i % 2
    for k in range(TOP_K):
        pltpu.make_async_copy(y_hbm.at[pl.ds(0, tc)], buf.at[bslot, k], sem.at[bslot]).wait()
    gates = [route_ref[:, TOP_K + k:TOP_K + k + 1] for k in range(TOP_K)]
    hw = tn // 2
    for j in range(out_ref.shape[1] // tn):
        parts = [_unpack_bf16_pairs(buf[bslot, k, :, j * hw:(j + 1) * hw]) for k in range(TOP_K)]
        for half in range(2):
            cols = slice(j * tn + half * hw, j * tn + (half + 1) * hw)
            acc = x_ref[:, cols]
            for k in range(TOP_K):
                acc = acc + gates[k] * parts[k][half]
            out_ref[:, cols] = acc


def _combine(slot_flat, x2, route, y, *, tc, tn):
    t, d = x2.shape
    grid_spec = pltpu.PrefetchScalarGridSpec(
        num_scalar_prefetch=1,
        grid=(t // tc,),
        in_specs=[pl.BlockSpec((tc, d), lambda i, sl: (i, 0)),
                  pl.BlockSpec((tc, LANES), lambda i, sl: (i, 0)),
                  pl.BlockSpec(memory_space=pl.ANY)],
        out_specs=pl.BlockSpec((tc, d), lambda i, sl: (i, 0)),
        scratch_shapes=[pltpu.VMEM((2, TOP_K, tc, d // 2), jnp.uint32),
                        pltpu.SemaphoreType.DMA((2,))],
    )
    return pl.pallas_call(
        functools.partial(_combine_kernel, tc=tc, tn=tn),
        grid_spec=grid_spec,
        out_shape=jax.ShapeDtypeStruct((t, d), F32),
        compiler_params=_params(("arbitrary",)),
        name="moe_combine",
    )(slot_flat, x2, route, y)


def _pad_lanes(a, offset=0):
    width = a.shape[-1]
    return jnp.pad(a, [(0, 0)] * (a.ndim - 1) + [(offset, LANES - offset - width)])


def _layer(x2d, mem2d, lw, *, batch, seq, mem_len):
    t, d = x2d.shape
    gla_qk = lw["gla_alpha_up"].shape[1]
    gla_rank = lw["gla_alpha_up"].shape[0]
    gla_v = lw["w_branch_gla"].shape[0]
    fox_w = lw["w_branch_fox"].shape[0]
    dk, dv = gla_qk // GLA_HEADS, gla_v // GLA_HEADS
    dh = fox_w // FOX_HEADS
    mem_w = lw["mem_w_q"].shape[1]
    mem_dh = mem_w // MEM_HEADS
    n_experts = lw["w_router"].shape[1]
    ff = lw["w_expert_down"].shape[1]

    sizes = (gla_qk, gla_qk, gla_v, gla_v, gla_rank, fox_w, fox_w, fox_w, FOX_HEADS, d, d)
    starts = [0]
    for sz in sizes:
        starts.append(starts[-1] + sz)
    w_in = lw["w_in"]
    seg = lambda i: w_in[:, starts[i]:starts[i + 1]]
    big_ids = (0, 1, 2, 3, 5, 6, 7, 9, 10)
    w_big = jnp.concatenate([seg(i).astype(BF16) for i in big_ids], axis=1)
    w_small = _pad_lanes(jnp.concatenate([seg(4), seg(8)], axis=1)).astype(BF16)
    off = {}
    acc = 0
    for i in big_ids:
        off[i] = acc
        acc += sizes[i]
    n_big = acc
    tn1 = 1024
    colscale = jnp.ones((1, n_big), F32)
    colscale = colscale.at[:, off[5]:off[5] + fox_w].set(
        jnp.tile(lw["fox_q_norm_g"] * (dh ** -0.5 * LOG2E), FOX_HEADS)[None])
    colscale = colscale.at[:, off[6]:off[6] + fox_w].set(jnp.tile(lw["fox_k_norm_g"], FOX_HEADS)[None])
    proj, small = _norm_matmul(
        x2d, lw["mix_norm_g"][None], w_big, colscale, w_small,
        tm=1024, tn=tn1, norm_lo=off[5] // tn1, norm_hi=(off[6] + fox_w) // tn1)

    alpha_up_pad = jnp.pad(lw["gla_alpha_up"], ((0, LANES - gla_rank), (0, 0))).astype(BF16)
    o_gla = _gla(proj, small, alpha_up_pad, lw["gla_alpha_bias"][None], lw["gla_out_norm_g"][None],
                 batch=batch, seq=seq, dk=dk, dv=dv, offs=(off[0], off[1], off[2], off[3]), rows=256, hps=4)

    c = _fcum(small, _pad_lanes(lw["fox_f_bias"][None], gla_rank), batch=batch, seq=seq, tb=512)
    c_row = c[:, gla_rank:gla_rank + FOX_HEADS].reshape(batch, seq, FOX_HEADS)
    c_row = c_row.transpose(0, 2, 1).reshape(batch * FOX_HEADS, 1, seq) * LOG2E
    o_fox = _fox(proj, c_row, batch=batch, seq=seq, dh=dh, offs=(off[5], off[6], off[7]),
                 tq=min(seq, 2048))

    merged = _merge(o_gla, o_fox, proj, lw["w_branch_gla"].astype(BF16),
                    lw["w_branch_fox"].astype(BF16), offs=(off[9], off[10]), tm=512, tn=1024)
    x1 = _resid_matmul(merged, lw["w_out"].astype(BF16), x2d, tm=1024, tn=1024)

    kv_scale = jnp.concatenate([jnp.tile(lw["mem_k_norm_g"], MEM_HEADS),
                                jnp.ones((mem_w,), F32)])[None]
    kv = _norm_matmul(mem2d, lw["mem_kv_norm_g"][None], lw["mem_w_kv"].astype(BF16), kv_scale, None,
                      tm=mem2d.shape[0], tn=mem_w, norm_lo=0, norm_hi=1)
    x2, rlogits, xs_packed = _memory(
        x1, lw["mem_norm_g"][None], lw["mem_w_q"].astype(BF16),
        (lw["mem_q_norm_g"] * (mem_dh ** -0.5))[None], kv,
        lw["mem_w_o"].astype(BF16), lw["moe_norm_g"][None],
        _pad_lanes(lw["w_router"]).astype(BF16), _pad_lanes(lw["b_router"][None]),
        batch=batch, seq=seq, mem_len=mem_len, tm=512)

    route, counts_f = _route(rlogits, n_experts=n_experts, tr=512)
    tm_e = 512
    a_total = t * TOP_K
    n_blocks = a_total // tm_e + n_experts
    p_rows = n_blocks * tm_e
    counts = counts_f[0, :n_experts].astype(jnp.int32)
    padded = (counts + tm_e - 1) // tm_e * tm_e
    pstart = jnp.cumsum(padded) - padded
    idx = route[:, :TOP_K].astype(jnp.int32)
    rank = route[:, 2 * TOP_K:3 * TOP_K].astype(jnp.int32)
    hit = idx[:, :, None] == jnp.arange(n_experts, dtype=jnp.int32)
    slot = (jnp.sum(jnp.where(hit, pstart, 0), axis=-1) + rank).reshape(a_total)
    slot_tok = (jnp.arange(p_rows, dtype=jnp.int32) % t).at[slot].set(
        jnp.arange(a_total, dtype=jnp.int32) // TOP_K, unique_indices=True,
        mode="promise_in_bounds")

    tf = 1024
    g_rows = 512
    nused = (jnp.sum(padded) // g_rows).astype(jnp.int32).reshape(1)
    xg = _gather_rows(slot_tok, nused, xs_packed, rows=g_rows)
    up_tables = _expert_schedule(counts, tm=tm_e, n_tiles=ff // tf, n_blocks=n_blocks)
    act = _expert_up(up_tables, xg, lw["w_expert_up"], lw["b_expert_up"][:, None, :], tm=tm_e, tf=tf)
    tn_d = 1024
    down_tables = _expert_schedule(counts, tm=tm_e, n_tiles=d // tn_d, n_blocks=n_blocks)
    y = _expert_down(down_tables, act, lw["w_expert_down"], lw["b_expert_down"][:, None, :],
                     tm=tm_e, tn=tn_d)
    return _combine(slot, x2, route, y, tc=256, tn=tn_d)


def kernel(x, mem, mix_norm_g, w_in, gla_alpha_up, gla_alpha_bias, gla_out_norm_g, fox_f_bias, fox_q_norm_g, fox_k_norm_g, w_branch_gla, w_branch_fox, w_out, mem_norm_g, mem_kv_norm_g, mem_w_q, mem_w_kv, mem_q_norm_g, mem_k_norm_g, mem_w_o, moe_norm_g, w_router, b_router, w_expert_up, b_expert_up, w_expert_down, b_expert_down):
    weights = dict(
        mix_norm_g=mix_norm_g, w_in=w_in, gla_alpha_up=gla_alpha_up, gla_alpha_bias=gla_alpha_bias,
        gla_out_norm_g=gla_out_norm_g, fox_f_bias=fox_f_bias, fox_q_norm_g=fox_q_norm_g,
        fox_k_norm_g=fox_k_norm_g, w_branch_gla=w_branch_gla, w_branch_fox=w_branch_fox, w_out=w_out,
        mem_norm_g=mem_norm_g, mem_kv_norm_g=mem_kv_norm_g, mem_w_q=mem_w_q, mem_w_kv=mem_w_kv,
        mem_q_norm_g=mem_q_norm_g, mem_k_norm_g=mem_k_norm_g, mem_w_o=mem_w_o, moe_norm_g=moe_norm_g,
        w_router=w_router, b_router=b_router, w_expert_up=w_expert_up, b_expert_up=b_expert_up,
        w_expert_down=w_expert_down, b_expert_down=b_expert_down)
    batch, seq, d = x.shape
    mem_len = mem.shape[1]
    x2d = x.reshape(batch * seq, d)
    mem2d = mem.reshape(batch * mem_len, d)
    for layer in range(mix_norm_g.shape[0]):
        lw = {name: w[layer] for name, w in weights.items()}
        x2d = _layer(x2d, mem2d, lw, batch=batch, seq=seq, mem_len=mem_len)
    return x2d.reshape(batch, seq, d)
```

```python
import functools

import jax
import jax.numpy as jnp
from jax import lax
from jax.experimental import pallas as pl
from jax.experimental.pallas import tpu as pltpu

F32 = jnp.float32
BF16 = jnp.bfloat16

EPS = 1e-6
LANES = 128
SUBLANES = 8
VMEM_LIMIT = 56 * 1024 * 1024
VMEM_LIMIT_MAX = 61 * 1024 * 1024

GLA_HEADS = 4
GLA_GATE_NORM = 16.0
GLA_CHUNK = 64
GLA_SUB = 16
FOX_HEADS = 16
FOX_STRIP = 16
FOX_GROUP = 128
LOG2E = 1.4426950408889634
MEM_HEADS = 4
TOP_K = 4
SWIGLU_LIMIT = 7.0
SWIGLU_ALPHA = 1.702

TILE = dict(
    w_pack_rows=128,
    proj_tm=1024, proj_tn=1024,
    gla_rows=256, gla_heads_per_step=4,
    fcum_rows=512,
    fox_tq=2048,
    merge_tm=512, merge_tn=1024,
    resid_tm=1024, resid_tn=1024,
    mem_tm=512,
    route_rows=512,
    expert_tm=512, expert_tf=1024, expert_tn=1024,
    gather_rows=512, combine_rows=256,
)

NT_DIMS = (((1,), (1,)), ((), ()))
TN_DIMS = (((0,), (0,)), ((), ()))


def _params(sem, vmem=VMEM_LIMIT):
    return pltpu.CompilerParams(dimension_semantics=sem, vmem_limit_bytes=vmem)


def _rms(xf, g):
    return xf * lax.rsqrt(jnp.mean(xf * xf, axis=-1, keepdims=True) + EPS) * g


def _pack_bf16_pairs(a):
    half = a.shape[1] // 2
    bits = lambda v: lax.bitcast_convert_type(v.astype(BF16).astype(F32), jnp.uint32)
    return (bits(a[:, half:]) & jnp.uint32(0xFFFF0000)) | (bits(a[:, :half]) >> 16)


def _unpack_bf16_pairs(u):
    lo = lax.bitcast_convert_type(u << 16, F32)
    hi = lax.bitcast_convert_type(u & jnp.uint32(0xFFFF0000), F32)
    return lo, hi


def _log_sigmoid(z):
    return jnp.minimum(z, 0.0) - jnp.log1p(jnp.exp(-jnp.abs(z)))


def _w_in_pack_kernel(w_ref, big_ref, small_ref, *, spans, pieces):
    rows, n_cols = w_ref.shape
    lane = lax.broadcasted_iota(jnp.int32, (rows, LANES), 1)
    tile_at = lambda c0: w_ref[:, c0:c0 + LANES]
    for s0, width, d0 in spans:
        sh = s0 % LANES
        a0 = s0 - sh
        for t in range(width // LANES):
            dst = slice(d0 + t * LANES, d0 + (t + 1) * LANES)
            if sh == 0:
                big_ref[:, dst] = tile_at(a0 + t * LANES).astype(BF16)
            elif a0 + (t + 2) * LANES <= n_cols:
                left = pltpu.roll(tile_at(a0 + t * LANES), LANES - sh, 1)
                right = pltpu.roll(tile_at(a0 + (t + 1) * LANES), LANES - sh, 1)
                big_ref[:, dst] = jnp.where(lane < LANES - sh, left, right).astype(BF16)
            else:
                big_ref[:, dst] = jnp.zeros((rows, LANES), BF16)
    side = jnp.zeros((rows, LANES), F32)
    for s0, width, dl in pieces:
        sh = s0 % LANES
        tile = tile_at(s0 - sh)
        if (dl - sh) % LANES:
            tile = pltpu.roll(tile, (dl - sh) % LANES, 1)
        side = jnp.where(jnp.logical_and(lane >= dl, lane < dl + width), tile, side)
    small_ref[...] = side.astype(BF16)


def _w_in_pack(w_in, *, spans, pieces, tr):
    d, n_cols = w_in.shape
    n_big = sum(width for _, width, _ in spans)
    w_big, w_small = pl.pallas_call(
        functools.partial(_w_in_pack_kernel, spans=spans, pieces=pieces),
        grid=(d // tr,),
        in_specs=[pl.BlockSpec((tr, n_cols), lambda i: (i, 0))],
        out_specs=[pl.BlockSpec((tr, n_big), lambda i: (i, 0)),
                   pl.BlockSpec((tr, LANES), lambda i: (i, 0))],
        out_shape=[jax.ShapeDtypeStruct((d, n_big), BF16), jax.ShapeDtypeStruct((d, LANES), BF16)],
        compiler_params=_params(("arbitrary",)),
        name="w_in_pack",
    )(w_in)
    for s0, width, d0 in spans:
        if s0 % LANES and s0 - s0 % LANES + width + LANES > n_cols:
            w_big = w_big.at[:, d0 + width - LANES:d0 + width].set(
                w_in[:, s0 + width - LANES:s0 + width].astype(BF16))
    return w_big, w_small


def _norm_matmul_kernel(*refs, norm_lo, norm_hi, tn, has_small):
    if has_small:
        x_ref, g_ref, w_ref, cs_ref, ws_ref, out_ref, small_ref, h_ref = refs
    else:
        x_ref, g_ref, w_ref, cs_ref, out_ref, h_ref = refs
    j = pl.program_id(1)

    @pl.when(j == 0)
    def _():
        hb = _rms(x_ref[...], g_ref[...]).astype(BF16)
        h_ref[...] = hb
        if has_small:
            small_ref[...] = jnp.dot(hb, ws_ref[...], preferred_element_type=F32)

    acc = jnp.dot(h_ref[...], w_ref[...], preferred_element_type=F32)
    is_norm = jnp.logical_and(j >= norm_lo, j < norm_hi)

    @pl.when(is_norm)
    def _():
        for c in range(tn // LANES):
            sl = slice(c * LANES, (c + 1) * LANES)
            out_ref[:, sl] = _rms(acc[:, sl], cs_ref[:, sl]).astype(out_ref.dtype)

    @pl.when(jnp.logical_not(is_norm))
    def _():
        out_ref[...] = acc.astype(out_ref.dtype)


def _norm_matmul(x, g, w, colscale, w_small, *, tm, tn, norm_lo, norm_hi):
    m, d = x.shape
    n = w.shape[1]
    has_small = w_small is not None
    in_specs = [
        pl.BlockSpec((tm, d), lambda i, j: (i, 0)),
        pl.BlockSpec((1, d), lambda i, j: (0, 0)),
        pl.BlockSpec((d, tn), lambda i, j: (0, j)),
        pl.BlockSpec((1, tn), lambda i, j: (0, j)),
    ]
    args = [x, g, w, colscale]
    out_shape = [jax.ShapeDtypeStruct((m, n), BF16)]
    out_specs = [pl.BlockSpec((tm, tn), lambda i, j: (i, j))]
    if has_small:
        in_specs.append(pl.BlockSpec((d, LANES), lambda i, j: (0, 0)))
        args.append(w_small)
        out_shape.append(jax.ShapeDtypeStruct((m, LANES), F32))
        out_specs.append(pl.BlockSpec((tm, LANES), lambda i, j: (i, 0)))
    res = pl.pallas_call(
        functools.partial(_norm_matmul_kernel, norm_lo=norm_lo, norm_hi=norm_hi, tn=tn,
                          has_small=has_small),
        grid=(m // tm, n // tn),
        in_specs=in_specs,
        out_specs=out_specs,
        out_shape=out_shape,
        scratch_shapes=[pltpu.VMEM((tm, d), BF16)],
        compiler_params=_params(("arbitrary", "arbitrary")),
        name="norm_matmul",
    )(*args)
    return res if has_small else res[0]


def _prefix_sum_rows(tri_bf, x):
    p1 = x.astype(BF16)
    r1 = x - p1.astype(F32)
    p2 = r1.astype(BF16)
    p3 = (r1 - p2.astype(F32)).astype(BF16)
    dot = lambda p: jnp.dot(tri_bf, p, preferred_element_type=F32)
    return (dot(p1) + dot(p2)) + dot(p3)


def _gla_kernel(q_ref, k_ref, v_ref, r_ref, sm_ref, au_ref, ab_ref, gn_ref, out_ref, st_ref,
                *, chunk, nsub, scale, hps, dk, dv):
    @pl.when(pl.program_id(2) == 0)
    def _():
        st_ref[...] = jnp.zeros_like(st_ref)

    row = lax.broadcasted_iota(jnp.int32, (chunk, chunk), 0)
    col = lax.broadcasted_iota(jnp.int32, (chunk, chunk), 1)
    causal = row >= col
    tri = causal.astype(BF16)
    key_row = lax.broadcasted_iota(jnp.int32, (chunk, 1), 0)
    for s in range(nsub):
        rows = pl.ds(s * chunk, chunk)
        ga = sm_ref[rows, :].astype(BF16)
        for hh in range(hps):
            kc = slice(hh * dk, (hh + 1) * dk)
            vc = slice(hh * dv, (hh + 1) * dv)
            z = jnp.dot(ga, au_ref[:, kc], preferred_element_type=F32)
            la = _log_sigmoid(z + ab_ref[:, kc]) * (1.0 / GLA_GATE_NORM)
            b = _prefix_sum_rows(tri, la)
            b_last = b[chunk - 1:chunk, :]
            q = q_ref[rows, kc].astype(F32)
            k = k_ref[rows, kc].astype(F32)
            v = v_ref[rows, vc]
            qd = (q * (scale * jnp.exp(b))).astype(BF16)
            kl = (k * jnp.exp(b_last - b)).astype(BF16)
            blocks = []
            for gi in range(chunk // GLA_SUB):
                grp = slice(gi * GLA_SUB, (gi + 1) * GLA_SUB)
                if gi == 0:
                    q_fac, k_arg = b[grp], -b
                else:
                    r = b[gi * GLA_SUB - 1:gi * GLA_SUB, :]
                    q_fac, k_arg = b[grp] - r, r - b
                k_arg = jnp.where(key_row < (gi + 1) * GLA_SUB, k_arg, -jnp.inf)
                qg = (q[grp] * (scale * jnp.exp(q_fac))).astype(BF16)
                kg = (k * jnp.exp(k_arg)).astype(BF16)
                blocks.append(lax.dot_general(qg, kg, NT_DIMS, preferred_element_type=F32))
            sc = jnp.where(causal, jnp.concatenate(blocks, axis=0), 0.0).astype(BF16)
            st = st_ref[hh]
            o = jnp.dot(sc, v, preferred_element_type=F32) + lax.dot_general(
                qd, st.astype(BF16), NT_DIMS, preferred_element_type=F32)
            st_ref[hh] = st * jnp.exp(b_last) + lax.dot_general(
                v, kl, TN_DIMS, preferred_element_type=F32)
            r = r_ref[rows, vc].astype(F32)
            out_ref[rows, vc] = (_rms(o, gn_ref[...]) * (r * jax.nn.sigmoid(r))).astype(out_ref.dtype)


def _gla(proj, small, alpha_up_pad, alpha_bias, out_norm_g, *, batch, seq, dk, dv, offs, rows, hps):
    t = batch * seq
    nblk = seq // rows
    h = GLA_HEADS
    wk, wv = hps * dk, hps * dv
    qo, ko, vo, ro = (offs[0] // wk, offs[1] // wk, offs[2] // wv, offs[3] // wv)
    rmap = lambda b, hh, c: b * nblk + c
    return pl.pallas_call(
        functools.partial(_gla_kernel, chunk=GLA_CHUNK, nsub=rows // GLA_CHUNK, scale=dk ** -0.5,
                          hps=hps, dk=dk, dv=dv),
        grid=(batch, h // hps, nblk),
        in_specs=[
            pl.BlockSpec((rows, wk), lambda b, hh, c: (rmap(b, hh, c), qo + hh)),
            pl.BlockSpec((rows, wk), lambda b, hh, c: (rmap(b, hh, c), ko + hh)),
            pl.BlockSpec((rows, wv), lambda b, hh, c: (rmap(b, hh, c), vo + hh)),
            pl.BlockSpec((rows, wv), lambda b, hh, c: (rmap(b, hh, c), ro + hh)),
            pl.BlockSpec((rows, LANES), lambda b, hh, c: (rmap(b, hh, c), 0)),
            pl.BlockSpec((LANES, wk), lambda b, hh, c: (0, hh)),
            pl.BlockSpec((1, wk), lambda b, hh, c: (0, hh)),
            pl.BlockSpec((1, dv), lambda b, hh, c: (0, 0)),
        ],
        out_specs=pl.BlockSpec((rows, wv), lambda b, hh, c: (rmap(b, hh, c), hh)),
        out_shape=jax.ShapeDtypeStruct((t, h * dv), BF16),
        scratch_shapes=[pltpu.VMEM((hps, dv, dk), F32)],
        compiler_params=_params(("arbitrary", "arbitrary", "arbitrary")),
        name="gla",
    )(proj, proj, proj, proj, small, alpha_up_pad, alpha_bias, out_norm_g)


def _fcum_kernel(sm_ref, bias_ref, out_ref, carry_ref, *, tb):
    @pl.when(pl.program_id(1) == 0)
    def _():
        carry_ref[...] = jnp.zeros_like(carry_ref)

    row = lax.broadcasted_iota(jnp.int32, (tb, tb), 0)
    col = lax.broadcasted_iota(jnp.int32, (tb, tb), 1)
    tri = (row >= col).astype(BF16)
    lf = _log_sigmoid(sm_ref[...] + bias_ref[...])
    c = _prefix_sum_rows(tri, lf) + carry_ref[...]
    out_ref[...] = c
    carry_ref[...] = c[tb - 1:tb, :]


def _fcum(small, bias_pad, *, batch, seq, tb):
    nb = seq // tb
    return pl.pallas_call(
        functools.partial(_fcum_kernel, tb=tb),
        grid=(batch, nb),
        in_specs=[pl.BlockSpec((tb, LANES), lambda b, i: (b * nb + i, 0)),
                  pl.BlockSpec((1, LANES), lambda b, i: (0, 0))],
        out_specs=pl.BlockSpec((tb, LANES), lambda b, i: (b * nb + i, 0)),
        out_shape=jax.ShapeDtypeStruct(small.shape, F32),
        scratch_shapes=[pltpu.VMEM((1, LANES), F32)],
        compiler_params=_params(("arbitrary", "arbitrary")),
        name="forget_cumsum",
    )(small, bias_pad)


def _fox_kernel(qt_ref, kt_ref, q_ref, k_ref, v_ref, c_ref, out_ref, m_ref, mn_ref, a_ref, l_ref,
                acc_ref, s_ref, p_ref, *, tq, rb, grp):
    p = pl.program_id(2)
    qi = qt_ref[p]
    ki = kt_ref[p]

    @pl.when(ki == 0)
    def _():
        m_ref[...] = jnp.full_like(m_ref, -jnp.inf)
        l_ref[...] = jnp.zeros_like(l_ref)
        acc_ref[...] = jnp.zeros_like(acc_ref)

    def update(diag):
        c = c_ref[0]

        def ncols(g):
            return (g + 1) * grp if diag else tq

        def qk(g):
            n = ncols(g)
            s_ref[g % 2, :, 0:n] = lax.dot_general(q_ref[g * grp:(g + 1) * grp, :], k_ref[0:n, :],
                                                   NT_DIMS, preferred_element_type=F32)

        def pv(g):
            n = ncols(g)
            rows = slice(g * grp, (g + 1) * grp)
            acc_ref[rows, :] += jnp.dot(p_ref[g % 2, :, 0:n], v_ref[0:n, :],
                                        preferred_element_type=F32)

        def softmax(g):
            buf = g % 2
            strips = []
            for st in range(grp // rb):
                r0 = g * grp + st * rb
                visible = (r0 + rb - 1) // LANES + 1 if diag else tq // LANES
                strips.append((slice(st * rb, (st + 1) * rb), slice(r0, r0 + rb), r0, visible))

            def logits(rows_l, r0, t):
                cols = slice(t * LANES, (t + 1) * LANES)
                s = s_ref[buf, rows_l, cols] - c[:, cols]
                if diag and (t + 1) * LANES - 1 > r0:
                    row = r0 + lax.broadcasted_iota(jnp.int32, (rb, LANES), 0)
                    col = t * LANES + lax.broadcasted_iota(jnp.int32, (rb, LANES), 1)
                    s = jnp.where(row >= col, s, -jnp.inf)
                return s

            for rows_l, rows, r0, visible in strips:
                part = logits(rows_l, r0, 0)
                for t in range(1, visible):
                    part = jnp.maximum(part, logits(rows_l, r0, t))
                m_old = m_ref[rows, :]
                m_new = jnp.maximum(m_old, jnp.max(part, axis=-1, keepdims=True))
                a_ref[rows, :] = jnp.exp2(m_old - m_new)
                mn_ref[rows, :] = m_new
            for rows_l, rows, r0, visible in strips:
                m_new = mn_ref[rows, :]
                m_ref[rows, :] = m_new
                part = jnp.zeros((rb, LANES), F32)
                for t in range(ncols(g) // LANES):
                    cols = slice(t * LANES, (t + 1) * LANES)
                    if t < visible:
                        pr = jnp.exp2(logits(rows_l, r0, t) - m_new)
                        part = part + pr
                        p_ref[buf, rows_l, cols] = pr.astype(BF16)
                    else:
                        p_ref[buf, rows_l, cols] = jnp.zeros((rb, LANES), BF16)
                alpha = a_ref[rows, :]
                l_ref[rows, :] = alpha * l_ref[rows, :] + part
                acc_ref[rows, :] = alpha * acc_ref[rows, :]

        ngrp = tq // grp
        qk(0)
        for g in range(ngrp):
            if g + 1 < ngrp:
                qk(g + 1)
            softmax(g)
            if g >= 1:
                pv(g - 1)
        pv(ngrp - 1)

    @pl.when(ki < qi)
    def _():
        update(False)

    @pl.when(ki == qi)
    def _():
        update(True)
        denom = jnp.sum(l_ref[...], axis=-1, keepdims=True)
        out_ref[...] = (acc_ref[...] / denom).astype(out_ref.dtype)


def _fox(proj, c_row, *, batch, seq, dh, offs, tq):
    t = batch * seq
    nq = seq // tq
    pairs = [(a, b) for a in range(nq) for b in range(a + 1)]
    qt = jnp.asarray([a for a, _ in pairs], jnp.int32)
    kt = jnp.asarray([b for _, b in pairs], jnp.int32)
    qo, ko, vo = (o // dh for o in offs)
    grp = min(tq, FOX_GROUP)
    grid_spec = pltpu.PrefetchScalarGridSpec(
        num_scalar_prefetch=2,
        grid=(batch, FOX_HEADS, len(pairs)),
        in_specs=[
            pl.BlockSpec((tq, dh), lambda b, h, p, qt, kt: (b * nq + qt[p], qo + h)),
            pl.BlockSpec((tq, dh), lambda b, h, p, qt, kt: (b * nq + kt[p], ko + h)),
            pl.BlockSpec((tq, dh), lambda b, h, p, qt, kt: (b * nq + kt[p], vo + h)),
            pl.BlockSpec((1, 1, tq), lambda b, h, p, qt, kt: (b * FOX_HEADS + h, 0, kt[p])),
        ],
        out_specs=pl.BlockSpec((tq, dh), lambda b, h, p, qt, kt: (b * nq + qt[p], h)),
        scratch_shapes=[pltpu.VMEM((tq, LANES), F32)] * 4 + [
                        pltpu.VMEM((tq, dh), F32), pltpu.VMEM((2, grp, tq), F32),
                        pltpu.VMEM((2, grp, tq), BF16)],
    )
    return pl.pallas_call(
        functools.partial(_fox_kernel, tq=tq, rb=FOX_STRIP, grp=grp),
        grid_spec=grid_spec,
        out_shape=jax.ShapeDtypeStruct((t, FOX_HEADS * dh), BF16),
        compiler_params=_params(("arbitrary", "arbitrary", "arbitrary")),
        name="fox_attention",
    )(qt, kt, proj, proj, proj, c_row)


def _merge_kernel(a_ref, f_ref, ga_ref, gf_ref, wa_ref, wf_ref, out_ref):
    ya = jnp.dot(a_ref[...], wa_ref[...], preferred_element_type=F32)
    yf = jnp.dot(f_ref[...], wf_ref[...], preferred_element_type=F32)
    out_ref[...] = (jax.nn.sigmoid(ga_ref[...].astype(F32)) * ya
                    + jax.nn.sigmoid(gf_ref[...].astype(F32)) * yf).astype(out_ref.dtype)


def _merge(o_gla, o_fox, proj, w_gla, w_fox, *, offs, tm, tn):
    m, kdim = o_gla.shape
    n = w_gla.shape[1]
    go, fo = offs[0] // tn, offs[1] // tn
    return pl.pallas_call(
        _merge_kernel,
        grid=(m // tm, n // tn),
        in_specs=[
            pl.BlockSpec((tm, kdim), lambda i, j: (i, 0)),
            pl.BlockSpec((tm, o_fox.shape[1]), lambda i, j: (i, 0)),
            pl.BlockSpec((tm, tn), lambda i, j: (i, go + j)),
            pl.BlockSpec((tm, tn), lambda i, j: (i, fo + j)),
            pl.BlockSpec((kdim, tn), lambda i, j: (0, j)),
            pl.BlockSpec((o_fox.shape[1], tn), lambda i, j: (0, j)),
        ],
        out_specs=pl.BlockSpec((tm, tn), lambda i, j: (i, j)),
        out_shape=jax.ShapeDtypeStruct((m, n), BF16),
        compiler_params=_params(("arbitrary", "arbitrary")),
        name="gated_merge",
    )(o_gla, o_fox, proj, proj, w_gla, w_fox)


def _resid_matmul_kernel(a_ref, w_ref, res_ref, out_ref):
    out_ref[...] = res_ref[...] + jnp.dot(a_ref[...], w_ref[...], preferred_element_type=F32)


def _resid_matmul(a, w, res, *, tm, tn):
    m, kdim = a.shape
    n = w.shape[1]
    return pl.pallas_call(
        _resid_matmul_kernel,
        grid=(m // tm, n // tn),
        in_specs=[pl.BlockSpec((tm, kdim), lambda i, j: (i, 0)),
                  pl.BlockSpec((kdim, tn), lambda i, j: (0, j)),
                  pl.BlockSpec((tm, tn), lambda i, j: (i, j))],
        out_specs=pl.BlockSpec((tm, tn), lambda i, j: (i, j)),
        out_shape=jax.ShapeDtypeStruct((m, n), F32),
        compiler_params=_params(("arbitrary", "arbitrary")),
        name="resid_matmul",
    )(a, w, res)


def _memory_kernel(x_ref, gn_ref, wq_ref, gq_ref, k_ref, v_ref, wo_ref, gm_ref, wr_ref, br_ref,
                   x2_ref, rl_ref, xsp_ref, *, dh):
    x1 = x_ref[...]
    hb = _rms(x1, gn_ref[...]).astype(BF16)
    q = jnp.dot(hb, wq_ref[...], preferred_element_type=F32)
    outs = []
    for h in range(MEM_HEADS):
        sl = slice(h * dh, (h + 1) * dh)
        qn = _rms(q[:, sl], gq_ref[...]).astype(BF16)
        s = lax.dot_general(qn, k_ref[:, sl], NT_DIMS, preferred_element_type=F32)
        s = s - jnp.max(s, axis=-1, keepdims=True)
        p = jnp.exp(s)
        p = p / jnp.sum(p, axis=-1, keepdims=True)
        outs.append(jnp.dot(p.astype(BF16), v_ref[:, sl], preferred_element_type=F32).astype(BF16))
    o = jnp.concatenate(outs, axis=-1)
    x2 = x1 + jnp.dot(o, wo_ref[...], preferred_element_type=F32)
    x2_ref[...] = x2
    xs = _rms(x2, gm_ref[...])
    rl_ref[...] = jnp.dot(xs.astype(BF16), wr_ref[...], preferred_element_type=F32) + br_ref[...]
    xsp_ref[...] = _pack_bf16_pairs(xs)


def _memory(x1, mem_norm_g, wq, gq, kv, wo, moe_norm_g, wr_pad, br_pad, *, batch, seq,
            mem_len, tm):
    t, d = x1.shape
    w = wq.shape[1]
    dh = w // MEM_HEADS
    nblk = seq // tm
    const = lambda i: (0, 0)
    return pl.pallas_call(
        functools.partial(_memory_kernel, dh=dh),
        grid=(t // tm,),
        in_specs=[
            pl.BlockSpec((tm, d), lambda i: (i, 0)),
            pl.BlockSpec((1, d), const),
            pl.BlockSpec((d, w), const),
            pl.BlockSpec((1, dh), const),
            pl.BlockSpec((mem_len, w), lambda i: (i // nblk, 0)),
            pl.BlockSpec((mem_len, w), lambda i: (i // nblk, 1)),
            pl.BlockSpec((w, d), const),
            pl.BlockSpec((1, d), const),
            pl.BlockSpec((d, LANES), const),
            pl.BlockSpec((1, LANES), const),
        ],
        out_specs=[pl.BlockSpec((tm, d), lambda i: (i, 0)),
                   pl.BlockSpec((tm, LANES), lambda i: (i, 0)),
                   pl.BlockSpec((tm, d // 2), lambda i: (i, 0))],
        out_shape=[jax.ShapeDtypeStruct((t, d), F32), jax.ShapeDtypeStruct((t, LANES), F32),
                   jax.ShapeDtypeStruct((t, d // 2), jnp.uint32)],
        compiler_params=_params(("arbitrary",)),
        name="memory_block",
    )(x1, mem_norm_g, wq, gq, kv, kv, wo, moe_norm_g, wr_pad, br_pad)


def _route_kernel(rl_ref, route_ref, counts_ref, carry_ref, *, tr, n_experts):
    @pl.when(pl.program_id(0) == 0)
    def _():
        carry_ref[...] = jnp.zeros_like(carry_ref)

    lane = lax.broadcasted_iota(jnp.int32, (tr, LANES), 1)
    lg = jnp.where(lane < n_experts, rl_ref[...], -jnp.inf)
    vals, hots = [], []
    for _ in range(TOP_K):
        mx = jnp.max(lg, axis=-1, keepdims=True)
        idx = jnp.min(jnp.where(lg == mx, lane, LANES), axis=-1, keepdims=True)
        hot = lane == idx
        vals.append(mx)
        hots.append(hot)
        lg = jnp.where(hot, -jnp.inf, lg)
    exps = [jnp.exp(v - vals[0]) for v in vals]
    denom = exps[0]
    for e in exps[1:]:
        denom = denom + e
    onehot = hots[0]
    for hsel in hots[1:]:
        onehot = jnp.logical_or(onehot, hsel)
    onehot_f = onehot.astype(F32)
    row = lax.broadcasted_iota(jnp.int32, (tr, tr), 0)
    col = lax.broadcasted_iota(jnp.int32, (tr, tr), 1)
    strict = (row > col).astype(BF16)
    before = jnp.dot(strict, onehot_f.astype(BF16), preferred_element_type=F32) + carry_ref[...]
    route = jnp.zeros((tr, LANES), F32)
    for k in range(TOP_K):
        idx_f = jnp.sum(jnp.where(hots[k], lane, 0), axis=-1, keepdims=True).astype(F32)
        rank = jnp.sum(jnp.where(hots[k], before, 0.0), axis=-1, keepdims=True)
        route = jnp.where(lane == k, idx_f, route)
        route = jnp.where(lane == TOP_K + k, exps[k] / denom, route)
        route = jnp.where(lane == 2 * TOP_K + k, rank, route)
    route_ref[...] = route
    total = carry_ref[...] + jnp.sum(onehot_f, axis=0, keepdims=True)
    carry_ref[...] = total
    counts_ref[...] = jnp.broadcast_to(total, counts_ref.shape)


def _route(rlogits, *, n_experts, tr):
    t = rlogits.shape[0]
    return pl.pallas_call(
        functools.partial(_route_kernel, tr=tr, n_experts=n_experts),
        grid=(t // tr,),
        in_specs=[pl.BlockSpec((tr, LANES), lambda i: (i, 0))],
        out_specs=[pl.BlockSpec((tr, LANES), lambda i: (i, 0)),
                   pl.BlockSpec((8, LANES), lambda i: (0, 0))],
        out_shape=[jax.ShapeDtypeStruct((t, LANES), F32), jax.ShapeDtypeStruct((8, LANES), F32)],
        scratch_shapes=[pltpu.VMEM((1, LANES), F32)],
        compiler_params=_params(("arbitrary",)),
        name="route_topk",
    )(rlogits)


def _gather_kernel(tok_ref, nused_ref, x_hbm, out_ref, buf, sem, *, rows):
    i = pl.program_id(0)
    nused = nused_ref[0]

    def issue(blk, slot):
        def body(r8, carry):
            for j in range(SUBLANES):
                tok = tok_ref[blk * rows + r8 * SUBLANES + j]
                pltpu.make_async_copy(
                    x_hbm.at[tok >> 3, pl.ds(tok & (SUBLANES - 1), 1)],
                    buf.at[slot, r8, pl.ds(j, 1)], sem.at[slot]).start()
            return carry
        lax.fori_loop(0, rows // SUBLANES, body, 0)

    @pl.when(i == 0)
    def _():
        issue(0, 0)

    @pl.when(i + 1 < nused)
    def _():
        issue(i + 1, (i + 1) % 2)

    @pl.when(i < nused)
    def _():
        slot = i % 2
        pltpu.make_async_copy(x_hbm.at[pl.ds(0, rows // SUBLANES)], buf.at[slot],
                              sem.at[slot]).wait()
        lo, hi = _unpack_bf16_pairs(buf[slot].reshape(rows, -1))
        half = lo.shape[1]
        out_ref[:, :half] = lo.astype(out_ref.dtype)
        out_ref[:, half:] = hi.astype(out_ref.dtype)

    @pl.when(i >= nused)
    def _():
        out_ref[...] = jnp.zeros_like(out_ref)


def _gather_rows(slot_tok, nused, xs_packed, *, rows):
    p = slot_tok.shape[0]
    half = xs_packed.shape[1]
    grid_spec = pltpu.PrefetchScalarGridSpec(
        num_scalar_prefetch=2,
        grid=(p // rows,),
        in_specs=[pl.BlockSpec(memory_space=pl.ANY)],
        out_specs=pl.BlockSpec((rows, 2 * half), lambda i, tok, nu: (i, 0)),
        scratch_shapes=[pltpu.VMEM((2, rows // SUBLANES, SUBLANES, half), jnp.uint32),
                        pltpu.SemaphoreType.DMA((2,))],
    )
    return pl.pallas_call(
        functools.partial(_gather_kernel, rows=rows),
        grid_spec=grid_spec,
        out_shape=jax.ShapeDtypeStruct((p, 2 * half), BF16),
        compiler_params=_params(("arbitrary",)),
        name="moe_gather",
    )(slot_tok, nused, xs_packed.reshape(-1, SUBLANES, half))


def _start_all(copies):
    for c in copies:
        c.start()


def _wait_all(copies):
    for c in copies:
        c.wait()


def _weight_pipeline(step, flag, tabs, copies_for, convert):
    exp_ref, tile_ref, gslot_ref, nexp_ref, ntile_ref = tabs

    @pl.when((flag & 2) != 0)
    def _():
        slot = gslot_ref[step]
        mine = copies_for(exp_ref[step], tile_ref[step], slot)

        @pl.when(step == 0)
        def _():
            _start_all(mine)

        _wait_all(mine)
        convert(slot)

        @pl.when((flag & 4) != 0)
        def _():
            _start_all(copies_for(nexp_ref[step], ntile_ref[step], 1 - slot))


def _expert_up_kernel(blk_ref, tile_ref, exp_ref, flag_ref, gslot_ref, nexp_ref, ntile_ref,
                      x_ref, w_hbm, bg_ref, bl_ref, out_ref, wbuf, wg_bf, wl_bf, sem, *, tf, nt):
    step = pl.program_id(0)
    flag = flag_ref[step]

    def copies_for(e, tile, slot):
        return [pltpu.make_async_copy(
            w_hbm.at[e, :, pl.ds(pl.multiple_of((half * nt + tile) * tf, tf), tf)],
            wbuf.at[slot, half], sem.at[slot, half]) for half in range(2)]

    def convert(slot):
        wg_bf[...] = wbuf[slot, 0].astype(BF16)
        wl_bf[...] = wbuf[slot, 1].astype(BF16)

    _weight_pipeline(step, flag, (exp_ref, tile_ref, gslot_ref, nexp_ref, ntile_ref), copies_for,
                     convert)

    @pl.when((flag & 1) != 0)
    def _():
        x = x_ref[...]
        glu = jnp.dot(x, wg_bf[...], preferred_element_type=F32) + bg_ref[...]
        lin = jnp.dot(x, wl_bf[...], preferred_element_type=F32) + bl_ref[...]
        glu = jnp.minimum(glu, SWIGLU_LIMIT)
        lin = jnp.clip(lin, -SWIGLU_LIMIT, SWIGLU_LIMIT)
        out_ref[...] = (glu * jax.nn.sigmoid(SWIGLU_ALPHA * glu) * (lin + 1.0)).astype(out_ref.dtype)

    @pl.when((flag & 1) == 0)
    def _():
        out_ref[...] = jnp.zeros_like(out_ref)


def _expert_up(tables, xg, w_up, b_up, *, tm, tf):
    p, d = xg.shape
    n_e, _, ff2 = w_up.shape
    ff = ff2 // 2
    nt = ff // tf
    nsteps = tables[0].shape[0]
    bias = lambda half: pl.BlockSpec((None, 1, tf),
                                     lambda s, blk, tile, ex, *_: (ex[s], 0, half * nt + tile[s]))
    grid_spec = pltpu.PrefetchScalarGridSpec(
        num_scalar_prefetch=len(tables),
        grid=(nsteps,),
        in_specs=[
            pl.BlockSpec((tm, d), lambda s, blk, *_: (blk[s], 0)),
            pl.BlockSpec(memory_space=pl.ANY),
            bias(0), bias(1),
        ],
        out_specs=pl.BlockSpec((tm, tf), lambda s, blk, tile, *_: (blk[s], tile[s])),
        scratch_shapes=[pltpu.VMEM((2, 2, d, tf), F32), pltpu.VMEM((d, tf), BF16),
                        pltpu.VMEM((d, tf), BF16), pltpu.SemaphoreType.DMA((2, 2))],
    )
    return pl.pallas_call(
        functools.partial(_expert_up_kernel, tf=tf, nt=nt),
        grid_spec=grid_spec,
        out_shape=jax.ShapeDtypeStruct((p, ff), BF16),
        compiler_params=_params(("arbitrary",), VMEM_LIMIT_MAX),
        name="expert_up",
    )(*tables, xg, w_up, b_up, b_up)


def _expert_down_kernel(blk_ref, tile_ref, exp_ref, flag_ref, gslot_ref, nexp_ref, ntile_ref,
                        a_ref, w_hbm, b_ref, out_ref, wbuf, w_bf, sem, *, tn):
    step = pl.program_id(0)
    flag = flag_ref[step]

    def copies_for(e, tile, slot):
        return [pltpu.make_async_copy(w_hbm.at[e, :, pl.ds(pl.multiple_of(tile * tn, tn), tn)],
                                      wbuf.at[slot], sem.at[slot])]

    def convert(slot):
        w_bf[...] = wbuf[slot].astype(BF16)

    _weight_pipeline(step, flag, (exp_ref, tile_ref, gslot_ref, nexp_ref, ntile_ref), copies_for,
                     convert)

    @pl.when((flag & 1) != 0)
    def _():
        y = jnp.dot(a_ref[...], w_bf[...], preferred_element_type=F32) + b_ref[...]
        out_ref[...] = _pack_bf16_pairs(y)

    @pl.when((flag & 1) == 0)
    def _():
        out_ref[...] = jnp.zeros_like(out_ref)


def _expert_down(tables, act, w_down, b_down, *, tm, tn):
    p, ff = act.shape
    d = w_down.shape[2]
    nsteps = tables[0].shape[0]
    grid_spec = pltpu.PrefetchScalarGridSpec(
        num_scalar_prefetch=len(tables),
        grid=(nsteps,),
        in_specs=[
            pl.BlockSpec((tm, ff), lambda s, blk, *_: (blk[s], 0)),
            pl.BlockSpec(memory_space=pl.ANY),
            pl.BlockSpec((None, 1, tn), lambda s, blk, tile, ex, *_: (ex[s], 0, tile[s])),
        ],
        out_specs=pl.BlockSpec((tm, tn // 2), lambda s, blk, tile, *_: (blk[s], tile[s])),
        scratch_shapes=[pltpu.VMEM((2, ff, tn), F32), pltpu.VMEM((ff, tn), BF16),
                        pltpu.SemaphoreType.DMA((2,))],
    )
    return pl.pallas_call(
        functools.partial(_expert_down_kernel, tn=tn),
        grid_spec=grid_spec,
        out_shape=jax.ShapeDtypeStruct((p, d // 2), jnp.uint32),
        compiler_params=_params(("arbitrary",)),
        name="expert_down",
    )(*tables, act, w_down, b_down)


def _expert_schedule(counts, *, tm, n_tiles, n_blocks):
    n_e = counts.shape[0]
    nb = (counts + tm - 1) // tm
    bend = jnp.cumsum(nb)
    bstart = bend - nb
    steps_e = nb * n_tiles
    cs = jnp.cumsum(steps_e)
    total = cs[-1]
    s = jnp.arange(n_blocks * n_tiles, dtype=jnp.int32)
    sc = jnp.minimum(s, total - 1)
    e = jnp.sum((cs[None, :] <= sc[:, None]).astype(jnp.int32), axis=1)
    e = jnp.minimum(e, n_e - 1)
    r = sc - (cs[e] - steps_e[e])
    nbe = jnp.maximum(nb[e], 1)
    valid = s < total
    n_unused = jnp.maximum(n_blocks - bend[-1], 1)
    u = jnp.maximum(s - total, 0)
    live_tile = r // nbe
    tile = jnp.where(valid, live_tile, u // n_unused)
    blk = jnp.where(valid, bstart[e] + r % nbe, bend[-1] + u % n_unused)
    first = jnp.logical_and(valid, r % nbe == 0)
    ids = jnp.where(nb > 0, jnp.arange(n_e, dtype=jnp.int32), n_e)
    later = jnp.concatenate([lax.cummin(ids, reverse=True)[1:], jnp.full((1,), n_e, jnp.int32)])
    wraps = live_tile + 1 >= n_tiles
    nexp = jnp.where(wraps, later[e], e)
    ntile = jnp.where(wraps, 0, live_tile + 1)
    flag = (valid.astype(jnp.int32) + 2 * first.astype(jnp.int32)
            + 4 * jnp.logical_and(first, nexp < n_e).astype(jnp.int32))
    gslot = (jnp.cumsum(first.astype(jnp.int32)) - 1) % 2
    i32 = lambda a: a.astype(jnp.int32)
    return (i32(blk), i32(tile), i32(e), flag, i32(gslot), i32(jnp.minimum(nexp, n_e - 1)),
            i32(ntile))


def _combine_kernel(slot_ref, x_ref, route_ref, y_hbm, out_ref, buf, sem, *, tc, tn):
    i = pl.program_id(0)
    n = pl.num_programs(0)

    def issue(blk, bslot):
        def body(r, carry):
            for k in range(TOP_K):
                src = slot_ref[(blk * tc + r) * TOP_K + k]
                pltpu.make_async_copy(y_hbm.at[pl.ds(src, 1)], buf.at[bslot, k, pl.ds(r, 1)],
                                      sem.at[bslot]).start()
            return carry
        lax.fori_loop(0, tc, body, 0, unroll=4)

    @pl.when(i == 0)
    def _():
        issue(0, 0)

    @pl.when(i + 1 < n)
    def _():
        issue(i + 1, (i + 1) % 2)

    bslot = i % 2
    for k in range(TOP_K):
        pltpu.make_async_copy(y_hbm.at[pl.ds(0, tc)], buf.at[bslot, k], sem.at[bslot]).wait()
    gates = [route_ref[:, TOP_K + k:TOP_K + k + 1] for k in range(TOP_K)]
    hw = tn // 2
    for j in range(out_ref.shape[1] // tn):
        parts = [_unpack_bf16_pairs(buf[bslot, k, :, j * hw:(j + 1) * hw]) for k in range(TOP_K)]
        for half in range(2):
            cols = slice(j * tn + half * hw, j * tn + (half + 1) * hw)
            acc = x_ref[:, cols]
            for k in range(TOP_K):
                acc = acc + gates[k] * parts[k][half]
            out_ref[:, cols] = acc


def _combine(slot_flat, x2, route, y, *, tc, tn):
    t, d = x2.shape
    grid_spec = pltpu.PrefetchScalarGridSpec(
        num_scalar_prefetch=1,
        grid=(t // tc,),
        in_specs=[pl.BlockSpec((tc, d), lambda i, sl: (i, 0)),
                  pl.BlockSpec((tc, LANES), lambda i, sl: (i, 0)),
                  pl.BlockSpec(memory_space=pl.ANY)],
        out_specs=pl.BlockSpec((tc, d), lambda i, sl: (i, 0)),
        scratch_shapes=[pltpu.VMEM((2, TOP_K, tc, d // 2), jnp.uint32),
                        pltpu.SemaphoreType.DMA((2,))],
    )
    return pl.pallas_call(
        functools.partial(_combine_kernel, tc=tc, tn=tn),
        grid_spec=grid_spec,
        out_shape=jax.ShapeDtypeStruct((t, d), F32),
        compiler_params=_params(("arbitrary",)),
        name="moe_combine",
    )(slot_flat, x2, route, y)


def _pad_lanes(a, offset=0):
    width = a.shape[-1]
    return jnp.pad(a, [(0, 0)] * (a.ndim - 1) + [(offset, LANES - offset - width)])


def _layer(x2d, mem2d, lw, *, batch, seq, mem_len):
    t, d = x2d.shape
    gla_qk = lw["gla_alpha_up"].shape[1]
    gla_rank = lw["gla_alpha_up"].shape[0]
    gla_v = lw["w_branch_gla"].shape[0]
    fox_w = lw["w_branch_fox"].shape[0]
    dk, dv = gla_qk // GLA_HEADS, gla_v // GLA_HEADS
    dh = fox_w // FOX_HEADS
    mem_w = lw["mem_w_q"].shape[1]
    mem_dh = mem_w // MEM_HEADS
    n_experts = lw["w_router"].shape[1]
    ff = lw["w_expert_down"].shape[1]

    sizes = (gla_qk, gla_qk, gla_v, gla_v, gla_rank, fox_w, fox_w, fox_w, FOX_HEADS, d, d)
    starts = [0]
    for sz in sizes:
        starts.append(starts[-1] + sz)
    big_ids = (0, 1, 2, 3, 5, 6, 7, 9, 10)
    off = {}
    acc = 0
    for i in big_ids:
        off[i] = acc
        acc += sizes[i]
    n_big = acc
    w_big, w_small = _w_in_pack(
        lw["w_in"], spans=tuple((starts[i], sizes[i], off[i]) for i in big_ids),
        pieces=((starts[4], sizes[4], 0), (starts[8], sizes[8], sizes[4])), tr=TILE["w_pack_rows"])
    tn1 = TILE["proj_tn"]
    colscale = jnp.ones((1, n_big), F32)
    colscale = colscale.at[:, off[5]:off[5] + fox_w].set(
        jnp.tile(lw["fox_q_norm_g"] * (dh ** -0.5 * LOG2E), FOX_HEADS)[None])
    colscale = colscale.at[:, off[6]:off[6] + fox_w].set(jnp.tile(lw["fox_k_norm_g"], FOX_HEADS)[None])
    proj, small = _norm_matmul(
        x2d, lw["mix_norm_g"][None], w_big, colscale, w_small,
        tm=TILE["proj_tm"], tn=tn1, norm_lo=off[5] // tn1, norm_hi=(off[6] + fox_w) // tn1)

    alpha_up_pad = jnp.pad(lw["gla_alpha_up"], ((0, LANES - gla_rank), (0, 0))).astype(BF16)
    o_gla = _gla(proj, small, alpha_up_pad, lw["gla_alpha_bias"][None], lw["gla_out_norm_g"][None],
                 batch=batch, seq=seq, dk=dk, dv=dv, offs=(off[0], off[1], off[2], off[3]),
                 rows=TILE["gla_rows"], hps=TILE["gla_heads_per_step"])

    c = _fcum(small, _pad_lanes(lw["fox_f_bias"][None], gla_rank), batch=batch, seq=seq,
              tb=TILE["fcum_rows"])
    c_row = c[:, gla_rank:gla_rank + FOX_HEADS].reshape(batch, seq, FOX_HEADS)
    c_row = c_row.transpose(0, 2, 1).reshape(batch * FOX_HEADS, 1, seq) * LOG2E
    o_fox = _fox(proj, c_row, batch=batch, seq=seq, dh=dh, offs=(off[5], off[6], off[7]),
                 tq=min(seq, TILE["fox_tq"]))

    merged = _merge(o_gla, o_fox, proj, lw["w_branch_gla"].astype(BF16),
                    lw["w_branch_fox"].astype(BF16), offs=(off[9], off[10]),
                    tm=TILE["merge_tm"], tn=TILE["merge_tn"])
    x1 = _resid_matmul(merged, lw["w_out"].astype(BF16), x2d, tm=TILE["resid_tm"],
                       tn=TILE["resid_tn"])

    kv_scale = jnp.concatenate([jnp.tile(lw["mem_k_norm_g"], MEM_HEADS),
                                jnp.ones((mem_w,), F32)])[None]
    kv = _norm_matmul(mem2d, lw["mem_kv_norm_g"][None], lw["mem_w_kv"].astype(BF16), kv_scale, None,
                      tm=mem2d.shape[0], tn=mem_w, norm_lo=0, norm_hi=1)
    x2, rlogits, xs_packed = _memory(
        x1, lw["mem_norm_g"][None], lw["mem_w_q"].astype(BF16),
        (lw["mem_q_norm_g"] * (mem_dh ** -0.5))[None], kv,
        lw["mem_w_o"].astype(BF16), lw["moe_norm_g"][None],
        _pad_lanes(lw["w_router"]).astype(BF16), _pad_lanes(lw["b_router"][None]),
        batch=batch, seq=seq, mem_len=mem_len, tm=TILE["mem_tm"])

    route, counts_f = _route(rlogits, n_experts=n_experts, tr=TILE["route_rows"])
    tm_e = TILE["expert_tm"]
    a_total = t * TOP_K
    n_blocks = a_total // tm_e + n_experts
    p_rows = n_blocks * tm_e
    counts = counts_f[0, :n_experts].astype(jnp.int32)
    padded = (counts + tm_e - 1) // tm_e * tm_e
    pstart = jnp.cumsum(padded) - padded
    idx = route[:, :TOP_K].astype(jnp.int32)
    rank = route[:, 2 * TOP_K:3 * TOP_K].astype(jnp.int32)
    hit = idx[:, :, None] == jnp.arange(n_experts, dtype=jnp.int32)
    slot = (jnp.sum(jnp.where(hit, pstart, 0), axis=-1) + rank).reshape(a_total)
    slot_tok = (jnp.arange(p_rows, dtype=jnp.int32) % t).at[slot].set(
        jnp.arange(a_total, dtype=jnp.int32) // TOP_K, unique_indices=True,
        mode="promise_in_bounds")

    tf = TILE["expert_tf"]
    g_rows = TILE["gather_rows"]
    nused = (jnp.sum(padded) // g_rows).astype(jnp.int32).reshape(1)
    xg = _gather_rows(slot_tok, nused, xs_packed, rows=g_rows)
    up_tables = _expert_schedule(counts, tm=tm_e, n_tiles=ff // tf, n_blocks=n_blocks)
    act = _expert_up(up_tables, xg, lw["w_expert_up"], lw["b_expert_up"][:, None, :], tm=tm_e, tf=tf)
    tn_d = TILE["expert_tn"]
    down_tables = _expert_schedule(counts, tm=tm_e, n_tiles=d // tn_d, n_blocks=n_blocks)
    y = _expert_down(down_tables, act, lw["w_expert_down"], lw["b_expert_down"][:, None, :],
                     tm=tm_e, tn=tn_d)
    return _combine(slot, x2, route, y, tc=TILE["combine_rows"], tn=tn_d)


def kernel(x, mem, mix_norm_g, w_in, gla_alpha_up, gla_alpha_bias, gla_out_norm_g, fox_f_bias, fox_q_norm_g, fox_k_norm_g, w_branch_gla, w_branch_fox, w_out, mem_norm_g, mem_kv_norm_g, mem_w_q, mem_w_kv, mem_q_norm_g, mem_k_norm_g, mem_w_o, moe_norm_g, w_router, b_router, w_expert_up, b_expert_up, w_expert_down, b_expert_down):
    weights = dict(
        mix_norm_g=mix_norm_g, w_in=w_in, gla_alpha_up=gla_alpha_up, gla_alpha_bias=gla_alpha_bias,
        gla_out_norm_g=gla_out_norm_g, fox_f_bias=fox_f_bias, fox_q_norm_g=fox_q_norm_g,
        fox_k_norm_g=fox_k_norm_g, w_branch_gla=w_branch_gla, w_branch_fox=w_branch_fox, w_out=w_out,
        mem_norm_g=mem_norm_g, mem_kv_norm_g=mem_kv_norm_g, mem_w_q=mem_w_q, mem_w_kv=mem_w_kv,
        mem_q_norm_g=mem_q_norm_g, mem_k_norm_g=mem_k_norm_g, mem_w_o=mem_w_o, moe_norm_g=moe_norm_g,
        w_router=w_router, b_router=b_router, w_expert_up=w_expert_up, b_expert_up=b_expert_up,
        w_expert_down=w_expert_down, b_expert_down=b_expert_down)
    batch, seq, d = x.shape
    mem_len = mem.shape[1]
    x2d = x.reshape(batch * seq, d)
    mem2d = mem.reshape(batch * mem_len, d)
    for layer in range(mix_norm_g.shape[0]):
        lw = {name: w[layer] for name, w in weights.items()}
        x2d = _layer(x2d, mem2d, lw, batch=batch, seq=seq, mem_len=mem_len)
    return x2d.reshape(batch, seq, d)
```

```python
import functools

import jax
import jax.numpy as jnp
from jax import lax
from jax.experimental import pallas as pl
from jax.experimental.pallas import tpu as pltpu

F32 = jnp.float32
BF16 = jnp.bfloat16

EPS = 1e-6
LANES = 128
SUBLANES = 8
VMEM_LIMIT = 56 * 1024 * 1024
VMEM_LIMIT_MAX = 61 * 1024 * 1024

GLA_HEADS = 4
GLA_GATE_NORM = 16.0
GLA_CHUNK = 64
GLA_SUB = 16
FOX_HEADS = 16
FOX_STRIP = 16
FOX_GROUP = 128
LOG2E = 1.4426950408889634
MEM_HEADS = 4
TOP_K = 4
SWIGLU_LIMIT = 7.0
SWIGLU_ALPHA = 1.702

TILE = dict(
    w_pack_rows=128,
    proj_tm=1024, proj_tn=1024,
    gla_rows=256, gla_heads_per_step=4,
    fcum_rows=512,
    fox_tq=2048,
    merge_tm=512, merge_tn=1024,
    resid_tm=1024, resid_tn=1024,
    mem_tm=512,
    route_rows=512,
    expert_tm=512, expert_tf=1024, expert_tn=1024,
    gather_rows=512, combine_rows=256,
)

NT_DIMS = (((1,), (1,)), ((), ()))
TN_DIMS = (((0,), (0,)), ((), ()))


def _params(sem, vmem=VMEM_LIMIT):
    return pltpu.CompilerParams(dimension_semantics=sem, vmem_limit_bytes=vmem)


def _rms(xf, g):
    return xf * lax.rsqrt(jnp.mean(xf * xf, axis=-1, keepdims=True) + EPS) * g


def _pack_bf16_pairs(a):
    half = a.shape[1] // 2
    bits = lambda v: lax.bitcast_convert_type(v.astype(BF16).astype(F32), jnp.uint32)
    return (bits(a[:, half:]) & jnp.uint32(0xFFFF0000)) | (bits(a[:, :half]) >> 16)


def _unpack_bf16_pairs(u):
    lo = lax.bitcast_convert_type(u << 16, F32)
    hi = lax.bitcast_convert_type(u & jnp.uint32(0xFFFF0000), F32)
    return lo, hi


def _log_sigmoid(z):
    return jnp.minimum(z, 0.0) - jnp.log1p(jnp.exp(-jnp.abs(z)))


def _w_in_pack_kernel(w_ref, tail_ref, big_ref, small_ref, *, spans, pieces):
    rows, n_cols = w_ref.shape
    lane = lax.broadcasted_iota(jnp.int32, (rows, LANES), 1)
    tile_at = lambda c0: w_ref[:, c0:c0 + LANES]
    for s0, width, d0 in spans:
        sh = s0 % LANES
        a0 = s0 - sh
        for t in range(width // LANES):
            dst = slice(d0 + t * LANES, d0 + (t + 1) * LANES)
            if sh == 0:
                big_ref[:, dst] = tile_at(a0 + t * LANES).astype(BF16)
            elif a0 + (t + 2) * LANES <= n_cols:
                left = pltpu.roll(tile_at(a0 + t * LANES), LANES - sh, 1)
                right = pltpu.roll(tile_at(a0 + (t + 1) * LANES), LANES - sh, 1)
                big_ref[:, dst] = jnp.where(lane < LANES - sh, left, right).astype(BF16)
            else:
                assert s0 + (t + 1) * LANES == n_cols
                big_ref[:, dst] = tail_ref[...].astype(BF16)
    side = jnp.zeros((rows, LANES), F32)
    for s0, width, dl in pieces:
        sh = s0 % LANES
        tile = tile_at(s0 - sh)
        if (dl - sh) % LANES:
            tile = pltpu.roll(tile, (dl - sh) % LANES, 1)
        side = jnp.where(jnp.logical_and(lane >= dl, lane < dl + width), tile, side)
    small_ref[...] = side.astype(BF16)


def _w_in_pack(w_in, *, spans, pieces, tr):
    d, n_cols = w_in.shape
    n_big = sum(width for _, width, _ in spans)
    w_big, w_small = pl.pallas_call(
        functools.partial(_w_in_pack_kernel, spans=spans, pieces=pieces),
        grid=(d // tr,),
        in_specs=[pl.BlockSpec((tr, n_cols), lambda i: (i, 0)),
                  pl.BlockSpec((tr, LANES), lambda i: (i, 0))],
        out_specs=[pl.BlockSpec((tr, n_big), lambda i: (i, 0)),
                   pl.BlockSpec((tr, LANES), lambda i: (i, 0))],
        out_shape=[jax.ShapeDtypeStruct((d, n_big), BF16), jax.ShapeDtypeStruct((d, LANES), BF16)],
        compiler_params=_params(("arbitrary",)),
        name="w_in_pack",
    )(w_in, w_in[:, n_cols - LANES:])
    return w_big, w_small


def _norm_matmul_kernel(*refs, norm_lo, norm_hi, tn, has_small):
    if has_small:
        x_ref, g_ref, w_ref, cs_ref, ws_ref, out_ref, small_ref, h_ref = refs
    else:
        x_ref, g_ref, w_ref, cs_ref, out_ref, h_ref = refs
    j = pl.program_id(1)

    @pl.when(j == 0)
    def _():
        hb = _rms(x_ref[...], g_ref[...]).astype(BF16)
        h_ref[...] = hb
        if has_small:
            small_ref[...] = jnp.dot(hb, ws_ref[...], preferred_element_type=F32)

    acc = jnp.dot(h_ref[...], w_ref[...], preferred_element_type=F32)
    is_norm = jnp.logical_and(j >= norm_lo, j < norm_hi)

    @pl.when(is_norm)
    def _():
        for c in range(tn // LANES):
            sl = slice(c * LANES, (c + 1) * LANES)
            out_ref[:, sl] = _rms(acc[:, sl], cs_ref[:, sl]).astype(out_ref.dtype)

    @pl.when(jnp.logical_not(is_norm))
    def _():
        out_ref[...] = acc.astype(out_ref.dtype)


def _norm_matmul(x, g, w, colscale, w_small, *, tm, tn, norm_lo, norm_hi):
    m, d = x.shape
    n = w.shape[1]
    has_small = w_small is not None
    in_specs = [
        pl.BlockSpec((tm, d), lambda i, j: (i, 0)),
        pl.BlockSpec((1, d), lambda i, j: (0, 0)),
        pl.BlockSpec((d, tn), lambda i, j: (0, j)),
        pl.BlockSpec((1, tn), lambda i, j: (0, j)),
    ]
    args = [x, g, w, colscale]
    out_shape = [jax.ShapeDtypeStruct((m, n), BF16)]
    out_specs = [pl.BlockSpec((tm, tn), lambda i, j: (i, j))]
    if has_small:
        in_specs.append(pl.BlockSpec((d, LANES), lambda i, j: (0, 0)))
        args.append(w_small)
        out_shape.append(jax.ShapeDtypeStruct((m, LANES), F32))
        out_specs.append(pl.BlockSpec((tm, LANES), lambda i, j: (i, 0)))
    res = pl.pallas_call(
        functools.partial(_norm_matmul_kernel, norm_lo=norm_lo, norm_hi=norm_hi, tn=tn,
                          has_small=has_small),
        grid=(m // tm, n // tn),
        in_specs=in_specs,
        out_specs=out_specs,
        out_shape=out_shape,
        scratch_shapes=[pltpu.VMEM((tm, d), BF16)],
        compiler_params=_params(("arbitrary", "arbitrary")),
        name="norm_matmul",
    )(*args)
    return res if has_small else res[0]


def _prefix_sum_rows(tri_bf, x):
    p1 = x.astype(BF16)
    r1 = x - p1.astype(F32)
    p2 = r1.astype(BF16)
    p3 = (r1 - p2.astype(F32)).astype(BF16)
    dot = lambda p: jnp.dot(tri_bf, p, preferred_element_type=F32)
    return (dot(p1) + dot(p2)) + dot(p3)


def _gla_kernel(q_ref, k_ref, v_ref, r_ref, sm_ref, au_ref, ab_ref, gn_ref, out_ref, st_ref,
                *, chunk, nsub, scale, hps, dk, dv):
    @pl.when(pl.program_id(2) == 0)
    def _():
        st_ref[...] = jnp.zeros_like(st_ref)

    row = lax.broadcasted_iota(jnp.int32, (chunk, chunk), 0)
    col = lax.broadcasted_iota(jnp.int32, (chunk, chunk), 1)
    causal = row >= col
    tri = causal.astype(BF16)
    key_row = lax.broadcasted_iota(jnp.int32, (chunk, 1), 0)
    for s in range(nsub):
        rows = pl.ds(s * chunk, chunk)
        ga = sm_ref[rows, :].astype(BF16)
        for hh in range(hps):
            kc = slice(hh * dk, (hh + 1) * dk)
            vc = slice(hh * dv, (hh + 1) * dv)
            z = jnp.dot(ga, au_ref[:, kc], preferred_element_type=F32)
            la = _log_sigmoid(z + ab_ref[:, kc]) * (1.0 / GLA_GATE_NORM)
            b = _prefix_sum_rows(tri, la)
            b_last = b[chunk - 1:chunk, :]
            q = q_ref[rows, kc].astype(F32)
            k = k_ref[rows, kc].astype(F32)
            v = v_ref[rows, vc]
            qd = (q * (scale * jnp.exp(b))).astype(BF16)
            kl = (k * jnp.exp(b_last - b)).astype(BF16)
            blocks = []
            for gi in range(chunk // GLA_SUB):
                grp = slice(gi * GLA_SUB, (gi + 1) * GLA_SUB)
                if gi == 0:
                    q_fac, k_arg = b[grp], -b
                else:
                    r = b[gi * GLA_SUB - 1:gi * GLA_SUB, :]
                    q_fac, k_arg = b[grp] - r, r - b
                k_arg = jnp.where(key_row < (gi + 1) * GLA_SUB, k_arg, -jnp.inf)
                qg = (q[grp] * (scale * jnp.exp(q_fac))).astype(BF16)
                kg = (k * jnp.exp(k_arg)).astype(BF16)
                blocks.append(lax.dot_general(qg, kg, NT_DIMS, preferred_element_type=F32))
            sc = jnp.where(causal, jnp.concatenate(blocks, axis=0), 0.0).astype(BF16)
            st = st_ref[hh]
            o = jnp.dot(sc, v, preferred_element_type=F32) + lax.dot_general(
                qd, st.astype(BF16), NT_DIMS, preferred_element_type=F32)
            st_ref[hh] = st * jnp.exp(b_last) + lax.dot_general(
                v, kl, TN_DIMS, preferred_element_type=F32)
            r = r_ref[rows, vc].astype(F32)
            out_ref[rows, vc] = (_rms(o, gn_ref[...]) * (r * jax.nn.sigmoid(r))).astype(out_ref.dtype)


def _gla(proj, small, alpha_up_pad, alpha_bias, out_norm_g, *, batch, seq, dk, dv, offs, rows, hps):
    t = batch * seq
    nblk = seq // rows
    h = GLA_HEADS
    wk, wv = hps * dk, hps * dv
    qo, ko, vo, ro = (offs[0] // wk, offs[1] // wk, offs[2] // wv, offs[3] // wv)
    rmap = lambda b, hh, c: b * nblk + c
    return pl.pallas_call(
        functools.partial(_gla_kernel, chunk=GLA_CHUNK, nsub=rows // GLA_CHUNK, scale=dk ** -0.5,
                          hps=hps, dk=dk, dv=dv),
        grid=(batch, h // hps, nblk),
        in_specs=[
            pl.BlockSpec((rows, wk), lambda b, hh, c: (rmap(b, hh, c), qo + hh)),
            pl.BlockSpec((rows, wk), lambda b, hh, c: (rmap(b, hh, c), ko + hh)),
            pl.BlockSpec((rows, wv), lambda b, hh, c: (rmap(b, hh, c), vo + hh)),
            pl.BlockSpec((rows, wv), lambda b, hh, c: (rmap(b, hh, c), ro + hh)),
            pl.BlockSpec((rows, LANES), lambda b, hh, c: (rmap(b, hh, c), 0)),
            pl.BlockSpec((LANES, wk), lambda b, hh, c: (0, hh)),
            pl.BlockSpec((1, wk), lambda b, hh, c: (0, hh)),
            pl.BlockSpec((1, dv), lambda b, hh, c: (0, 0)),
        ],
        out_specs=pl.BlockSpec((rows, wv), lambda b, hh, c: (rmap(b, hh, c), hh)),
        out_shape=jax.ShapeDtypeStruct((t, h * dv), BF16),
        scratch_shapes=[pltpu.VMEM((hps, dv, dk), F32)],
        compiler_params=_params(("arbitrary", "arbitrary", "arbitrary")),
        name="gla",
    )(proj, proj, proj, proj, small, alpha_up_pad, alpha_bias, out_norm_g)


def _fcum_kernel(sm_ref, bias_ref, out_ref, carry_ref, *, tb):
    @pl.when(pl.program_id(1) == 0)
    def _():
        carry_ref[...] = jnp.zeros_like(carry_ref)

    row = lax.broadcasted_iota(jnp.int32, (tb, tb), 0)
    col = lax.broadcasted_iota(jnp.int32, (tb, tb), 1)
    tri = (row >= col).astype(BF16)
    lf = _log_sigmoid(sm_ref[...] + bias_ref[...])
    c = _prefix_sum_rows(tri, lf) + carry_ref[...]
    out_ref[...] = c
    carry_ref[...] = c[tb - 1:tb, :]


def _fcum(small, bias_pad, *, batch, seq, tb):
    nb = seq // tb
    return pl.pallas_call(
        functools.partial(_fcum_kernel, tb=tb),
        grid=(batch, nb),
        in_specs=[pl.BlockSpec((tb, LANES), lambda b, i: (b * nb + i, 0)),
                  pl.BlockSpec((1, LANES), lambda b, i: (0, 0))],
        out_specs=pl.BlockSpec((tb, LANES), lambda b, i: (b * nb + i, 0)),
        out_shape=jax.ShapeDtypeStruct(small.shape, F32),
        scratch_shapes=[pltpu.VMEM((1, LANES), F32)],
        compiler_params=_params(("arbitrary", "arbitrary")),
        name="forget_cumsum",
    )(small, bias_pad)


def _fox_kernel(qt_ref, kt_ref, q_ref, k_ref, v_ref, c_ref, out_ref, m_ref, mn_ref, a_ref, l_ref,
                acc_ref, s_ref, p_ref, *, tq, rb, grp):
    p = pl.program_id(2)
    qi = qt_ref[p]
    ki = kt_ref[p]

    @pl.when(ki == 0)
    def _():
        m_ref[...] = jnp.full_like(m_ref, -jnp.inf)
        l_ref[...] = jnp.zeros_like(l_ref)
        acc_ref[...] = jnp.zeros_like(acc_ref)

    def update(diag):
        c = c_ref[0]

        def ncols(g):
            return (g + 1) * grp if diag else tq

        def qk(g):
            n = ncols(g)
            s_ref[g % 2, :, 0:n] = lax.dot_general(q_ref[g * grp:(g + 1) * grp, :], k_ref[0:n, :],
                                                   NT_DIMS, preferred_element_type=F32)

        def pv(g):
            n = ncols(g)
            rows = slice(g * grp, (g + 1) * grp)
            acc_ref[rows, :] += jnp.dot(p_ref[g % 2, :, 0:n], v_ref[0:n, :],
                                        preferred_element_type=F32)

        def softmax(g):
            buf = g % 2
            strips = []
            for st in range(grp // rb):
                r0 = g * grp + st * rb
                visible = (r0 + rb - 1) // LANES + 1 if diag else tq // LANES
                strips.append((slice(st * rb, (st + 1) * rb), slice(r0, r0 + rb), r0, visible))

            def logits(rows_l, r0, t):
                cols = slice(t * LANES, (t + 1) * LANES)
                s = s_ref[buf, rows_l, cols] - c[:, cols]
                if diag and (t + 1) * LANES - 1 > r0:
                    row = r0 + lax.broadcasted_iota(jnp.int32, (rb, LANES), 0)
                    col = t * LANES + lax.broadcasted_iota(jnp.int32, (rb, LANES), 1)
                    s = jnp.where(row >= col, s, -jnp.inf)
                return s

            for rows_l, rows, r0, visible in strips:
                part = logits(rows_l, r0, 0)
                for t in range(1, visible):
                    part = jnp.maximum(part, logits(rows_l, r0, t))
                m_old = m_ref[rows, :]
                m_new = jnp.maximum(m_old, jnp.max(part, axis=-1, keepdims=True))
                a_ref[rows, :] = jnp.exp2(m_old - m_new)
                mn_ref[rows, :] = m_new
            for rows_l, rows, r0, visible in strips:
                m_new = mn_ref[rows, :]
                m_ref[rows, :] = m_new
                part = jnp.zeros((rb, LANES), F32)
                for t in range(ncols(g) // LANES):
                    cols = slice(t * LANES, (t + 1) * LANES)
                    if t < visible:
                        pr = jnp.exp2(logits(rows_l, r0, t) - m_new)
                        part = part + pr
                        p_ref[buf, rows_l, cols] = pr.astype(BF16)
                    else:
                        p_ref[buf, rows_l, cols] = jnp.zeros((rb, LANES), BF16)
                alpha = a_ref[rows, :]
                l_ref[rows, :] = alpha * l_ref[rows, :] + part
                acc_ref[rows, :] = alpha * acc_ref[rows, :]

        ngrp = tq // grp
        qk(0)
        for g in range(ngrp):
            if g + 1 < ngrp:
                qk(g + 1)
            softmax(g)
            if g >= 1:
                pv(g - 1)
        pv(ngrp - 1)

    @pl.when(ki < qi)
    def _():
        update(False)

    @pl.when(ki == qi)
    def _():
        update(True)
        denom = jnp.sum(l_ref[...], axis=-1, keepdims=True)
        out_ref[...] = (acc_ref[...] / denom).astype(out_ref.dtype)


def _fox(proj, c_row, *, batch, seq, dh, offs, tq):
    t = batch * seq
    nq = seq // tq
    pairs = [(a, b) for a in range(nq) for b in range(a + 1)]
    qt = jnp.asarray([a for a, _ in pairs], jnp.int32)
    kt = jnp.asarray([b for _, b in pairs], jnp.int32)
    qo, ko, vo = (o // dh for o in offs)
    grp = min(tq, FOX_GROUP)
    grid_spec = pltpu.PrefetchScalarGridSpec(
        num_scalar_prefetch=2,
        grid=(batch, FOX_HEADS, len(pairs)),
        in_specs=[
            pl.BlockSpec((tq, dh), lambda b, h, p, qt, kt: (b * nq + qt[p], qo + h)),
            pl.BlockSpec((tq, dh), lambda b, h, p, qt, kt: (b * nq + kt[p], ko + h)),
            pl.BlockSpec((tq, dh), lambda b, h, p, qt, kt: (b * nq + kt[p], vo + h)),
            pl.BlockSpec((1, 1, tq), lambda b, h, p, qt, kt: (b * FOX_HEADS + h, 0, kt[p])),
        ],
        out_specs=pl.BlockSpec((tq, dh), lambda b, h, p, qt, kt: (b * nq + qt[p], h)),
        scratch_shapes=[pltpu.VMEM((tq, LANES), F32)] * 4 + [
                        pltpu.VMEM((tq, dh), F32), pltpu.VMEM((2, grp, tq), F32),
                        pltpu.VMEM((2, grp, tq), BF16)],
    )
    return pl.pallas_call(
        functools.partial(_fox_kernel, tq=tq, rb=FOX_STRIP, grp=grp),
        grid_spec=grid_spec,
        out_shape=jax.ShapeDtypeStruct((t, FOX_HEADS * dh), BF16),
        compiler_params=_params(("arbitrary", "arbitrary", "arbitrary")),
        name="fox_attention",
    )(qt, kt, proj, proj, proj, c_row)


def _merge_kernel(a_ref, f_ref, ga_ref, gf_ref, wa_ref, wf_ref, out_ref):
    ya = jnp.dot(a_ref[...], wa_ref[...], preferred_element_type=F32)
    yf = jnp.dot(f_ref[...], wf_ref[...], preferred_element_type=F32)
    out_ref[...] = (jax.nn.sigmoid(ga_ref[...].astype(F32)) * ya
                    + jax.nn.sigmoid(gf_ref[...].astype(F32)) * yf).astype(out_ref.dtype)


def _merge(o_gla, o_fox, proj, w_gla, w_fox, *, offs, tm, tn):
    m, kdim = o_gla.shape
    n = w_gla.shape[1]
    go, fo = offs[0] // tn, offs[1] // tn
    return pl.pallas_call(
        _merge_kernel,
        grid=(m // tm, n // tn),
        in_specs=[
            pl.BlockSpec((tm, kdim), lambda i, j: (i, 0)),
            pl.BlockSpec((tm, o_fox.shape[1]), lambda i, j: (i, 0)),
            pl.BlockSpec((tm, tn), lambda i, j: (i, go + j)),
            pl.BlockSpec((tm, tn), lambda i, j: (i, fo + j)),
            pl.BlockSpec((kdim, tn), lambda i, j: (0, j)),
            pl.BlockSpec((o_fox.shape[1], tn), lambda i, j: (0, j)),
        ],
        out_specs=pl.BlockSpec((tm, tn), lambda i, j: (i, j)),
        out_shape=jax.ShapeDtypeStruct((m, n), BF16),
        compiler_params=_params(("arbitrary", "arbitrary")),
        name="gated_merge",
    )(o_gla, o_fox, proj, proj, w_gla, w_fox)


def _resid_matmul_kernel(a_ref, w_ref, res_ref, out_ref):
    out_ref[...] = res_ref[...] + jnp.dot(a_ref[...], w_ref[...], preferred_element_type=F32)


def _resid_matmul(a, w, res, *, tm, tn):
    m, kdim = a.shape
    n = w.shape[1]
    return pl.pallas_call(
        _resid_matmul_kernel,
        grid=(m // tm, n // tn),
        in_specs=[pl.BlockSpec((tm, kdim), lambda i, j: (i, 0)),
                  pl.BlockSpec((kdim, tn), lambda i, j: (0, j)),
                  pl.BlockSpec((tm, tn), lambda i, j: (i, j))],
        out_specs=pl.BlockSpec((tm, tn), lambda i, j: (i, j)),
        out_shape=jax.ShapeDtypeStruct((m, n), F32),
        compiler_params=_params(("arbitrary", "arbitrary")),
        name="resid_matmul",
    )(a, w, res)


def _memory_kernel(x_ref, gn_ref, wq_ref, gq_ref, k_ref, v_ref, wo_ref, gm_ref, wr_ref, br_ref,
                   x2_ref, rl_ref, xsp_ref, *, dh):
    x1 = x_ref[...]
    hb = _rms(x1, gn_ref[...]).astype(BF16)
    q = jnp.dot(hb, wq_ref[...], preferred_element_type=F32)
    outs = []
    for h in range(MEM_HEADS):
        sl = slice(h * dh, (h + 1) * dh)
        qn = _rms(q[:, sl], gq_ref[...]).astype(BF16)
        s = lax.dot_general(qn, k_ref[:, sl], NT_DIMS, preferred_element_type=F32)
        s = s - jnp.max(s, axis=-1, keepdims=True)
        p = jnp.exp(s)
        p = p / jnp.sum(p, axis=-1, keepdims=True)
        outs.append(jnp.dot(p.astype(BF16), v_ref[:, sl], preferred_element_type=F32).astype(BF16))
    o = jnp.concatenate(outs, axis=-1)
    x2 = x1 + jnp.dot(o, wo_ref[...], preferred_element_type=F32)
    x2_ref[...] = x2
    xs = _rms(x2, gm_ref[...])
    rl_ref[...] = jnp.dot(xs.astype(BF16), wr_ref[...], preferred_element_type=F32) + br_ref[...]
    xsp_ref[...] = _pack_bf16_pairs(xs)


def _memory(x1, mem_norm_g, wq, gq, kv, wo, moe_norm_g, wr_pad, br_pad, *, batch, seq,
            mem_len, tm):
    t, d = x1.shape
    w = wq.shape[1]
    dh = w // MEM_HEADS
    nblk = seq // tm
    const = lambda i: (0, 0)
    return pl.pallas_call(
        functools.partial(_memory_kernel, dh=dh),
        grid=(t // tm,),
        in_specs=[
            pl.BlockSpec((tm, d), lambda i: (i, 0)),
            pl.BlockSpec((1, d), const),
            pl.BlockSpec((d, w), const),
            pl.BlockSpec((1, dh), const),
            pl.BlockSpec((mem_len, w), lambda i: (i // nblk, 0)),
            pl.BlockSpec((mem_len, w), lambda i: (i // nblk, 1)),
            pl.BlockSpec((w, d), const),
            pl.BlockSpec((1, d), const),
            pl.BlockSpec((d, LANES), const),
            pl.BlockSpec((1, LANES), const),
        ],
        out_specs=[pl.BlockSpec((tm, d), lambda i: (i, 0)),
                   pl.BlockSpec((tm, LANES), lambda i: (i, 0)),
                   pl.BlockSpec((tm, d // 2), lambda i: (i, 0))],
        out_shape=[jax.ShapeDtypeStruct((t, d), F32), jax.ShapeDtypeStruct((t, LANES), F32),
                   jax.ShapeDtypeStruct((t, d // 2), jnp.uint32)],
        compiler_params=_params(("arbitrary",)),
        name="memory_block",
    )(x1, mem_norm_g, wq, gq, kv, kv, wo, moe_norm_g, wr_pad, br_pad)


def _route_kernel(rl_ref, route_ref, counts_ref, carry_ref, *, tr, n_experts):
    @pl.when(pl.program_id(0) == 0)
    def _():
        carry_ref[...] = jnp.zeros_like(carry_ref)

    lane = lax.broadcasted_iota(jnp.int32, (tr, LANES), 1)
    lg = jnp.where(lane < n_experts, rl_ref[...], -jnp.inf)
    vals, hots = [], []
    for _ in range(TOP_K):
        mx = jnp.max(lg, axis=-1, keepdims=True)
        idx = jnp.min(jnp.where(lg == mx, lane, LANES), axis=-1, keepdims=True)
        hot = lane == idx
        vals.append(mx)
        hots.append(hot)
        lg = jnp.where(hot, -jnp.inf, lg)
    exps = [jnp.exp(v - vals[0]) for v in vals]
    denom = exps[0]
    for e in exps[1:]:
        denom = denom + e
    onehot = hots[0]
    for hsel in hots[1:]:
        onehot = jnp.logical_or(onehot, hsel)
    onehot_f = onehot.astype(F32)
    row = lax.broadcasted_iota(jnp.int32, (tr, tr), 0)
    col = lax.broadcasted_iota(jnp.int32, (tr, tr), 1)
    strict = (row > col).astype(BF16)
    before = jnp.dot(strict, onehot_f.astype(BF16), preferred_element_type=F32) + carry_ref[...]
    route = jnp.zeros((tr, LANES), F32)
    for k in range(TOP_K):
        idx_f = jnp.sum(jnp.where(hots[k], lane, 0), axis=-1, keepdims=True).astype(F32)
        rank = jnp.sum(jnp.where(hots[k], before, 0.0), axis=-1, keepdims=True)
        route = jnp.where(lane == k, idx_f, route)
        route = jnp.where(lane == TOP_K + k, exps[k] / denom, route)
        route = jnp.where(lane == 2 * TOP_K + k, rank, route)
    route_ref[...] = route
    total = carry_ref[...] + jnp.sum(onehot_f, axis=0, keepdims=True)
    carry_ref[...] = total
    counts_ref[...] = jnp.broadcast_to(total, counts_ref.shape)


def _route(rlogits, *, n_experts, tr):
    t = rlogits.shape[0]
    return pl.pallas_call(
        functools.partial(_route_kernel, tr=tr, n_experts=n_experts),
        grid=(t // tr,),
        in_specs=[pl.BlockSpec((tr, LANES), lambda i: (i, 0))],
        out_specs=[pl.BlockSpec((tr, LANES), lambda i: (i, 0)),
                   pl.BlockSpec((8, LANES), lambda i: (0, 0))],
        out_shape=[jax.ShapeDtypeStruct((t, LANES), F32), jax.ShapeDtypeStruct((8, LANES), F32)],
        scratch_shapes=[pltpu.VMEM((1, LANES), F32)],
        compiler_params=_params(("arbitrary",)),
        name="route_topk",
    )(rlogits)


def _gather_kernel(tok_ref, nused_ref, x_hbm, out_ref, buf, sem, *, rows):
    i = pl.program_id(0)
    nused = nused_ref[0]

    def issue(blk, slot):
        def body(r8, carry):
            for j in range(SUBLANES):
                tok = tok_ref[blk * rows + r8 * SUBLANES + j]
                pltpu.make_async_copy(
                    x_hbm.at[tok >> 3, pl.ds(tok & (SUBLANES - 1), 1)],
                    buf.at[slot, r8, pl.ds(j, 1)], sem.at[slot]).start()
            return carry
        lax.fori_loop(0, rows // SUBLANES, body, 0)

    @pl.when(i == 0)
    def _():
        issue(0, 0)

    @pl.when(i + 1 < nused)
    def _():
        issue(i + 1, (i + 1) % 2)

    @pl.when(i < nused)
    def _():
        slot = i % 2
        pltpu.make_async_copy(x_hbm.at[pl.ds(0, rows // SUBLANES)], buf.at[slot],
                              sem.at[slot]).wait()
        lo, hi = _unpack_bf16_pairs(buf[slot].reshape(rows, -1))
        half = lo.shape[1]
        out_ref[:, :half] = lo.astype(out_ref.dtype)
        out_ref[:, half:] = hi.astype(out_ref.dtype)

    @pl.when(i >= nused)
    def _():
        out_ref[...] = jnp.zeros_like(out_ref)


def _gather_rows(slot_tok, nused, xs_packed, *, rows):
    p = slot_tok.shape[0]
    half = xs_packed.shape[1]
    grid_spec = pltpu.PrefetchScalarGridSpec(
        num_scalar_prefetch=2,
        grid=(p // rows,),
        in_specs=[pl.BlockSpec(memory_space=pl.ANY)],
        out_specs=pl.BlockSpec((rows, 2 * half), lambda i, tok, nu: (i, 0)),
        scratch_shapes=[pltpu.VMEM((2, rows // SUBLANES, SUBLANES, half), jnp.uint32),
                        pltpu.SemaphoreType.DMA((2,))],
    )
    return pl.pallas_call(
        functools.partial(_gather_kernel, rows=rows),
        grid_spec=grid_spec,
        out_shape=jax.ShapeDtypeStruct((p, 2 * half), BF16),
        compiler_params=_params(("arbitrary",)),
        name="moe_gather",
    )(slot_tok, nused, xs_packed.reshape(-1, SUBLANES, half))


def _start_all(copies):
    for c in copies:
        c.start()


def _wait_all(copies):
    for c in copies:
        c.wait()


def _weight_pipeline(step, flag, tabs, copies_for, convert):
    exp_ref, tile_ref, gslot_ref, nexp_ref, ntile_ref = tabs

    @pl.when((flag & 2) != 0)
    def _():
        slot = gslot_ref[step]
        mine = copies_for(exp_ref[step], tile_ref[step], slot)

        @pl.when(step == 0)
        def _():
            _start_all(mine)

        _wait_all(mine)
        convert(slot)

        @pl.when((flag & 4) != 0)
        def _():
            _start_all(copies_for(nexp_ref[step], ntile_ref[step], 1 - slot))


def _expert_up_kernel(blk_ref, tile_ref, exp_ref, flag_ref, gslot_ref, nexp_ref, ntile_ref,
                      x_ref, w_hbm, bg_ref, bl_ref, out_ref, wbuf, wg_bf, wl_bf, sem, *, tf, nt):
    step = pl.program_id(0)
    flag = flag_ref[step]

    def copies_for(e, tile, slot):
        return [pltpu.make_async_copy(
            w_hbm.at[e, :, pl.ds(pl.multiple_of((half * nt + tile) * tf, tf), tf)],
            wbuf.at[slot, half], sem.at[slot, half]) for half in range(2)]

    def convert(slot):
        wg_bf[...] = wbuf[slot, 0].astype(BF16)
        wl_bf[...] = wbuf[slot, 1].astype(BF16)

    _weight_pipeline(step, flag, (exp_ref, tile_ref, gslot_ref, nexp_ref, ntile_ref), copies_for,
                     convert)

    @pl.when((flag & 1) != 0)
    def _():
        x = x_ref[...]
        glu = jnp.dot(x, wg_bf[...], preferred_element_type=F32) + bg_ref[...]
        lin = jnp.dot(x, wl_bf[...], preferred_element_type=F32) + bl_ref[...]
        glu = jnp.minimum(glu, SWIGLU_LIMIT)
        lin = jnp.clip(lin, -SWIGLU_LIMIT, SWIGLU_LIMIT)
        out_ref[...] = (glu * jax.nn.sigmoid(SWIGLU_ALPHA * glu) * (lin + 1.0)).astype(out_ref.dtype)

    @pl.when((flag & 1) == 0)
    def _():
        out_ref[...] = jnp.zeros_like(out_ref)


def _expert_up(tables, xg, w_up, b_up, *, tm, tf):
    p, d = xg.shape
    n_e, _, ff2 = w_up.shape
    ff = ff2 // 2
    nt = ff // tf
    nsteps = tables[0].shape[0]
    bias = lambda half: pl.BlockSpec((None, 1, tf),
                                     lambda s, blk, tile, ex, *_: (ex[s], 0, half * nt + tile[s]))
    grid_spec = pltpu.PrefetchScalarGridSpec(
        num_scalar_prefetch=len(tables),
        grid=(nsteps,),
        in_specs=[
            pl.BlockSpec((tm, d), lambda s, blk, *_: (blk[s], 0)),
            pl.BlockSpec(memory_space=pl.ANY),
            bias(0), bias(1),
        ],
        out_specs=pl.BlockSpec((tm, tf), lambda s, blk, tile, *_: (blk[s], tile[s])),
        scratch_shapes=[pltpu.VMEM((2, 2, d, tf), F32), pltpu.VMEM((d, tf), BF16),
                        pltpu.VMEM((d, tf), BF16), pltpu.SemaphoreType.DMA((2, 2))],
    )
    return pl.pallas_call(
        functools.partial(_expert_up_kernel, tf=tf, nt=nt),
        grid_spec=grid_spec,
        out_shape=jax.ShapeDtypeStruct((p, ff), BF16),
        compiler_params=_params(("arbitrary",), VMEM_LIMIT_MAX),
        name="expert_up",
    )(*tables, xg, w_up, b_up, b_up)


def _expert_down_kernel(blk_ref, tile_ref, exp_ref, flag_ref, gslot_ref, nexp_ref, ntile_ref,
                        a_ref, w_hbm, b_ref, out_ref, wbuf, w_bf, sem, *, tn):
    step = pl.program_id(0)
    flag = flag_ref[step]

    def copies_for(e, tile, slot):
        return [pltpu.make_async_copy(w_hbm.at[e, :, pl.ds(pl.multiple_of(tile * tn, tn), tn)],
                                      wbuf.at[slot], sem.at[slot])]

    def convert(slot):
        w_bf[...] = wbuf[slot].astype(BF16)

    _weight_pipeline(step, flag, (exp_ref, tile_ref, gslot_ref, nexp_ref, ntile_ref), copies_for,
                     convert)

    @pl.when((flag & 1) != 0)
    def _():
        y = jnp.dot(a_ref[...], w_bf[...], preferred_element_type=F32) + b_ref[...]
        out_ref[...] = _pack_bf16_pairs(y)

    @pl.when((flag & 1) == 0)
    def _():
        out_ref[...] = jnp.zeros_like(out_ref)


def _expert_down(tables, act, w_down, b_down, *, tm, tn):
    p, ff = act.shape
    d = w_down.shape[2]
    nsteps = tables[0].shape[0]
    grid_spec = pltpu.PrefetchScalarGridSpec(
        num_scalar_prefetch=len(tables),
        grid=(nsteps,),
        in_specs=[
            pl.BlockSpec((tm, ff), lambda s, blk, *_: (blk[s], 0)),
            pl.BlockSpec(memory_space=pl.ANY),
            pl.BlockSpec((None, 1, tn), lambda s, blk, tile, ex, *_: (ex[s], 0, tile[s])),
        ],
        out_specs=pl.BlockSpec((tm, tn // 2), lambda s, blk, tile, *_: (blk[s], tile[s])),
        scratch_shapes=[pltpu.VMEM((2, ff, tn), F32), pltpu.VMEM((ff, tn), BF16),
                        pltpu.SemaphoreType.DMA((2,))],
    )
    return pl.pallas_call(
        functools.partial(_expert_down_kernel, tn=tn),
        grid_spec=grid_spec,
        out_shape=jax.ShapeDtypeStruct((p, d // 2), jnp.uint32),
        compiler_params=_params(("arbitrary",)),
        name="expert_down",
    )(*tables, act, w_down, b_down)


def _expert_schedule(counts, *, tm, n_tiles, n_blocks):
    n_e = counts.shape[0]
    nb = (counts + tm - 1) // tm
    bend = jnp.cumsum(nb)
    bstart = bend - nb
    steps_e = nb * n_tiles
    cs = jnp.cumsum(steps_e)
    total = cs[-1]
    s = jnp.arange(n_blocks * n_tiles, dtype=jnp.int32)
    sc = jnp.minimum(s, total - 1)
    e = jnp.sum((cs[None, :] <= sc[:, None]).astype(jnp.int32), axis=1)
    e = jnp.minimum(e, n_e - 1)
    r = sc - (cs[e] - steps_e[e])
    nbe = jnp.maximum(nb[e], 1)
    valid = s < total
    n_unused = jnp.maximum(n_blocks - bend[-1], 1)
    u = jnp.maximum(s - total, 0)
    live_tile = r // nbe
    tile = jnp.where(valid, live_tile, u // n_unused)
    blk = jnp.where(valid, bstart[e] + r % nbe, bend[-1] + u % n_unused)
    first = jnp.logical_and(valid, r % nbe == 0)
    ids = jnp.where(nb > 0, jnp.arange(n_e, dtype=jnp.int32), n_e)
    later = jnp.concatenate([lax.cummin(ids, reverse=True)[1:], jnp.full((1,), n_e, jnp.int32)])
    wraps = live_tile + 1 >= n_tiles
    nexp = jnp.where(wraps, later[e], e)
    ntile = jnp.where(wraps, 0, live_tile + 1)
    flag = (valid.astype(jnp.int32) + 2 * first.astype(jnp.int32)
            + 4 * jnp.logical_and(first, nexp < n_e).astype(jnp.int32))
    gslot = (jnp.cumsum(first.astype(jnp.int32)) - 1) % 2
    i32 = lambda a: a.astype(jnp.int32)
    return (i32(blk), i32(tile), i32(e), flag, i32(gslot), i32(jnp.minimum(nexp, n_e - 1)),
            i32(ntile))


def _combine_kernel(slot_ref, x_ref, route_ref, y_hbm, out_ref, buf, sem, *, tc, tn):
    i = pl.program_id(0)
    n = pl.num_programs(0)

    def issue(blk, bslot):
        def body(r, carry):
            for k in range(TOP_K):
                src = slot_ref[(blk * tc + r) * TOP_K + k]
                pltpu.make_async_copy(y_hbm.at[pl.ds(src, 1)], buf.at[bslot, k, pl.ds(r, 1)],
                                      sem.at[bslot]).start()
            return carry
        lax.fori_loop(0, tc, body, 0, unroll=4)

    @pl.when(i == 0)
    def _():
        issue(0, 0)

    @pl.when(i + 1 < n)
    def _():
        issue(i + 1, (i + 1) % 2)

    bslot = i % 2
    for k in range(TOP_K):
        pltpu.make_async_copy(y_hbm.at[pl.ds(0, tc)], buf.at[bslot, k], sem.at[bslot]).wait()
    gates = [route_ref[:, TOP_K + k:TOP_K + k + 1] for k in range(TOP_K)]
    hw = tn // 2
    for j in range(out_ref.shape[1] // tn):
        parts = [_unpack_bf16_pairs(buf[bslot, k, :, j * hw:(j + 1) * hw]) for k in range(TOP_K)]
        for half in range(2):
            cols = slice(j * tn + half * hw, j * tn + (half + 1) * hw)
            acc = x_ref[:, cols]
            for k in range(TOP_K):
                acc = acc + gates[k] * parts[k][half]
            out_ref[:, cols] = acc


def _combine(slot_flat, x2, route, y, *, tc, tn):
    t, d = x2.shape
    grid_spec = pltpu.PrefetchScalarGridSpec(
        num_scalar_prefetch=1,
        grid=(t // tc,),
        in_specs=[pl.BlockSpec((tc, d), lambda i, sl: (i, 0)),
                  pl.BlockSpec((tc, LANES), lambda i, sl: (i, 0)),
                  pl.BlockSpec(memory_space=pl.ANY)],
        out_specs=pl.BlockSpec((tc, d), lambda i, sl: (i, 0)),
        scratch_shapes=[pltpu.VMEM((2, TOP_K, tc, d // 2), jnp.uint32),
                        pltpu.SemaphoreType.DMA((2,))],
    )
    return pl.pallas_call(
        functools.partial(_combine_kernel, tc=tc, tn=tn),
        grid_spec=grid_spec,
        out_shape=jax.ShapeDtypeStruct((t, d), F32),
        compiler_params=_params(("arbitrary",)),
        name="moe_combine",
    )(slot_flat, x2, route, y)


def _pad_lanes(a, offset=0):
    width = a.shape[-1]
    return jnp.pad(a, [(0, 0)] * (a.ndim - 1) + [(offset, LANES - offset - width)])


def _layer(x2d, mem2d, lw, *, batch, seq, mem_len):
    t, d = x2d.shape
    gla_qk = lw["gla_alpha_up"].shape[1]
    gla_rank = lw["gla_alpha_up"].shape[0]
    gla_v = lw["w_branch_gla"].shape[0]
    fox_w = lw["w_branch_fox"].shape[0]
    dk, dv = gla_qk // GLA_HEADS, gla_v // GLA_HEADS
    dh = fox_w // FOX_HEADS
    mem_w = lw["mem_w_q"].shape[1]
    mem_dh = mem_w // MEM_HEADS
    n_experts = lw["w_router"].shape[1]
    ff = lw["w_expert_down"].shape[1]

    sizes = (gla_qk, gla_qk, gla_v, gla_v, gla_rank, fox_w, fox_w, fox_w, FOX_HEADS, d, d)
    starts = [0]
    for sz in sizes:
        starts.append(starts[-1] + sz)
    big_ids = (0, 1, 2, 3, 5, 6, 7, 9, 10)
    off = {}
    acc = 0
    for i in big_ids:
        off[i] = acc
        acc += sizes[i]
    n_big = acc
    w_big, w_small = _w_in_pack(
        lw["w_in"], spans=tuple((starts[i], sizes[i], off[i]) for i in big_ids),
        pieces=((starts[4], sizes[4], 0), (starts[8], sizes[8], sizes[4])), tr=TILE["w_pack_rows"])
    tn1 = TILE["proj_tn"]
    colscale = jnp.ones((1, n_big), F32)
    colscale = colscale.at[:, off[5]:off[5] + fox_w].set(
        jnp.tile(lw["fox_q_norm_g"] * (dh ** -0.5 * LOG2E), FOX_HEADS)[None])
    colscale = colscale.at[:, off[6]:off[6] + fox_w].set(jnp.tile(lw["fox_k_norm_g"], FOX_HEADS)[None])
    proj, small = _norm_matmul(
        x2d, lw["mix_norm_g"][None], w_big, colscale, w_small,
        tm=TILE["proj_tm"], tn=tn1, norm_lo=off[5] // tn1, norm_hi=(off[6] + fox_w) // tn1)

    alpha_up_pad = jnp.pad(lw["gla_alpha_up"], ((0, LANES - gla_rank), (0, 0))).astype(BF16)
    o_gla = _gla(proj, small, alpha_up_pad, lw["gla_alpha_bias"][None], lw["gla_out_norm_g"][None],
                 batch=batch, seq=seq, dk=dk, dv=dv, offs=(off[0], off[1], off[2], off[3]),
                 rows=TILE["gla_rows"], hps=TILE["gla_heads_per_step"])

    c = _fcum(small, _pad_lanes(lw["fox_f_bias"][None], gla_rank), batch=batch, seq=seq,
              tb=TILE["fcum_rows"])
    c_row = c[:, gla_rank:gla_rank + FOX_HEADS].reshape(batch, seq, FOX_HEADS)
    c_row = c_row.transpose(0, 2, 1).reshape(batch * FOX_HEADS, 1, seq) * LOG2E
    o_fox = _fox(proj, c_row, batch=batch, seq=seq, dh=dh, offs=(off[5], off[6], off[7]),
                 tq=min(seq, TILE["fox_tq"]))

    merged = _merge(o_gla, o_fox, proj, lw["w_branch_gla"].astype(BF16),
                    lw["w_branch_fox"].astype(BF16), offs=(off[9], off[10]),
                    tm=TILE["merge_tm"], tn=TILE["merge_tn"])
    x1 = _resid_matmul(merged, lw["w_out"].astype(BF16), x2d, tm=TILE["resid_tm"],
                       tn=TILE["resid_tn"])

    kv_scale = jnp.concatenate([jnp.tile(lw["mem_k_norm_g"], MEM_HEADS),
                                jnp.ones((mem_w,), F32)])[None]
    kv = _norm_matmul(mem2d, lw["mem_kv_norm_g"][None], lw["mem_w_kv"].astype(BF16), kv_scale, None,
                      tm=mem2d.shape[0], tn=mem_w, norm_lo=0, norm_hi=1)
    x2, rlogits, xs_packed = _memory(
        x1, lw["mem_norm_g"][None], lw["mem_w_q"].astype(BF16),
        (lw["mem_q_norm_g"] * (mem_dh ** -0.5))[None], kv,
        lw["mem_w_o"].astype(BF16), lw["moe_norm_g"][None],
        _pad_lanes(lw["w_router"]).astype(BF16), _pad_lanes(lw["b_router"][None]),
        batch=batch, seq=seq, mem_len=mem_len, tm=TILE["mem_tm"])

    route, counts_f = _route(rlogits, n_experts=n_experts, tr=TILE["route_rows"])
    tm_e = TILE["expert_tm"]
    a_total = t * TOP_K
    n_blocks = a_total // tm_e + n_experts
    p_rows = n_blocks * tm_e
    counts = counts_f[0, :n_experts].astype(jnp.int32)
    padded = (counts + tm_e - 1) // tm_e * tm_e
    pstart = jnp.cumsum(padded) - padded
    idx = route[:, :TOP_K].astype(jnp.int32)
    rank = route[:, 2 * TOP_K:3 * TOP_K].astype(jnp.int32)
    hit = idx[:, :, None] == jnp.arange(n_experts, dtype=jnp.int32)
    slot = (jnp.sum(jnp.where(hit, pstart, 0), axis=-1) + rank).reshape(a_total)
    slot_tok = (jnp.arange(p_rows, dtype=jnp.int32) % t).at[slot].set(
        jnp.arange(a_total, dtype=jnp.int32) // TOP_K, unique_indices=True,
        mode="promise_in_bounds")

    tf = TILE["expert_tf"]
    g_rows = TILE["gather_rows"]
    nused = (jnp.sum(padded) // g_rows).astype(jnp.int32).reshape(1)
    xg = _gather_rows(slot_tok, nused, xs_packed, rows=g_rows)
    up_tables = _expert_schedule(counts, tm=tm_e, n_tiles=ff // tf, n_blocks=n_blocks)
    act = _expert_up(up_tables, xg, lw["w_expert_up"], lw["b_expert_up"][:, None, :], tm=tm_e, tf=tf)
    tn_d = TILE["expert_tn"]
    down_tables = _expert_schedule(counts, tm=tm_e, n_tiles=d // tn_d, n_blocks=n_blocks)
    y = _expert_down(down_tables, act, lw["w_expert_down"], lw["b_expert_down"][:, None, :],
                     tm=tm_e, tn=tn_d)
    return _combine(slot, x2, route, y, tc=TILE["combine_rows"], tn=tn_d)


def kernel(x, mem, mix_norm_g, w_in, gla_alpha_up, gla_alpha_bias, gla_out_norm_g, fox_f_bias, fox_q_norm_g, fox_k_norm_g, w_branch_gla, w_branch_fox, w_out, mem_norm_g, mem_kv_norm_g, mem_w_q, mem_w_kv, mem_q_norm_g, mem_k_norm_g, mem_w_o, moe_norm_g, w_router, b_router, w_expert_up, b_expert_up, w_expert_down, b_expert_down):
    weights = dict(
        mix_norm_g=mix_norm_g, w_in=w_in, gla_alpha_up=gla_alpha_up, gla_alpha_bias=gla_alpha_bias,
        gla_out_norm_g=gla_out_norm_g, fox_f_bias=fox_f_bias, fox_q_norm_g=fox_q_norm_g,
        fox_k_norm_g=fox_k_norm_g, w_branch_gla=w_branch_gla, w_branch_fox=w_branch_fox, w_out=w_out,
        mem_norm_g=mem_norm_g, mem_kv_norm_g=mem_kv_norm_g, mem_w_q=mem_w_q, mem_w_kv=mem_w_kv,
        mem_q_norm_g=mem_q_norm_g, mem_k_norm_g=mem_k_norm_g, mem_w_o=mem_w_o, moe_norm_g=moe_norm_g,
        w_router=w_router, b_router=b_router, w_expert_up=w_expert_up, b_expert_up=b_expert_up,
        w_expert_down=w_expert_down, b_expert_down=b_expert_down)
    batch, seq, d = x.shape
    mem_len = mem.shape[1]
    x2d = x.reshape(batch * seq, d)
    mem2d = mem.reshape(batch * mem_len, d)
    for layer in range(mix_norm_g.shape[0]):
        lw = {name: w[layer] for name, w in weights.items()}
        x2d = _layer(x2d, mem2d, lw, batch=batch, seq=seq, mem_len=mem_len)
    return x2d.reshape(batch, seq, d)
```

```python
import functools

import jax
import jax.numpy as jnp
from jax import lax
from jax.experimental import pallas as pl
from jax.experimental.pallas import tpu as pltpu

F32 = jnp.float32
BF16 = jnp.bfloat16

EPS = 1e-6
LANES = 128
SUBLANES = 8
VMEM_LIMIT = 56 * 1024 * 1024
VMEM_LIMIT_MAX = 61 * 1024 * 1024

GLA_HEADS = 4
GLA_GATE_NORM = 16.0
GLA_CHUNK = 64
GLA_SUB = 16
FOX_HEADS = 16
FOX_STRIP = 16
FOX_GROUP = 128
LOG2E = 1.4426950408889634
MEM_HEADS = 4
TOP_K = 4
SWIGLU_LIMIT = 7.0
SWIGLU_ALPHA = 1.702

TILE = dict(
    w_pack_rows=128,
    proj_tm=1024, proj_tn=1024,
    gla_rows=256, gla_heads_per_step=4,
    fcum_rows=512,
    fox_tq=2048,
    merge_tm=512, merge_tn=1024,
    resid_tm=1024, resid_tn=1024,
    mem_tm=512,
    route_rows=512,
    expert_tm=512, expert_tf=1024, expert_tn=1024,
    gather_rows=512, combine_rows=256,
)

NT_DIMS = (((1,), (1,)), ((), ()))
TN_DIMS = (((0,), (0,)), ((), ()))


def _params(sem, vmem=VMEM_LIMIT):
    return pltpu.CompilerParams(dimension_semantics=sem, vmem_limit_bytes=vmem)


def _rms(xf, g):
    return xf * lax.rsqrt(jnp.mean(xf * xf, axis=-1, keepdims=True) + EPS) * g


def _pack_bf16_pairs(a):
    half = a.shape[1] // 2
    bits = lambda v: lax.bitcast_convert_type(v.astype(BF16).astype(F32), jnp.uint32)
    return (bits(a[:, half:]) & jnp.uint32(0xFFFF0000)) | (bits(a[:, :half]) >> 16)


def _unpack_bf16_pairs(u):
    lo = lax.bitcast_convert_type(u << 16, F32)
    hi = lax.bitcast_convert_type(u & jnp.uint32(0xFFFF0000), F32)
    return lo, hi


def _log_sigmoid(z):
    return jnp.minimum(z, 0.0) - jnp.log1p(jnp.exp(-jnp.abs(z)))


def _w_in_pack_kernel(w_ref, tail_ref, big_ref, small_ref, *, spans, pieces):
    rows, n_cols = w_ref.shape
    lane = lax.broadcasted_iota(jnp.int32, (rows, LANES), 1)
    tile_at = lambda c0: w_ref[:, c0:c0 + LANES]
    for s0, width, d0 in spans:
        sh = s0 % LANES
        a0 = s0 - sh
        for t in range(width // LANES):
            dst = slice(d0 + t * LANES, d0 + (t + 1) * LANES)
            if sh == 0:
                big_ref[:, dst] = tile_at(a0 + t * LANES).astype(BF16)
            elif a0 + (t + 2) * LANES <= n_cols:
                left = pltpu.roll(tile_at(a0 + t * LANES), LANES - sh, 1)
                right = pltpu.roll(tile_at(a0 + (t + 1) * LANES), LANES - sh, 1)
                big_ref[:, dst] = jnp.where(lane < LANES - sh, left, right).astype(BF16)
            else:
                assert s0 + (t + 1) * LANES == n_cols
                big_ref[:, dst] = tail_ref[...].astype(BF16)
    side = jnp.zeros((rows, LANES), F32)
    for s0, width, dl in pieces:
        sh = s0 % LANES
        tile = tile_at(s0 - sh)
        if (dl - sh) % LANES:
            tile = pltpu.roll(tile, (dl - sh) % LANES, 1)
        side = jnp.where(jnp.logical_and(lane >= dl, lane < dl + width), tile, side)
    small_ref[...] = side.astype(BF16)


def _w_in_pack(w_stack, layer, *, spans, pieces, tr):
    _, d, n_cols = w_stack.shape
    n_big = sum(width for _, width, _ in spans)
    w_big, w_small = pl.pallas_call(
        functools.partial(_w_in_pack_kernel, spans=spans, pieces=pieces),
        grid=(d // tr,),
        in_specs=[pl.BlockSpec((None, tr, n_cols), lambda i: (layer, i, 0)),
                  pl.BlockSpec((tr, LANES), lambda i: (i, 0))],
        out_specs=[pl.BlockSpec((tr, n_big), lambda i: (i, 0)),
                   pl.BlockSpec((tr, LANES), lambda i: (i, 0))],
        out_shape=[jax.ShapeDtypeStruct((d, n_big), BF16), jax.ShapeDtypeStruct((d, LANES), BF16)],
        compiler_params=_params(("arbitrary",)),
        name="w_in_pack",
    )(w_stack, w_stack[layer, :, n_cols - LANES:])
    return w_big, w_small


def _norm_matmul_kernel(*refs, norm_lo, norm_hi, tn, has_small):
    if has_small:
        x_ref, g_ref, w_ref, cs_ref, ws_ref, out_ref, small_ref, h_ref = refs
    else:
        x_ref, g_ref, w_ref, cs_ref, out_ref, h_ref = refs
    j = pl.program_id(1)

    @pl.when(j == 0)
    def _():
        hb = _rms(x_ref[...], g_ref[...]).astype(BF16)
        h_ref[...] = hb
        if has_small:
            small_ref[...] = jnp.dot(hb, ws_ref[...], preferred_element_type=F32)

    acc = jnp.dot(h_ref[...], w_ref[...], preferred_element_type=F32)
    is_norm = jnp.logical_and(j >= norm_lo, j < norm_hi)

    @pl.when(is_norm)
    def _():
        for c in range(tn // LANES):
            sl = slice(c * LANES, (c + 1) * LANES)
            out_ref[:, sl] = _rms(acc[:, sl], cs_ref[:, sl]).astype(out_ref.dtype)

    @pl.when(jnp.logical_not(is_norm))
    def _():
        out_ref[...] = acc.astype(out_ref.dtype)


def _norm_matmul(x, g, w, colscale, w_small, *, tm, tn, norm_lo, norm_hi):
    m, d = x.shape
    n = w.shape[1]
    has_small = w_small is not None
    in_specs = [
        pl.BlockSpec((tm, d), lambda i, j: (i, 0)),
        pl.BlockSpec((1, d), lambda i, j: (0, 0)),
        pl.BlockSpec((d, tn), lambda i, j: (0, j)),
        pl.BlockSpec((1, tn), lambda i, j: (0, j)),
    ]
    args = [x, g, w, colscale]
    out_shape = [jax.ShapeDtypeStruct((m, n), BF16)]
    out_specs = [pl.BlockSpec((tm, tn), lambda i, j: (i, j))]
    if has_small:
        in_specs.append(pl.BlockSpec((d, LANES), lambda i, j: (0, 0)))
        args.append(w_small)
        out_shape.append(jax.ShapeDtypeStruct((m, LANES), F32))
        out_specs.append(pl.BlockSpec((tm, LANES), lambda i, j: (i, 0)))
    res = pl.pallas_call(
        functools.partial(_norm_matmul_kernel, norm_lo=norm_lo, norm_hi=norm_hi, tn=tn,
                          has_small=has_small),
        grid=(m // tm, n // tn),
        in_specs=in_specs,
        out_specs=out_specs,
        out_shape=out_shape,
        scratch_shapes=[pltpu.VMEM((tm, d), BF16)],
        compiler_params=_params(("arbitrary", "arbitrary")),
        name="norm_matmul",
    )(*args)
    return res if has_small else res[0]


def _prefix_sum_rows(tri_bf, x):
    p1 = x.astype(BF16)
    r1 = x - p1.astype(F32)
    p2 = r1.astype(BF16)
    p3 = (r1 - p2.astype(F32)).astype(BF16)
    dot = lambda p: jnp.dot(tri_bf, p, preferred_element_type=F32)
    return (dot(p1) + dot(p2)) + dot(p3)


def _gla_kernel(q_ref, k_ref, v_ref, r_ref, sm_ref, au_ref, ab_ref, gn_ref, out_ref, st_ref,
                *, chunk, nsub, scale, hps, dk, dv):
    @pl.when(pl.program_id(2) == 0)
    def _():
        st_ref[...] = jnp.zeros_like(st_ref)

    row = lax.broadcasted_iota(jnp.int32, (chunk, chunk), 0)
    col = lax.broadcasted_iota(jnp.int32, (chunk, chunk), 1)
    causal = row >= col
    tri = causal.astype(BF16)
    key_row = lax.broadcasted_iota(jnp.int32, (chunk, 1), 0)
    for s in range(nsub):
        rows = pl.ds(s * chunk, chunk)
        ga = sm_ref[rows, :].astype(BF16)
        for hh in range(hps):
            kc = slice(hh * dk, (hh + 1) * dk)
            vc = slice(hh * dv, (hh + 1) * dv)
            z = jnp.dot(ga, au_ref[:, kc], preferred_element_type=F32)
            la = _log_sigmoid(z + ab_ref[:, kc]) * (1.0 / GLA_GATE_NORM)
            b = _prefix_sum_rows(tri, la)
            b_last = b[chunk - 1:chunk, :]
            q = q_ref[rows, kc].astype(F32)
            k = k_ref[rows, kc].astype(F32)
            v = v_ref[rows, vc]
            qd = (q * (scale * jnp.exp(b))).astype(BF16)
            kl = (k * jnp.exp(b_last - b)).astype(BF16)
            blocks = []
            for gi in range(chunk // GLA_SUB):
                grp = slice(gi * GLA_SUB, (gi + 1) * GLA_SUB)
                if gi == 0:
                    q_fac, k_arg = b[grp], -b
                else:
                    r = b[gi * GLA_SUB - 1:gi * GLA_SUB, :]
                    q_fac, k_arg = b[grp] - r, r - b
                k_arg = jnp.where(key_row < (gi + 1) * GLA_SUB, k_arg, -jnp.inf)
                qg = (q[grp] * (scale * jnp.exp(q_fac))).astype(BF16)
                kg = (k * jnp.exp(k_arg)).astype(BF16)
                blocks.append(lax.dot_general(qg, kg, NT_DIMS, preferred_element_type=F32))
            sc = jnp.where(causal, jnp.concatenate(blocks, axis=0), 0.0).astype(BF16)
            st = st_ref[hh]
            o = jnp.dot(sc, v, preferred_element_type=F32) + lax.dot_general(
                qd, st.astype(BF16), NT_DIMS, preferred_element_type=F32)
            st_ref[hh] = st * jnp.exp(b_last) + lax.dot_general(
                v, kl, TN_DIMS, preferred_element_type=F32)
            r = r_ref[rows, vc].astype(F32)
            out_ref[rows, vc] = (_rms(o, gn_ref[...]) * (r * jax.nn.sigmoid(r))).astype(out_ref.dtype)


def _gla(proj, small, alpha_up_pad, alpha_bias, out_norm_g, *, batch, seq, dk, dv, offs, rows, hps):
    t = batch * seq
    nblk = seq // rows
    h = GLA_HEADS
    wk, wv = hps * dk, hps * dv
    qo, ko, vo, ro = (offs[0] // wk, offs[1] // wk, offs[2] // wv, offs[3] // wv)
    rmap = lambda b, hh, c: b * nblk + c
    return pl.pallas_call(
        functools.partial(_gla_kernel, chunk=GLA_CHUNK, nsub=rows // GLA_CHUNK, scale=dk ** -0.5,
                          hps=hps, dk=dk, dv=dv),
        grid=(batch, h // hps, nblk),
        in_specs=[
            pl.BlockSpec((rows, wk), lambda b, hh, c: (rmap(b, hh, c), qo + hh)),
            pl.BlockSpec((rows, wk), lambda b, hh, c: (rmap(b, hh, c), ko + hh)),
            pl.BlockSpec((rows, wv), lambda b, hh, c: (rmap(b, hh, c), vo + hh)),
            pl.BlockSpec((rows, wv), lambda b, hh, c: (rmap(b, hh, c), ro + hh)),
            pl.BlockSpec((rows, LANES), lambda b, hh, c: (rmap(b, hh, c), 0)),
            pl.BlockSpec((LANES, wk), lambda b, hh, c: (0, hh)),
            pl.BlockSpec((1, wk), lambda b, hh, c: (0, hh)),
            pl.BlockSpec((1, dv), lambda b, hh, c: (0, 0)),
        ],
        out_specs=pl.BlockSpec((rows, wv), lambda b, hh, c: (rmap(b, hh, c), hh)),
        out_shape=jax.ShapeDtypeStruct((t, h * dv), BF16),
        scratch_shapes=[pltpu.VMEM((hps, dv, dk), F32)],
        compiler_params=_params(("arbitrary", "arbitrary", "arbitrary")),
        name="gla",
    )(proj, proj, proj, proj, small, alpha_up_pad, alpha_bias, out_norm_g)


def _fcum_kernel(sm_ref, bias_ref, out_ref, carry_ref, *, tb):
    @pl.when(pl.program_id(1) == 0)
    def _():
        carry_ref[...] = jnp.zeros_like(carry_ref)

    row = lax.broadcasted_iota(jnp.int32, (tb, tb), 0)
    col = lax.broadcasted_iota(jnp.int32, (tb, tb), 1)
    tri = (row >= col).astype(BF16)
    lf = _log_sigmoid(sm_ref[...] + bias_ref[...])
    c = _prefix_sum_rows(tri, lf) + carry_ref[...]
    out_ref[...] = c
    carry_ref[...] = c[tb - 1:tb, :]


def _fcum(small, bias_pad, *, batch, seq, tb):
    nb = seq // tb
    return pl.pallas_call(
        functools.partial(_fcum_kernel, tb=tb),
        grid=(batch, nb),
        in_specs=[pl.BlockSpec((tb, LANES), lambda b, i: (b * nb + i, 0)),
                  pl.BlockSpec((1, LANES), lambda b, i: (0, 0))],
        out_specs=pl.BlockSpec((tb, LANES), lambda b, i: (b * nb + i, 0)),
        out_shape=jax.ShapeDtypeStruct(small.shape, F32),
        scratch_shapes=[pltpu.VMEM((1, LANES), F32)],
        compiler_params=_params(("arbitrary", "arbitrary")),
        name="forget_cumsum",
    )(small, bias_pad)


def _fox_kernel(qt_ref, kt_ref, q_ref, k_ref, v_ref, c_ref, out_ref, m_ref, mn_ref, a_ref, l_ref,
                acc_ref, s_ref, p_ref, *, tq, rb, grp):
    p = pl.program_id(2)
    qi = qt_ref[p]
    ki = kt_ref[p]

    @pl.when(ki == 0)
    def _():
        m_ref[...] = jnp.full_like(m_ref, -jnp.inf)
        l_ref[...] = jnp.zeros_like(l_ref)
        acc_ref[...] = jnp.zeros_like(acc_ref)

    def update(diag):
        c = c_ref[0]

        def ncols(g):
            return (g + 1) * grp if diag else tq

        def qk(g):
            n = ncols(g)
            s_ref[g % 2, :, 0:n] = lax.dot_general(q_ref[g * grp:(g + 1) * grp, :], k_ref[0:n, :],
                                                   NT_DIMS, preferred_element_type=F32)

        def pv(g):
            n = ncols(g)
            rows = slice(g * grp, (g + 1) * grp)
            acc_ref[rows, :] += jnp.dot(p_ref[g % 2, :, 0:n], v_ref[0:n, :],
                                        preferred_element_type=F32)

        def softmax(g):
            buf = g % 2
            strips = []
            for st in range(grp // rb):
                r0 = g * grp + st * rb
                visible = (r0 + rb - 1) // LANES + 1 if diag else tq // LANES
                strips.append((slice(st * rb, (st + 1) * rb), slice(r0, r0 + rb), r0, visible))

            def logits(rows_l, r0, t):
                cols = slice(t * LANES, (t + 1) * LANES)
                s = s_ref[buf, rows_l, cols] - c[:, cols]
                if diag and (t + 1) * LANES - 1 > r0:
                    row = r0 + lax.broadcasted_iota(jnp.int32, (rb, LANES), 0)
                    col = t * LANES + lax.broadcasted_iota(jnp.int32, (rb, LANES), 1)
                    s = jnp.where(row >= col, s, -jnp.inf)
                return s

            for rows_l, rows, r0, visible in strips:
                part = logits(rows_l, r0, 0)
                for t in range(1, visible):
                    part = jnp.maximum(part, logits(rows_l, r0, t))
                m_old = m_ref[rows, :]
                m_new = jnp.maximum(m_old, jnp.max(part, axis=-1, keepdims=True))
                a_ref[rows, :] = jnp.exp2(m_old - m_new)
                mn_ref[rows, :] = m_new
            for rows_l, rows, r0, visible in strips:
                m_new = mn_ref[rows, :]
                m_ref[rows, :] = m_new
                part = jnp.zeros((rb, LANES), F32)
                for t in range(ncols(g) // LANES):
                    cols = slice(t * LANES, (t + 1) * LANES)
                    if t < visible:
                        pr = jnp.exp2(logits(rows_l, r0, t) - m_new)
                        part = part + pr
                        p_ref[buf, rows_l, cols] = pr.astype(BF16)
                    else:
                        p_ref[buf, rows_l, cols] = jnp.zeros((rb, LANES), BF16)
                alpha = a_ref[rows, :]
                l_ref[rows, :] = alpha * l_ref[rows, :] + part
                acc_ref[rows, :] = alpha * acc_ref[rows, :]

        ngrp = tq // grp
        qk(0)
        for g in range(ngrp):
            if g + 1 < ngrp:
                qk(g + 1)
            softmax(g)
            if g >= 1:
                pv(g - 1)
        pv(ngrp - 1)

    @pl.when(ki < qi)
    def _():
        update(False)

    @pl.when(ki == qi)
    def _():
        update(True)
        denom = jnp.sum(l_ref[...], axis=-1, keepdims=True)
        out_ref[...] = (acc_ref[...] / denom).astype(out_ref.dtype)


def _fox(proj, c_row, *, batch, seq, dh, offs, tq):
    t = batch * seq
    nq = seq // tq
    pairs = [(a, b) for a in range(nq) for b in range(a + 1)]
    qt = jnp.asarray([a for a, _ in pairs], jnp.int32)
    kt = jnp.asarray([b for _, b in pairs], jnp.int32)
    qo, ko, vo = (o // dh for o in offs)
    grp = min(tq, FOX_GROUP)
    grid_spec = pltpu.PrefetchScalarGridSpec(
        num_scalar_prefetch=2,
        grid=(batch, FOX_HEADS, len(pairs)),
        in_specs=[
            pl.BlockSpec((tq, dh), lambda b, h, p, qt, kt: (b * nq + qt[p], qo + h)),
            pl.BlockSpec((tq, dh), lambda b, h, p, qt, kt: (b * nq + kt[p], ko + h)),
            pl.BlockSpec((tq, dh), lambda b, h, p, qt, kt: (b * nq + kt[p], vo + h)),
            pl.BlockSpec((1, 1, tq), lambda b, h, p, qt, kt: (b * FOX_HEADS + h, 0, kt[p])),
        ],
        out_specs=pl.BlockSpec((tq, dh), lambda b, h, p, qt, kt: (b * nq + qt[p], h)),
        scratch_shapes=[pltpu.VMEM((tq, LANES), F32)] * 4 + [
                        pltpu.VMEM((tq, dh), F32), pltpu.VMEM((2, grp, tq), F32),
                        pltpu.VMEM((2, grp, tq), BF16)],
    )
    return pl.pallas_call(
        functools.partial(_fox_kernel, tq=tq, rb=FOX_STRIP, grp=grp),
        grid_spec=grid_spec,
        out_shape=jax.ShapeDtypeStruct((t, FOX_HEADS * dh), BF16),
        compiler_params=_params(("arbitrary", "arbitrary", "arbitrary")),
        name="fox_attention",
    )(qt, kt, proj, proj, proj, c_row)


def _merge_kernel(a_ref, f_ref, ga_ref, gf_ref, wa_ref, wf_ref, out_ref):
    ya = jnp.dot(a_ref[...], wa_ref[...], preferred_element_type=F32)
    yf = jnp.dot(f_ref[...], wf_ref[...], preferred_element_type=F32)
    out_ref[...] = (jax.nn.sigmoid(ga_ref[...].astype(F32)) * ya
                    + jax.nn.sigmoid(gf_ref[...].astype(F32)) * yf).astype(out_ref.dtype)


def _merge(o_gla, o_fox, proj, w_gla, w_fox, *, offs, tm, tn):
    m, kdim = o_gla.shape
    n = w_gla.shape[1]
    go, fo = offs[0] // tn, offs[1] // tn
    return pl.pallas_call(
        _merge_kernel,
        grid=(m // tm, n // tn),
        in_specs=[
            pl.BlockSpec((tm, kdim), lambda i, j: (i, 0)),
            pl.BlockSpec((tm, o_fox.shape[1]), lambda i, j: (i, 0)),
            pl.BlockSpec((tm, tn), lambda i, j: (i, go + j)),
            pl.BlockSpec((tm, tn), lambda i, j: (i, fo + j)),
            pl.BlockSpec((kdim, tn), lambda i, j: (0, j)),
            pl.BlockSpec((o_fox.shape[1], tn), lambda i, j: (0, j)),
        ],
        out_specs=pl.BlockSpec((tm, tn), lambda i, j: (i, j)),
        out_shape=jax.ShapeDtypeStruct((m, n), BF16),
        compiler_params=_params(("arbitrary", "arbitrary")),
        name="gated_merge",
    )(o_gla, o_fox, proj, proj, w_gla, w_fox)


def _resid_matmul_kernel(a_ref, w_ref, res_ref, out_ref):
    out_ref[...] = res_ref[...] + jnp.dot(a_ref[...], w_ref[...], preferred_element_type=F32)


def _resid_matmul(a, w, res, *, tm, tn):
    m, kdim = a.shape
    n = w.shape[1]
    return pl.pallas_call(
        _resid_matmul_kernel,
        grid=(m // tm, n // tn),
        in_specs=[pl.BlockSpec((tm, kdim), lambda i, j: (i, 0)),
                  pl.BlockSpec((kdim, tn), lambda i, j: (0, j)),
                  pl.BlockSpec((tm, tn), lambda i, j: (i, j))],
        out_specs=pl.BlockSpec((tm, tn), lambda i, j: (i, j)),
        out_shape=jax.ShapeDtypeStruct((m, n), F32),
        compiler_params=_params(("arbitrary", "arbitrary")),
        name="resid_matmul",
    )(a, w, res)


def _memory_kernel(x_ref, gn_ref, wq_ref, gq_ref, k_ref, v_ref, wo_ref, gm_ref, wr_ref, br_ref,
                   x2_ref, rl_ref, xsp_ref, *, dh):
    x1 = x_ref[...]
    hb = _rms(x1, gn_ref[...]).astype(BF16)
    q = jnp.dot(hb, wq_ref[...], preferred_element_type=F32)
    outs = []
    for h in range(MEM_HEADS):
        sl = slice(h * dh, (h + 1) * dh)
        qn = _rms(q[:, sl], gq_ref[...]).astype(BF16)
        s = lax.dot_general(qn, k_ref[:, sl], NT_DIMS, preferred_element_type=F32)
        s = s - jnp.max(s, axis=-1, keepdims=True)
        p = jnp.exp(s)
        p = p / jnp.sum(p, axis=-1, keepdims=True)
        outs.append(jnp.dot(p.astype(BF16), v_ref[:, sl], preferred_element_type=F32).astype(BF16))
    o = jnp.concatenate(outs, axis=-1)
    x2 = x1 + jnp.dot(o, wo_ref[...], preferred_element_type=F32)
    x2_ref[...] = x2
    xs = _rms(x2, gm_ref[...])
    rl_ref[...] = jnp.dot(xs.astype(BF16), wr_ref[...], preferred_element_type=F32) + br_ref[...]
    xsp_ref[...] = _pack_bf16_pairs(xs)


def _memory(x1, mem_norm_g, wq, gq, kv, wo, moe_norm_g, wr_pad, br_pad, *, batch, seq,
            mem_len, tm):
    t, d = x1.shape
    w = wq.shape[1]
    dh = w // MEM_HEADS
    nblk = seq // tm
    const = lambda i: (0, 0)
    return pl.pallas_call(
        functools.partial(_memory_kernel, dh=dh),
        grid=(t // tm,),
        in_specs=[
            pl.BlockSpec((tm, d), lambda i: (i, 0)),
            pl.BlockSpec((1, d), const),
            pl.BlockSpec((d, w), const),
            pl.BlockSpec((1, dh), const),
            pl.BlockSpec((mem_len, w), lambda i: (i // nblk, 0)),
            pl.BlockSpec((mem_len, w), lambda i: (i // nblk, 1)),
            pl.BlockSpec((w, d), const),
            pl.BlockSpec((1, d), const),
            pl.BlockSpec((d, LANES), const),
            pl.BlockSpec((1, LANES), const),
        ],
        out_specs=[pl.BlockSpec((tm, d), lambda i: (i, 0)),
                   pl.BlockSpec((tm, LANES), lambda i: (i, 0)),
                   pl.BlockSpec((tm, d // 2), lambda i: (i, 0))],
        out_shape=[jax.ShapeDtypeStruct((t, d), F32), jax.ShapeDtypeStruct((t, LANES), F32),
                   jax.ShapeDtypeStruct((t, d // 2), jnp.uint32)],
        compiler_params=_params(("arbitrary",)),
        name="memory_block",
    )(x1, mem_norm_g, wq, gq, kv, kv, wo, moe_norm_g, wr_pad, br_pad)


def _route_kernel(rl_ref, route_ref, counts_ref, carry_ref, *, tr, n_experts):
    @pl.when(pl.program_id(0) == 0)
    def _():
        carry_ref[...] = jnp.zeros_like(carry_ref)

    lane = lax.broadcasted_iota(jnp.int32, (tr, LANES), 1)
    lg = jnp.where(lane < n_experts, rl_ref[...], -jnp.inf)
    vals, hots = [], []
    for _ in range(TOP_K):
        mx = jnp.max(lg, axis=-1, keepdims=True)
        idx = jnp.min(jnp.where(lg == mx, lane, LANES), axis=-1, keepdims=True)
        hot = lane == idx
        vals.append(mx)
        hots.append(hot)
        lg = jnp.where(hot, -jnp.inf, lg)
    exps = [jnp.exp(v - vals[0]) for v in vals]
    denom = exps[0]
    for e in exps[1:]:
        denom = denom + e
    onehot = hots[0]
    for hsel in hots[1:]:
        onehot = jnp.logical_or(onehot, hsel)
    onehot_f = onehot.astype(F32)
    row = lax.broadcasted_iota(jnp.int32, (tr, tr), 0)
    col = lax.broadcasted_iota(jnp.int32, (tr, tr), 1)
    strict = (row > col).astype(BF16)
    before = jnp.dot(strict, onehot_f.astype(BF16), preferred_element_type=F32) + carry_ref[...]
    route = jnp.zeros((tr, LANES), F32)
    for k in range(TOP_K):
        idx_f = jnp.sum(jnp.where(hots[k], lane, 0), axis=-1, keepdims=True).astype(F32)
        rank = jnp.sum(jnp.where(hots[k], before, 0.0), axis=-1, keepdims=True)
        route = jnp.where(lane == k, idx_f, route)
        route = jnp.where(lane == TOP_K + k, exps[k] / denom, route)
        route = jnp.where(lane == 2 * TOP_K + k, rank, route)
    route_ref[...] = route
    total = carry_ref[...] + jnp.sum(onehot_f, axis=0, keepdims=True)
    carry_ref[...] = total
    counts_ref[...] = jnp.broadcast_to(total, counts_ref.shape)


def _route(rlogits, *, n_experts, tr):
    t = rlogits.shape[0]
    return pl.pallas_call(
        functools.partial(_route_kernel, tr=tr, n_experts=n_experts),
        grid=(t // tr,),
        in_specs=[pl.BlockSpec((tr, LANES), lambda i: (i, 0))],
        out_specs=[pl.BlockSpec((tr, LANES), lambda i: (i, 0)),
                   pl.BlockSpec((8, LANES), lambda i: (0, 0))],
        out_shape=[jax.ShapeDtypeStruct((t, LANES), F32), jax.ShapeDtypeStruct((8, LANES), F32)],
        scratch_shapes=[pltpu.VMEM((1, LANES), F32)],
        compiler_params=_params(("arbitrary",)),
        name="route_topk",
    )(rlogits)


def _gather_kernel(tok_ref, nused_ref, x_hbm, out_ref, buf, sem, *, rows):
    i = pl.program_id(0)
    nused = nused_ref[0]

    def issue(blk, slot):
        def body(r8, carry):
            for j in range(SUBLANES):
                tok = tok_ref[blk * rows + r8 * SUBLANES + j]
                pltpu.make_async_copy(
                    x_hbm.at[tok >> 3, pl.ds(tok & (SUBLANES - 1), 1)],
                    buf.at[slot, r8, pl.ds(j, 1)], sem.at[slot]).start()
            return carry
        lax.fori_loop(0, rows // SUBLANES, body, 0)

    @pl.when(i == 0)
    def _():
        issue(0, 0)

    @pl.when(i + 1 < nused)
    def _():
        issue(i + 1, (i + 1) % 2)

    @pl.when(i < nused)
    def _():
        slot = i % 2
        pltpu.make_async_copy(x_hbm.at[pl.ds(0, rows // SUBLANES)], buf.at[slot],
                              sem.at[slot]).wait()
        lo, hi = _unpack_bf16_pairs(buf[slot].reshape(rows, -1))
        half = lo.shape[1]
        out_ref[:, :half] = lo.astype(out_ref.dtype)
        out_ref[:, half:] = hi.astype(out_ref.dtype)

    @pl.when(i >= nused)
    def _():
        out_ref[...] = jnp.zeros_like(out_ref)


def _gather_rows(slot_tok, nused, xs_packed, *, rows):
    p = slot_tok.shape[0]
    half = xs_packed.shape[1]
    grid_spec = pltpu.PrefetchScalarGridSpec(
        num_scalar_prefetch=2,
        grid=(p // rows,),
        in_specs=[pl.BlockSpec(memory_space=pl.ANY)],
        out_specs=pl.BlockSpec((rows, 2 * half), lambda i, tok, nu: (i, 0)),
        scratch_shapes=[pltpu.VMEM((2, rows // SUBLANES, SUBLANES, half), jnp.uint32),
                        pltpu.SemaphoreType.DMA((2,))],
    )
    return pl.pallas_call(
        functools.partial(_gather_kernel, rows=rows),
        grid_spec=grid_spec,
        out_shape=jax.ShapeDtypeStruct((p, 2 * half), BF16),
        compiler_params=_params(("arbitrary",)),
        name="moe_gather",
    )(slot_tok, nused, xs_packed.reshape(-1, SUBLANES, half))


def _start_all(copies):
    for c in copies:
        c.start()


def _wait_all(copies):
    for c in copies:
        c.wait()


def _weight_pipeline(step, flag, tabs, copies_for, convert):
    exp_ref, tile_ref, gslot_ref, nexp_ref, ntile_ref = tabs

    @pl.when((flag & 2) != 0)
    def _():
        slot = gslot_ref[step]
        mine = copies_for(exp_ref[step], tile_ref[step], slot)

        @pl.when(step == 0)
        def _():
            _start_all(mine)

        _wait_all(mine)
        convert(slot)

        @pl.when((flag & 4) != 0)
        def _():
            _start_all(copies_for(nexp_ref[step], ntile_ref[step], 1 - slot))


def _expert_up_kernel(blk_ref, tile_ref, exp_ref, flag_ref, gslot_ref, nexp_ref, ntile_ref,
                      x_ref, w_hbm, bg_ref, bl_ref, out_ref, wbuf, wg_bf, wl_bf, sem, *, tf, nt):
    step = pl.program_id(0)
    flag = flag_ref[step]

    def copies_for(e, tile, slot):
        return [pltpu.make_async_copy(
            w_hbm.at[e, :, pl.ds(pl.multiple_of((half * nt + tile) * tf, tf), tf)],
            wbuf.at[slot, half], sem.at[slot, half]) for half in range(2)]

    def convert(slot):
        wg_bf[...] = wbuf[slot, 0].astype(BF16)
        wl_bf[...] = wbuf[slot, 1].astype(BF16)

    _weight_pipeline(step, flag, (exp_ref, tile_ref, gslot_ref, nexp_ref, ntile_ref), copies_for,
                     convert)

    @pl.when((flag & 1) != 0)
    def _():
        x = x_ref[...]
        glu = jnp.dot(x, wg_bf[...], preferred_element_type=F32) + bg_ref[...]
        lin = jnp.dot(x, wl_bf[...], preferred_element_type=F32) + bl_ref[...]
        glu = jnp.minimum(glu, SWIGLU_LIMIT)
        lin = jnp.clip(lin, -SWIGLU_LIMIT, SWIGLU_LIMIT)
        out_ref[...] = (glu * jax.nn.sigmoid(SWIGLU_ALPHA * glu) * (lin + 1.0)).astype(out_ref.dtype)

    @pl.when((flag & 1) == 0)
    def _():
        out_ref[...] = jnp.zeros_like(out_ref)


def _expert_up(tables, xg, w_up, b_up, *, tm, tf):
    p, d = xg.shape
    n_e, _, ff2 = w_up.shape
    ff = ff2 // 2
    nt = ff // tf
    nsteps = tables[0].shape[0]
    bias = lambda half: pl.BlockSpec((None, 1, tf),
                                     lambda s, blk, tile, ex, *_: (ex[s], 0, half * nt + tile[s]))
    grid_spec = pltpu.PrefetchScalarGridSpec(
        num_scalar_prefetch=len(tables),
        grid=(nsteps,),
        in_specs=[
            pl.BlockSpec((tm, d), lambda s, blk, *_: (blk[s], 0)),
            pl.BlockSpec(memory_space=pl.ANY),
            bias(0), bias(1),
        ],
        out_specs=pl.BlockSpec((tm, tf), lambda s, blk, tile, *_: (blk[s], tile[s])),
        scratch_shapes=[pltpu.VMEM((2, 2, d, tf), F32), pltpu.VMEM((d, tf), BF16),
                        pltpu.VMEM((d, tf), BF16), pltpu.SemaphoreType.DMA((2, 2))],
    )
    return pl.pallas_call(
        functools.partial(_expert_up_kernel, tf=tf, nt=nt),
        grid_spec=grid_spec,
        out_shape=jax.ShapeDtypeStruct((p, ff), BF16),
        compiler_params=_params(("arbitrary",), VMEM_LIMIT_MAX),
        name="expert_up",
    )(*tables, xg, w_up, b_up, b_up)


def _expert_down_kernel(blk_ref, tile_ref, exp_ref, flag_ref, gslot_ref, nexp_ref, ntile_ref,
                        a_ref, w_hbm, b_ref, out_ref, wbuf, w_bf, sem, *, tn):
    step = pl.program_id(0)
    flag = flag_ref[step]

    def copies_for(e, tile, slot):
        return [pltpu.make_async_copy(w_hbm.at[e, :, pl.ds(pl.multiple_of(tile * tn, tn), tn)],
                                      wbuf.at[slot], sem.at[slot])]

    def convert(slot):
        w_bf[...] = wbuf[slot].astype(BF16)

    _weight_pipeline(step, flag, (exp_ref, tile_ref, gslot_ref, nexp_ref, ntile_ref), copies_for,
                     convert)

    @pl.when((flag & 1) != 0)
    def _():
        y = jnp.dot(a_ref[...], w_bf[...], preferred_element_type=F32) + b_ref[...]
        out_ref[...] = _pack_bf16_pairs(y)

    @pl.when((flag & 1) == 0)
    def _():
        out_ref[...] = jnp.zeros_like(out_ref)


def _expert_down(tables, act, w_down, b_down, *, tm, tn):
    p, ff = act.shape
    d = w_down.shape[2]
    nsteps = tables[0].shape[0]
    grid_spec = pltpu.PrefetchScalarGridSpec(
        num_scalar_prefetch=len(tables),
        grid=(nsteps,),
        in_specs=[
            pl.BlockSpec((tm, ff), lambda s, blk, *_: (blk[s], 0)),
            pl.BlockSpec(memory_space=pl.ANY),
            pl.BlockSpec((None, 1, tn), lambda s, blk, tile, ex, *_: (ex[s], 0, tile[s])),
        ],
        out_specs=pl.BlockSpec((tm, tn // 2), lambda s, blk, tile, *_: (blk[s], tile[s])),
        scratch_shapes=[pltpu.VMEM((2, ff, tn), F32), pltpu.VMEM((ff, tn), BF16),
                        pltpu.SemaphoreType.DMA((2,))],
    )
    return pl.pallas_call(
        functools.partial(_expert_down_kernel, tn=tn),
        grid_spec=grid_spec,
        out_shape=jax.ShapeDtypeStruct((p, d // 2), jnp.uint32),
        compiler_params=_params(("arbitrary",)),
        name="expert_down",
    )(*tables, act, w_down, b_down)


def _expert_schedule(counts, *, tm, n_tiles, n_blocks):
    n_e = counts.shape[0]
    nb = (counts + tm - 1) // tm
    bend = jnp.cumsum(nb)
    bstart = bend - nb
    steps_e = nb * n_tiles
    cs = jnp.cumsum(steps_e)
    total = cs[-1]
    s = jnp.arange(n_blocks * n_tiles, dtype=jnp.int32)
    sc = jnp.minimum(s, total - 1)
    e = jnp.sum((cs[None, :] <= sc[:, None]).astype(jnp.int32), axis=1)
    e = jnp.minimum(e, n_e - 1)
    r = sc - (cs[e] - steps_e[e])
    nbe = jnp.maximum(nb[e], 1)
    valid = s < total
    n_unused = jnp.maximum(n_blocks - bend[-1], 1)
    u = jnp.maximum(s - total, 0)
    live_tile = r // nbe
    tile = jnp.where(valid, live_tile, u // n_unused)
    blk = jnp.where(valid, bstart[e] + r % nbe, bend[-1] + u % n_unused)
    first = jnp.logical_and(valid, r % nbe == 0)
    ids = jnp.where(nb > 0, jnp.arange(n_e, dtype=jnp.int32), n_e)
    later = jnp.concatenate([lax.cummin(ids, reverse=True)[1:], jnp.full((1,), n_e, jnp.int32)])
    wraps = live_tile + 1 >= n_tiles
    nexp = jnp.where(wraps, later[e], e)
    ntile = jnp.where(wraps, 0, live_tile + 1)
    flag = (valid.astype(jnp.int32) + 2 * first.astype(jnp.int32)
            + 4 * jnp.logical_and(first, nexp < n_e).astype(jnp.int32))
    gslot = (jnp.cumsum(first.astype(jnp.int32)) - 1) % 2
    i32 = lambda a: a.astype(jnp.int32)
    return (i32(blk), i32(tile), i32(e), flag, i32(gslot), i32(jnp.minimum(nexp, n_e - 1)),
            i32(ntile))


def _combine_kernel(slot_ref, x_ref, route_ref, y_hbm, out_ref, buf, sem, *, tc, tn):
    i = pl.program_id(0)
    n = pl.num_programs(0)

    def issue(blk, bslot):
        def body(r, carry):
            for k in range(TOP_K):
                src = slot_ref[(blk * tc + r) * TOP_K + k]
                pltpu.make_async_copy(y_hbm.at[pl.ds(src, 1)], buf.at[bslot, k, pl.ds(r, 1)],
                                      sem.at[bslot]).start()
            return carry
        lax.fori_loop(0, tc, body, 0, unroll=4)

    @pl.when(i == 0)
    def _():
        issue(0, 0)

    @pl.when(i + 1 < n)
    def _():
        issue(i + 1, (i + 1) % 2)

    bslot = i % 2
    for k in range(TOP_K):
        pltpu.make_async_copy(y_hbm.at[pl.ds(0, tc)], buf.at[bslot, k], sem.at[bslot]).wait()
    gates = [route_ref[:, TOP_K + k:TOP_K + k + 1] for k in range(TOP_K)]
    hw = tn // 2
    for j in range(out_ref.shape[1] // tn):
        parts = [_unpack_bf16_pairs(buf[bslot, k, :, j * hw:(j + 1) * hw]) for k in range(TOP_K)]
        for half in range(2):
            cols = slice(j * tn + half * hw, j * tn + (half + 1) * hw)
            acc = x_ref[:, cols]
            for k in range(TOP_K):
                acc = acc + gates[k] * parts[k][half]
            out_ref[:, cols] = acc


def _combine(slot_flat, x2, route, y, *, tc, tn):
    t, d = x2.shape
    grid_spec = pltpu.PrefetchScalarGridSpec(
        num_scalar_prefetch=1,
        grid=(t // tc,),
        in_specs=[pl.BlockSpec((tc, d), lambda i, sl: (i, 0)),
                  pl.BlockSpec((tc, LANES), lambda i, sl: (i, 0)),
                  pl.BlockSpec(memory_space=pl.ANY)],
        out_specs=pl.BlockSpec((tc, d), lambda i, sl: (i, 0)),
        scratch_shapes=[pltpu.VMEM((2, TOP_K, tc, d // 2), jnp.uint32),
                        pltpu.SemaphoreType.DMA((2,))],
    )
    return pl.pallas_call(
        functools.partial(_combine_kernel, tc=tc, tn=tn),
        grid_spec=grid_spec,
        out_shape=jax.ShapeDtypeStruct((t, d), F32),
        compiler_params=_params(("arbitrary",)),
        name="moe_combine",
    )(slot_flat, x2, route, y)


def _pad_lanes(a, offset=0):
    width = a.shape[-1]
    return jnp.pad(a, [(0, 0)] * (a.ndim - 1) + [(offset, LANES - offset - width)])


def _layer(x2d, mem2d, lw, w_in_stack, layer, *, batch, seq, mem_len):
    t, d = x2d.shape
    gla_qk = lw["gla_alpha_up"].shape[1]
    gla_rank = lw["gla_alpha_up"].shape[0]
    gla_v = lw["w_branch_gla"].shape[0]
    fox_w = lw["w_branch_fox"].shape[0]
    dk, dv = gla_qk // GLA_HEADS, gla_v // GLA_HEADS
    dh = fox_w // FOX_HEADS
    mem_w = lw["mem_w_q"].shape[1]
    mem_dh = mem_w // MEM_HEADS
    n_experts = lw["w_router"].shape[1]
    ff = lw["w_expert_down"].shape[1]

    sizes = (gla_qk, gla_qk, gla_v, gla_v, gla_rank, fox_w, fox_w, fox_w, FOX_HEADS, d, d)
    starts = [0]
    for sz in sizes:
        starts.append(starts[-1] + sz)
    big_ids = (0, 1, 2, 3, 5, 6, 7, 9, 10)
    off = {}
    acc = 0
    for i in big_ids:
        off[i] = acc
        acc += sizes[i]
    n_big = acc
    w_big, w_small = _w_in_pack(
        w_in_stack, layer, spans=tuple((starts[i], sizes[i], off[i]) for i in big_ids),
        pieces=((starts[4], sizes[4], 0), (starts[8], sizes[8], sizes[4])), tr=TILE["w_pack_rows"])
    tn1 = TILE["proj_tn"]
    colscale = jnp.ones((1, n_big), F32)
    colscale = colscale.at[:, off[5]:off[5] + fox_w].set(
        jnp.tile(lw["fox_q_norm_g"] * (dh ** -0.5 * LOG2E), FOX_HEADS)[None])
    colscale = colscale.at[:, off[6]:off[6] + fox_w].set(jnp.tile(lw["fox_k_norm_g"], FOX_HEADS)[None])
    proj, small = _norm_matmul(
        x2d, lw["mix_norm_g"][None], w_big, colscale, w_small,
        tm=TILE["proj_tm"], tn=tn1, norm_lo=off[5] // tn1, norm_hi=(off[6] + fox_w) // tn1)

    alpha_up_pad = jnp.pad(lw["gla_alpha_up"], ((0, LANES - gla_rank), (0, 0))).astype(BF16)
    o_gla = _gla(proj, small, alpha_up_pad, lw["gla_alpha_bias"][None], lw["gla_out_norm_g"][None],
                 batch=batch, seq=seq, dk=dk, dv=dv, offs=(off[0], off[1], off[2], off[3]),
                 rows=TILE["gla_rows"], hps=TILE["gla_heads_per_step"])

    c = _fcum(small, _pad_lanes(lw["fox_f_bias"][None], gla_rank), batch=batch, seq=seq,
              tb=TILE["fcum_rows"])
    c_row = c[:, gla_rank:gla_rank + FOX_HEADS].reshape(batch, seq, FOX_HEADS)
    c_row = c_row.transpose(0, 2, 1).reshape(batch * FOX_HEADS, 1, seq) * LOG2E
    o_fox = _fox(proj, c_row, batch=batch, seq=seq, dh=dh, offs=(off[5], off[6], off[7]),
                 tq=min(seq, TILE["fox_tq"]))

    merged = _merge(o_gla, o_fox, proj, lw["w_branch_gla"].astype(BF16),
                    lw["w_branch_fox"].astype(BF16), offs=(off[9], off[10]),
                    tm=TILE["merge_tm"], tn=TILE["merge_tn"])
    x1 = _resid_matmul(merged, lw["w_out"].astype(BF16), x2d, tm=TILE["resid_tm"],
                       tn=TILE["resid_tn"])

    kv_scale = jnp.concatenate([jnp.tile(lw["mem_k_norm_g"], MEM_HEADS),
                                jnp.ones((mem_w,), F32)])[None]
    kv = _norm_matmul(mem2d, lw["mem_kv_norm_g"][None], lw["mem_w_kv"].astype(BF16), kv_scale, None,
                      tm=mem2d.shape[0], tn=mem_w, norm_lo=0, norm_hi=1)
    x2, rlogits, xs_packed = _memory(
        x1, lw["mem_norm_g"][None], lw["mem_w_q"].astype(BF16),
        (lw["mem_q_norm_g"] * (mem_dh ** -0.5))[None], kv,
        lw["mem_w_o"].astype(BF16), lw["moe_norm_g"][None],
        _pad_lanes(lw["w_router"]).astype(BF16), _pad_lanes(lw["b_router"][None]),
        batch=batch, seq=seq, mem_len=mem_len, tm=TILE["mem_tm"])

    route, counts_f = _route(rlogits, n_experts=n_experts, tr=TILE["route_rows"])
    tm_e = TILE["expert_tm"]
    a_total = t * TOP_K
    n_blocks = a_total // tm_e + n_experts
    p_rows = n_blocks * tm_e
    counts = counts_f[0, :n_experts].astype(jnp.int32)
    padded = (counts + tm_e - 1) // tm_e * tm_e
    pstart = jnp.cumsum(padded) - padded
    idx = route[:, :TOP_K].astype(jnp.int32)
    rank = route[:, 2 * TOP_K:3 * TOP_K].astype(jnp.int32)
    hit = idx[:, :, None] == jnp.arange(n_experts, dtype=jnp.int32)
    slot = (jnp.sum(jnp.where(hit, pstart, 0), axis=-1) + rank).reshape(a_total)
    slot_tok = (jnp.arange(p_rows, dtype=jnp.int32) % t).at[slot].set(
        jnp.arange(a_total, dtype=jnp.int32) // TOP_K, unique_indices=True,
        mode="promise_in_bounds")

    tf = TILE["expert_tf"]
    g_rows = TILE["gather_rows"]
    nused = (jnp.sum(padded) // g_rows).astype(jnp.int32).reshape(1)
    xg = _gather_rows(slot_tok, nused, xs_packed, rows=g_rows)
    up_tables = _expert_schedule(counts, tm=tm_e, n_tiles=ff // tf, n_blocks=n_blocks)
    act = _expert_up(up_tables, xg, lw["w_expert_up"], lw["b_expert_up"][:, None, :], tm=tm_e, tf=tf)
    tn_d = TILE["expert_tn"]
    down_tables = _expert_schedule(counts, tm=tm_e, n_tiles=d // tn_d, n_blocks=n_blocks)
    y = _expert_down(down_tables, act, lw["w_expert_down"], lw["b_expert_down"][:, None, :],
                     tm=tm_e, tn=tn_d)
    return _combine(slot, x2, route, y, tc=TILE["combine_rows"], tn=tn_d)


def kernel(x, mem, mix_norm_g, w_in, gla_alpha_up, gla_alpha_bias, gla_out_norm_g, fox_f_bias, fox_q_norm_g, fox_k_norm_g, w_branch_gla, w_branch_fox, w_out, mem_norm_g, mem_kv_norm_g, mem_w_q, mem_w_kv, mem_q_norm_g, mem_k_norm_g, mem_w_o, moe_norm_g, w_router, b_router, w_expert_up, b_expert_up, w_expert_down, b_expert_down):
    weights = dict(
        mix_norm_g=mix_norm_g, w_in=w_in, gla_alpha_up=gla_alpha_up, gla_alpha_bias=gla_alpha_bias,
        gla_out_norm_g=gla_out_norm_g, fox_f_bias=fox_f_bias, fox_q_norm_g=fox_q_norm_g,
        fox_k_norm_g=fox_k_norm_g, w_branch_gla=w_branch_gla, w_branch_fox=w_branch_fox, w_out=w_out,
        mem_norm_g=mem_norm_g, mem_kv_norm_g=mem_kv_norm_g, mem_w_q=mem_w_q, mem_w_kv=mem_w_kv,
        mem_q_norm_g=mem_q_norm_g, mem_k_norm_g=mem_k_norm_g, mem_w_o=mem_w_o, moe_norm_g=moe_norm_g,
        w_router=w_router, b_router=b_router, w_expert_up=w_expert_up, b_expert_up=b_expert_up,
        w_expert_down=w_expert_down, b_expert_down=b_expert_down)
    batch, seq, d = x.shape
    mem_len = mem.shape[1]
    x2d = x.reshape(batch * seq, d)
    mem2d = mem.reshape(batch * mem_len, d)
    for layer in range(mix_norm_g.shape[0]):
        lw = {name: w[layer] for name, w in weights.items() if name != "w_in"}
        x2d = _layer(x2d, mem2d, lw, w_in, layer, batch=batch, seq=seq, mem_len=mem_len)
    return x2d.reshape(batch, seq, d)
```

```python
import functools

import jax
import jax.numpy as jnp
from jax import lax
from jax.experimental import pallas as pl
from jax.experimental.pallas import tpu as pltpu

F32 = jnp.float32
BF16 = jnp.bfloat16

EPS = 1e-6
LANES = 128
SUBLANES = 8
VMEM_LIMIT = 56 * 1024 * 1024
VMEM_LIMIT_MAX = 61 * 1024 * 1024

GLA_HEADS = 4
GLA_GATE_NORM = 16.0
GLA_CHUNK = 64
GLA_SUB = 16
FOX_HEADS = 16
FOX_STRIP = 16
FOX_GROUP = 128
LOG2E = 1.4426950408889634
MEM_HEADS = 4
TOP_K = 4
SWIGLU_LIMIT = 7.0
SWIGLU_ALPHA = 1.702

TILE = dict(
    w_pack_rows=128,
    proj_tm=1024, proj_tn=1024,
    gla_rows=256, gla_heads_per_step=4,
    fcum_rows=512,
    fox_tq=2048,
    merge_tm=512, merge_tn=1024,
    resid_tm=1024, resid_tn=1024,
    mem_tm=512,
    route_rows=512,
    expert_tm=512, expert_tf=1024, expert_tn=1024,
    gather_rows=512, combine_rows=256,
)

NT_DIMS = (((1,), (1,)), ((), ()))
TN_DIMS = (((0,), (0,)), ((), ()))


def _params(sem, vmem=VMEM_LIMIT):
    return pltpu.CompilerParams(dimension_semantics=sem, vmem_limit_bytes=vmem)


def _rms(xf, g):
    return xf * lax.rsqrt(jnp.mean(xf * xf, axis=-1, keepdims=True) + EPS) * g


def _pack_bf16_pairs(a):
    half = a.shape[1] // 2
    bits = lambda v: lax.bitcast_convert_type(v.astype(BF16).astype(F32), jnp.uint32)
    return (bits(a[:, half:]) & jnp.uint32(0xFFFF0000)) | (bits(a[:, :half]) >> 16)


def _unpack_bf16_pairs(u):
    lo = lax.bitcast_convert_type(u << 16, F32)
    hi = lax.bitcast_convert_type(u & jnp.uint32(0xFFFF0000), F32)
    return lo, hi


def _log_sigmoid(z):
    return jnp.minimum(z, 0.0) - jnp.log1p(jnp.exp(-jnp.abs(z)))


def _w_in_pack_kernel(w_ref, tail_ref, big_ref, small_ref, *, spans, pieces):
    rows, n_cols = w_ref.shape
    lane = lax.broadcasted_iota(jnp.int32, (rows, LANES), 1)
    tile_at = lambda c0: w_ref[:, c0:c0 + LANES]
    for s0, width, d0 in spans:
        sh = s0 % LANES
        a0 = s0 - sh
        for t in range(width // LANES):
            dst = slice(d0 + t * LANES, d0 + (t + 1) * LANES)
            if sh == 0:
                big_ref[:, dst] = tile_at(a0 + t * LANES).astype(BF16)
            elif a0 + (t + 2) * LANES <= n_cols:
                left = pltpu.roll(tile_at(a0 + t * LANES), LANES - sh, 1)
                right = pltpu.roll(tile_at(a0 + (t + 1) * LANES), LANES - sh, 1)
                big_ref[:, dst] = jnp.where(lane < LANES - sh, left, right).astype(BF16)
            else:
                assert s0 + (t + 1) * LANES == n_cols
                big_ref[:, dst] = tail_ref[...].astype(BF16)
    side = jnp.zeros((rows, LANES), F32)
    for s0, width, dl in pieces:
        sh = s0 % LANES
        tile = tile_at(s0 - sh)
        if (dl - sh) % LANES:
            tile = pltpu.roll(tile, (dl - sh) % LANES, 1)
        side = jnp.where(jnp.logical_and(lane >= dl, lane < dl + width), tile, side)
    small_ref[...] = side.astype(BF16)


def _w_in_pack(w_stack, layer, *, spans, pieces, tr):
    _, d, n_cols = w_stack.shape
    n_big = sum(width for _, width, _ in spans)
    w_big, w_small = pl.pallas_call(
        functools.partial(_w_in_pack_kernel, spans=spans, pieces=pieces),
        grid=(d // tr,),
        in_specs=[pl.BlockSpec((None, tr, n_cols), lambda i: (layer, i, 0)),
                  pl.BlockSpec((tr, LANES), lambda i: (i, 0))],
        out_specs=[pl.BlockSpec((tr, n_big), lambda i: (i, 0)),
                   pl.BlockSpec((tr, LANES), lambda i: (i, 0))],
        out_shape=[jax.ShapeDtypeStruct((d, n_big), BF16), jax.ShapeDtypeStruct((d, LANES), BF16)],
        compiler_params=_params(("arbitrary",)),
        name="w_in_pack",
    )(w_stack, w_stack[layer, :, n_cols - LANES:])
    return w_big, w_small


def _norm_matmul_kernel(*refs, norm_lo, norm_hi, tn, has_small):
    if has_small:
        x_ref, g_ref, w_ref, cs_ref, ws_ref, out_ref, small_ref, h_ref = refs
    else:
        x_ref, g_ref, w_ref, cs_ref, out_ref, h_ref = refs
    j = pl.program_id(1)

    @pl.when(j == 0)
    def _():
        hb = _rms(x_ref[...], g_ref[...]).astype(BF16)
        h_ref[...] = hb
        if has_small:
            small_ref[...] = jnp.dot(hb, ws_ref[...], preferred_element_type=F32)

    acc = jnp.dot(h_ref[...], w_ref[...], preferred_element_type=F32)
    is_norm = jnp.logical_and(j >= norm_lo, j < norm_hi)

    @pl.when(is_norm)
    def _():
        for c in range(tn // LANES):
            sl = slice(c * LANES, (c + 1) * LANES)
            out_ref[:, sl] = _rms(acc[:, sl], cs_ref[:, sl]).astype(out_ref.dtype)

    @pl.when(jnp.logical_not(is_norm))
    def _():
        out_ref[...] = acc.astype(out_ref.dtype)


def _norm_matmul(x, g, w, colscale, w_small, *, tm, tn, norm_lo, norm_hi):
    m, d = x.shape
    n = w.shape[1]
    has_small = w_small is not None
    in_specs = [
        pl.BlockSpec((tm, d), lambda i, j: (i, 0)),
        pl.BlockSpec((1, d), lambda i, j: (0, 0)),
        pl.BlockSpec((d, tn), lambda i, j: (0, j)),
        pl.BlockSpec((1, tn), lambda i, j: (0, j)),
    ]
    args = [x, g, w, colscale]
    out_shape = [jax.ShapeDtypeStruct((m, n), BF16)]
    out_specs = [pl.BlockSpec((tm, tn), lambda i, j: (i, j))]
    if has_small:
        in_specs.append(pl.BlockSpec((d, LANES), lambda i, j: (0, 0)))
        args.append(w_small)
        out_shape.append(jax.ShapeDtypeStruct((m, LANES), F32))
        out_specs.append(pl.BlockSpec((tm, LANES), lambda i, j: (i, 0)))
    res = pl.pallas_call(
        functools.partial(_norm_matmul_kernel, norm_lo=norm_lo, norm_hi=norm_hi, tn=tn,
                          has_small=has_small),
        grid=(m // tm, n // tn),
        in_specs=in_specs,
        out_specs=out_specs,
        out_shape=out_shape,
        scratch_shapes=[pltpu.VMEM((tm, d), BF16)],
        compiler_params=_params(("arbitrary", "arbitrary")),
        name="norm_matmul",
    )(*args)
    return res if has_small else res[0]


def _prefix_sum_rows(tri_bf, x):
    p1 = x.astype(BF16)
    r1 = x - p1.astype(F32)
    p2 = r1.astype(BF16)
    p3 = (r1 - p2.astype(F32)).astype(BF16)
    dot = lambda p: jnp.dot(tri_bf, p, preferred_element_type=F32)
    return (dot(p1) + dot(p2)) + dot(p3)


def _gla_kernel(q_ref, k_ref, v_ref, r_ref, sm_ref, au_ref, ab_ref, gn_ref, out_ref, st_ref,
                *, chunk, nsub, scale, hps, dk, dv):
    @pl.when(pl.program_id(2) == 0)
    def _():
        st_ref[...] = jnp.zeros_like(st_ref)

    row = lax.broadcasted_iota(jnp.int32, (chunk, chunk), 0)
    col = lax.broadcasted_iota(jnp.int32, (chunk, chunk), 1)
    causal = row >= col
    tri = causal.astype(BF16)
    key_row = lax.broadcasted_iota(jnp.int32, (chunk, 1), 0)
    for s in range(nsub):
        rows = pl.ds(s * chunk, chunk)
        ga = sm_ref[rows, :].astype(BF16)
        for hh in range(hps):
            kc = slice(hh * dk, (hh + 1) * dk)
            vc = slice(hh * dv, (hh + 1) * dv)
            z = jnp.dot(ga, au_ref[:, kc], preferred_element_type=F32)
            la = _log_sigmoid(z + ab_ref[:, kc]) * (1.0 / GLA_GATE_NORM)
            b = _prefix_sum_rows(tri, la)
            b_last = b[chunk - 1:chunk, :]
            q = q_ref[rows, kc].astype(F32)
            k = k_ref[rows, kc].astype(F32)
            v = v_ref[rows, vc]
            qd = (q * (scale * jnp.exp(b))).astype(BF16)
            kl = (k * jnp.exp(b_last - b)).astype(BF16)
            blocks = []
            for gi in range(chunk // GLA_SUB):
                grp = slice(gi * GLA_SUB, (gi + 1) * GLA_SUB)
                if gi == 0:
                    q_fac, k_arg = b[grp], -b
                else:
                    r = b[gi * GLA_SUB - 1:gi * GLA_SUB, :]
                    q_fac, k_arg = b[grp] - r, r - b
                k_arg = jnp.where(key_row < (gi + 1) * GLA_SUB, k_arg, -jnp.inf)
                qg = (q[grp] * (scale * jnp.exp(q_fac))).astype(BF16)
                kg = (k * jnp.exp(k_arg)).astype(BF16)
                blocks.append(lax.dot_general(qg, kg, NT_DIMS, preferred_element_type=F32))
            sc = jnp.where(causal, jnp.concatenate(blocks, axis=0), 0.0).astype(BF16)
            st = st_ref[hh]
            o = jnp.dot(sc, v, preferred_element_type=F32) + lax.dot_general(
                qd, st.astype(BF16), NT_DIMS, preferred_element_type=F32)
            st_ref[hh] = st * jnp.exp(b_last) + lax.dot_general(
                v, kl, TN_DIMS, preferred_element_type=F32)
            r = r_ref[rows, vc].astype(F32)
            out_ref[rows, vc] = (_rms(o, gn_ref[...]) * (r * jax.nn.sigmoid(r))).astype(out_ref.dtype)


def _gla(proj, small, alpha_up_pad, alpha_bias, out_norm_g, *, batch, seq, dk, dv, offs, rows, hps):
    t = batch * seq
    nblk = seq // rows
    h = GLA_HEADS
    wk, wv = hps * dk, hps * dv
    qo, ko, vo, ro = (offs[0] // wk, offs[1] // wk, offs[2] // wv, offs[3] // wv)
    rmap = lambda b, hh, c: b * nblk + c
    return pl.pallas_call(
        functools.partial(_gla_kernel, chunk=GLA_CHUNK, nsub=rows // GLA_CHUNK, scale=dk ** -0.5,
                          hps=hps, dk=dk, dv=dv),
        grid=(batch, h // hps, nblk),
        in_specs=[
            pl.BlockSpec((rows, wk), lambda b, hh, c: (rmap(b, hh, c), qo + hh)),
            pl.BlockSpec((rows, wk), lambda b, hh, c: (rmap(b, hh, c), ko + hh)),
            pl.BlockSpec((rows, wv), lambda b, hh, c: (rmap(b, hh, c), vo + hh)),
            pl.BlockSpec((rows, wv), lambda b, hh, c: (rmap(b, hh, c), ro + hh)),
            pl.BlockSpec((rows, LANES), lambda b, hh, c: (rmap(b, hh, c), 0)),
            pl.BlockSpec((LANES, wk), lambda b, hh, c: (0, hh)),
            pl.BlockSpec((1, wk), lambda b, hh, c: (0, hh)),
            pl.BlockSpec((1, dv), lambda b, hh, c: (0, 0)),
        ],
        out_specs=pl.BlockSpec((rows, wv), lambda b, hh, c: (rmap(b, hh, c), hh)),
        out_shape=jax.ShapeDtypeStruct((t, h * dv), BF16),
        scratch_shapes=[pltpu.VMEM((hps, dv, dk), F32)],
        compiler_params=_params(("arbitrary", "arbitrary", "arbitrary")),
        name="gla",
    )(proj, proj, proj, proj, small, alpha_up_pad, alpha_bias, out_norm_g)


def _fcum_kernel(sm_ref, bias_ref, out_ref, carry_ref, *, tb):
    @pl.when(pl.program_id(1) == 0)
    def _():
        carry_ref[...] = jnp.zeros_like(carry_ref)

    row = lax.broadcasted_iota(jnp.int32, (tb, tb), 0)
    col = lax.broadcasted_iota(jnp.int32, (tb, tb), 1)
    tri = (row >= col).astype(BF16)
    lf = _log_sigmoid(sm_ref[...] + bias_ref[...])
    c = _prefix_sum_rows(tri, lf) + carry_ref[...]
    out_ref[...] = c
    carry_ref[...] = c[tb - 1:tb, :]


def _fcum(small, bias_pad, *, batch, seq, tb):
    nb = seq // tb
    return pl.pallas_call(
        functools.partial(_fcum_kernel, tb=tb),
        grid=(batch, nb),
        in_specs=[pl.BlockSpec((tb, LANES), lambda b, i: (b * nb + i, 0)),
                  pl.BlockSpec((1, LANES), lambda b, i: (0, 0))],
        out_specs=pl.BlockSpec((tb, LANES), lambda b, i: (b * nb + i, 0)),
        out_shape=jax.ShapeDtypeStruct(small.shape, F32),
        scratch_shapes=[pltpu.VMEM((1, LANES), F32)],
        compiler_params=_params(("arbitrary", "arbitrary")),
        name="forget_cumsum",
    )(small, bias_pad)


def _fox_kernel(qt_ref, kt_ref, q_ref, k_ref, v_ref, c_ref, out_ref, m_ref, mn_ref, a_ref, l_ref,
                acc_ref, s_ref, p_ref, *, tq, rb, grp):
    p = pl.program_id(2)
    qi = qt_ref[p]
    ki = kt_ref[p]

    @pl.when(ki == 0)
    def _():
        m_ref[...] = jnp.full_like(m_ref, -jnp.inf)
        l_ref[...] = jnp.zeros_like(l_ref)
        acc_ref[...] = jnp.zeros_like(acc_ref)

    def update(diag):
        c = c_ref[0]

        def ncols(g):
            return (g + 1) * grp if diag else tq

        def qk(g):
            n = ncols(g)
            s_ref[g % 2, :, 0:n] = lax.dot_general(q_ref[g * grp:(g + 1) * grp, :], k_ref[0:n, :],
                                                   NT_DIMS, preferred_element_type=F32)

        def pv(g):
            n = ncols(g)
            rows = slice(g * grp, (g + 1) * grp)
            acc_ref[rows, :] += jnp.dot(p_ref[g % 2, :, 0:n], v_ref[0:n, :],
                                        preferred_element_type=F32)

        def softmax(g):
            buf = g % 2
            strips = []
            for st in range(grp // rb):
                r0 = g * grp + st * rb
                visible = (r0 + rb - 1) // LANES + 1 if diag else tq // LANES
                strips.append((slice(st * rb, (st + 1) * rb), slice(r0, r0 + rb), r0, visible))

            def logits(rows_l, r0, t):
                cols = slice(t * LANES, (t + 1) * LANES)
                s = s_ref[buf, rows_l, cols] - c[:, cols]
                if diag and (t + 1) * LANES - 1 > r0:
                    row = r0 + lax.broadcasted_iota(jnp.int32, (rb, LANES), 0)
                    col = t * LANES + lax.broadcasted_iota(jnp.int32, (rb, LANES), 1)
                    s = jnp.where(row >= col, s, -jnp.inf)
                return s

            for rows_l, rows, r0, visible in strips:
                part = logits(rows_l, r0, 0)
                for t in range(1, visible):
                    part = jnp.maximum(part, logits(rows_l, r0, t))
                m_old = m_ref[rows, :]
                m_new = jnp.maximum(m_old, jnp.max(part, axis=-1, keepdims=True))
                a_ref[rows, :] = jnp.exp2(m_old - m_new)
                mn_ref[rows, :] = m_new
            for rows_l, rows, r0, visible in strips:
                m_new = mn_ref[rows, :]
                m_ref[rows, :] = m_new
                part = jnp.zeros((rb, LANES), F32)
                for t in range(ncols(g) // LANES):
                    cols = slice(t * LANES, (t + 1) * LANES)
                    if t < visible:
                        pr = jnp.exp2(logits(rows_l, r0, t) - m_new)
                        part = part + pr
                        p_ref[buf, rows_l, cols] = pr.astype(BF16)
                    else:
                        p_ref[buf, rows_l, cols] = jnp.zeros((rb, LANES), BF16)
                alpha = a_ref[rows, :]
                l_ref[rows, :] = alpha * l_ref[rows, :] + part
                acc_ref[rows, :] = alpha * acc_ref[rows, :]

        ngrp = tq // grp
        qk(0)
        for g in range(ngrp):
            if g + 1 < ngrp:
                qk(g + 1)
            softmax(g)
            if g >= 1:
                pv(g - 1)
        pv(ngrp - 1)

    @pl.when(ki < qi)
    def _():
        update(False)

    @pl.when(ki == qi)
    def _():
        update(True)
        denom = jnp.sum(l_ref[...], axis=-1, keepdims=True)
        out_ref[...] = (acc_ref[...] / denom).astype(out_ref.dtype)


def _fox(proj, c_row, *, batch, seq, dh, offs, tq):
    t = batch * seq
    nq = seq // tq
    pairs = [(a, b) for a in range(nq) for b in range(a + 1)]
    qt = jnp.asarray([a for a, _ in pairs], jnp.int32)
    kt = jnp.asarray([b for _, b in pairs], jnp.int32)
    qo, ko, vo = (o // dh for o in offs)
    grp = min(tq, FOX_GROUP)
    grid_spec = pltpu.PrefetchScalarGridSpec(
        num_scalar_prefetch=2,
        grid=(batch, FOX_HEADS, len(pairs)),
        in_specs=[
            pl.BlockSpec((tq, dh), lambda b, h, p, qt, kt: (b * nq + qt[p], qo + h)),
            pl.BlockSpec((tq, dh), lambda b, h, p, qt, kt: (b * nq + kt[p], ko + h)),
            pl.BlockSpec((tq, dh), lambda b, h, p, qt, kt: (b * nq + kt[p], vo + h)),
            pl.BlockSpec((1, 1, tq), lambda b, h, p, qt, kt: (b * FOX_HEADS + h, 0, kt[p])),
        ],
        out_specs=pl.BlockSpec((tq, dh), lambda b, h, p, qt, kt: (b * nq + qt[p], h)),
        scratch_shapes=[pltpu.VMEM((tq, LANES), F32)] * 4 + [
                        pltpu.VMEM((tq, dh), F32), pltpu.VMEM((2, grp, tq), F32),
                        pltpu.VMEM((2, grp, tq), BF16)],
    )
    return pl.pallas_call(
        functools.partial(_fox_kernel, tq=tq, rb=FOX_STRIP, grp=grp),
        grid_spec=grid_spec,
        out_shape=jax.ShapeDtypeStruct((t, FOX_HEADS * dh), BF16),
        compiler_params=_params(("arbitrary", "arbitrary", "arbitrary")),
        name="fox_attention",
    )(qt, kt, proj, proj, proj, c_row)


def _merge_kernel(a_ref, f_ref, ga_ref, gf_ref, wa_ref, wf_ref, out_ref):
    ya = jnp.dot(a_ref[...], wa_ref[...], preferred_element_type=F32)
    yf = jnp.dot(f_ref[...], wf_ref[...], preferred_element_type=F32)
    out_ref[...] = (jax.nn.sigmoid(ga_ref[...].astype(F32)) * ya
                    + jax.nn.sigmoid(gf_ref[...].astype(F32)) * yf).astype(out_ref.dtype)


def _merge(o_gla, o_fox, proj, w_gla, w_fox, *, offs, tm, tn):
    m, kdim = o_gla.shape
    n = w_gla.shape[1]
    go, fo = offs[0] // tn, offs[1] // tn
    return pl.pallas_call(
        _merge_kernel,
        grid=(m // tm, n // tn),
        in_specs=[
            pl.BlockSpec((tm, kdim), lambda i, j: (i, 0)),
            pl.BlockSpec((tm, o_fox.shape[1]), lambda i, j: (i, 0)),
            pl.BlockSpec((tm, tn), lambda i, j: (i, go + j)),
            pl.BlockSpec((tm, tn), lambda i, j: (i, fo + j)),
            pl.BlockSpec((kdim, tn), lambda i, j: (0, j)),
            pl.BlockSpec((o_fox.shape[1], tn), lambda i, j: (0, j)),
        ],
        out_specs=pl.BlockSpec((tm, tn), lambda i, j: (i, j)),
        out_shape=jax.ShapeDtypeStruct((m, n), BF16),
        compiler_params=_params(("arbitrary", "arbitrary")),
        name="gated_merge",
    )(o_gla, o_fox, proj, proj, w_gla, w_fox)


def _resid_matmul_kernel(a_ref, w_ref, res_ref, out_ref):
    out_ref[...] = res_ref[...] + jnp.dot(a_ref[...], w_ref[...], preferred_element_type=F32)


def _resid_matmul(a, w, res, *, tm, tn):
    m, kdim = a.shape
    n = w.shape[1]
    return pl.pallas_call(
        _resid_matmul_kernel,
        grid=(m // tm, n // tn),
        in_specs=[pl.BlockSpec((tm, kdim), lambda i, j: (i, 0)),
                  pl.BlockSpec((kdim, tn), lambda i, j: (0, j)),
                  pl.BlockSpec((tm, tn), lambda i, j: (i, j))],
        out_specs=pl.BlockSpec((tm, tn), lambda i, j: (i, j)),
        out_shape=jax.ShapeDtypeStruct((m, n), F32),
        compiler_params=_params(("arbitrary", "arbitrary")),
        name="resid_matmul",
    )(a, w, res)


def _memory_kernel(x_ref, gn_ref, wq_ref, gq_ref, k_ref, v_ref, wo_ref, gm_ref, wr_ref, br_ref,
                   x2_ref, rl_ref, xsp_ref, *, dh):
    x1 = x_ref[...]
    hb = _rms(x1, gn_ref[...]).astype(BF16)
    q = jnp.dot(hb, wq_ref[...], preferred_element_type=F32)
    outs = []
    for h in range(MEM_HEADS):
        sl = slice(h * dh, (h + 1) * dh)
        qn = _rms(q[:, sl], gq_ref[...]).astype(BF16)
        s = lax.dot_general(qn, k_ref[:, sl], NT_DIMS, preferred_element_type=F32)
        s = s - jnp.max(s, axis=-1, keepdims=True)
        p = jnp.exp(s)
        p = p / jnp.sum(p, axis=-1, keepdims=True)
        outs.append(jnp.dot(p.astype(BF16), v_ref[:, sl], preferred_element_type=F32).astype(BF16))
    o = jnp.concatenate(outs, axis=-1)
    x2 = x1 + jnp.dot(o, wo_ref[...], preferred_element_type=F32)
    x2_ref[...] = x2
    xs = _rms(x2, gm_ref[...])
    rl_ref[...] = jnp.dot(xs.astype(BF16), wr_ref[...], preferred_element_type=F32) + br_ref[...]
    xsp_ref[...] = _pack_bf16_pairs(xs)


def _memory(x1, mem_norm_g, wq, gq, kv, wo, moe_norm_g, wr_pad, br_pad, *, batch, seq,
            mem_len, tm):
    t, d = x1.shape
    w = wq.shape[1]
    dh = w // MEM_HEADS
    nblk = seq // tm
    const = lambda i: (0, 0)
    return pl.pallas_call(
        functools.partial(_memory_kernel, dh=dh),
        grid=(t // tm,),
        in_specs=[
            pl.BlockSpec((tm, d), lambda i: (i, 0)),
            pl.BlockSpec((1, d), const),
            pl.BlockSpec((d, w), const),
            pl.BlockSpec((1, dh), const),
            pl.BlockSpec((mem_len, w), lambda i: (i // nblk, 0)),
            pl.BlockSpec((mem_len, w), lambda i: (i // nblk, 1)),
            pl.BlockSpec((w, d), const),
            pl.BlockSpec((1, d), const),
            pl.BlockSpec((d, LANES), const),
            pl.BlockSpec((1, LANES), const),
        ],
        out_specs=[pl.BlockSpec((tm, d), lambda i: (i, 0)),
                   pl.BlockSpec((tm, LANES), lambda i: (i, 0)),
                   pl.BlockSpec((tm, d // 2), lambda i: (i, 0))],
        out_shape=[jax.ShapeDtypeStruct((t, d), F32), jax.ShapeDtypeStruct((t, LANES), F32),
                   jax.ShapeDtypeStruct((t, d // 2), jnp.uint32)],
        compiler_params=_params(("arbitrary",)),
        name="memory_block",
    )(x1, mem_norm_g, wq, gq, kv, kv, wo, moe_norm_g, wr_pad, br_pad)


def _route_kernel(rl_ref, route_ref, counts_ref, carry_ref, *, tr, n_experts):
    @pl.when(pl.program_id(0) == 0)
    def _():
        carry_ref[...] = jnp.zeros_like(carry_ref)

    lane = lax.broadcasted_iota(jnp.int32, (tr, LANES), 1)
    lg = jnp.where(lane < n_experts, rl_ref[...], -jnp.inf)
    vals, hots = [], []
    for _ in range(TOP_K):
        mx = jnp.max(lg, axis=-1, keepdims=True)
        idx = jnp.min(jnp.where(lg == mx, lane, LANES), axis=-1, keepdims=True)
        hot = lane == idx
        vals.append(mx)
        hots.append(hot)
        lg = jnp.where(hot, -jnp.inf, lg)
    exps = [jnp.exp(v - vals[0]) for v in vals]
    denom = exps[0]
    for e in exps[1:]:
        denom = denom + e
    onehot = hots[0]
    for hsel in hots[1:]:
        onehot = jnp.logical_or(onehot, hsel)
    onehot_f = onehot.astype(F32)
    row = lax.broadcasted_iota(jnp.int32, (tr, tr), 0)
    col = lax.broadcasted_iota(jnp.int32, (tr, tr), 1)
    strict = (row > col).astype(BF16)
    before = jnp.dot(strict, onehot_f.astype(BF16), preferred_element_type=F32) + carry_ref[...]
    route = jnp.zeros((tr, LANES), F32)
    for k in range(TOP_K):
        idx_f = jnp.sum(jnp.where(hots[k], lane, 0), axis=-1, keepdims=True).astype(F32)
        rank = jnp.sum(jnp.where(hots[k], before, 0.0), axis=-1, keepdims=True)
        route = jnp.where(lane == k, idx_f, route)
        route = jnp.where(lane == TOP_K + k, exps[k] / denom, route)
        route = jnp.where(lane == 2 * TOP_K + k, rank, route)
    route_ref[...] = route
    total = carry_ref[...] + jnp.sum(onehot_f, axis=0, keepdims=True)
    carry_ref[...] = total
    counts_ref[...] = jnp.broadcast_to(total, counts_ref.shape)


def _route(rlogits, *, n_experts, tr):
    t = rlogits.shape[0]
    return pl.pallas_call(
        functools.partial(_route_kernel, tr=tr, n_experts=n_experts),
        grid=(t // tr,),
        in_specs=[pl.BlockSpec((tr, LANES), lambda i: (i, 0))],
        out_specs=[pl.BlockSpec((tr, LANES), lambda i: (i, 0)),
                   pl.BlockSpec((8, LANES), lambda i: (0, 0))],
        out_shape=[jax.ShapeDtypeStruct((t, LANES), F32), jax.ShapeDtypeStruct((8, LANES), F32)],
        scratch_shapes=[pltpu.VMEM((1, LANES), F32)],
        compiler_params=_params(("arbitrary",)),
        name="route_topk",
    )(rlogits)


def _slot_kernel(route_ref, pstart_ref, slot_ref):
    route = route_ref[...]
    lane = lax.broadcasted_iota(jnp.int32, route.shape, 1)
    lane_f = lane.astype(F32)
    out = jnp.zeros_like(route)
    for k in range(TOP_K):
        start = jnp.sum(jnp.where(lane_f == route[:, k:k + 1], pstart_ref[...], 0.0),
                        axis=-1, keepdims=True)
        out = jnp.where(lane == k, start + route[:, 2 * TOP_K + k:2 * TOP_K + k + 1], out)
    slot_ref[...] = out


def _slots(route, pstart_pad, *, tr):
    t = route.shape[0]
    return pl.pallas_call(
        _slot_kernel,
        grid=(t // tr,),
        in_specs=[pl.BlockSpec((tr, LANES), lambda i: (i, 0)),
                  pl.BlockSpec((1, LANES), lambda i: (0, 0))],
        out_specs=pl.BlockSpec((tr, LANES), lambda i: (i, 0)),
        out_shape=jax.ShapeDtypeStruct((t, LANES), F32),
        compiler_params=_params(("arbitrary",)),
        name="assign_slots",
    )(route, pstart_pad)


def _gather_kernel(tok_ref, nused_ref, x_hbm, out_ref, buf, sem, *, rows):
    i = pl.program_id(0)
    nused = nused_ref[0]

    def issue(blk, slot):
        def body(r8, carry):
            for j in range(SUBLANES):
                tok = tok_ref[blk * rows + r8 * SUBLANES + j]
                pltpu.make_async_copy(
                    x_hbm.at[tok >> 3, pl.ds(tok & (SUBLANES - 1), 1)],
                    buf.at[slot, r8, pl.ds(j, 1)], sem.at[slot]).start()
            return carry
        lax.fori_loop(0, rows // SUBLANES, body, 0)

    @pl.when(i == 0)
    def _():
        issue(0, 0)

    @pl.when(i + 1 < nused)
    def _():
        issue(i + 1, (i + 1) % 2)

    @pl.when(i < nused)
    def _():
        slot = i % 2
        pltpu.make_async_copy(x_hbm.at[pl.ds(0, rows // SUBLANES)], buf.at[slot],
                              sem.at[slot]).wait()
        lo, hi = _unpack_bf16_pairs(buf[slot].reshape(rows, -1))
        half = lo.shape[1]
        out_ref[:, :half] = lo.astype(out_ref.dtype)
        out_ref[:, half:] = hi.astype(out_ref.dtype)

    @pl.when(i >= nused)
    def _():
        out_ref[...] = jnp.zeros_like(out_ref)


def _gather_rows(slot_tok, nused, xs_packed, *, rows):
    p = slot_tok.shape[0]
    half = xs_packed.shape[1]
    grid_spec = pltpu.PrefetchScalarGridSpec(
        num_scalar_prefetch=2,
        grid=(p // rows,),
        in_specs=[pl.BlockSpec(memory_space=pl.ANY)],
        out_specs=pl.BlockSpec((rows, 2 * half), lambda i, tok, nu: (i, 0)),
        scratch_shapes=[pltpu.VMEM((2, rows // SUBLANES, SUBLANES, half), jnp.uint32),
                        pltpu.SemaphoreType.DMA((2,))],
    )
    return pl.pallas_call(
        functools.partial(_gather_kernel, rows=rows),
        grid_spec=grid_spec,
        out_shape=jax.ShapeDtypeStruct((p, 2 * half), BF16),
        compiler_params=_params(("arbitrary",)),
        name="moe_gather",
    )(slot_tok, nused, xs_packed.reshape(-1, SUBLANES, half))


def _start_all(copies):
    for c in copies:
        c.start()


def _wait_all(copies):
    for c in copies:
        c.wait()


def _weight_pipeline(step, flag, tabs, copies_for, convert):
    exp_ref, tile_ref, gslot_ref, nexp_ref, ntile_ref = tabs

    @pl.when((flag & 2) != 0)
    def _():
        slot = gslot_ref[step]
        mine = copies_for(exp_ref[step], tile_ref[step], slot)

        @pl.when(step == 0)
        def _():
            _start_all(mine)

        _wait_all(mine)
        convert(slot)

        @pl.when((flag & 4) != 0)
        def _():
            _start_all(copies_for(nexp_ref[step], ntile_ref[step], 1 - slot))


def _expert_up_kernel(blk_ref, tile_ref, exp_ref, flag_ref, gslot_ref, nexp_ref, ntile_ref,
                      x_ref, w_hbm, bg_ref, bl_ref, out_ref, wbuf, wg_bf, wl_bf, sem, *, tf, nt):
    step = pl.program_id(0)
    flag = flag_ref[step]

    def copies_for(e, tile, slot):
        return [pltpu.make_async_copy(
            w_hbm.at[e, :, pl.ds(pl.multiple_of((half * nt + tile) * tf, tf), tf)],
            wbuf.at[slot, half], sem.at[slot, half]) for half in range(2)]

    def convert(slot):
        wg_bf[...] = wbuf[slot, 0].astype(BF16)
        wl_bf[...] = wbuf[slot, 1].astype(BF16)

    _weight_pipeline(step, flag, (exp_ref, tile_ref, gslot_ref, nexp_ref, ntile_ref), copies_for,
                     convert)

    @pl.when((flag & 1) != 0)
    def _():
        x = x_ref[...]
        glu = jnp.dot(x, wg_bf[...], preferred_element_type=F32) + bg_ref[...]
        lin = jnp.dot(x, wl_bf[...], preferred_element_type=F32) + bl_ref[...]
        glu = jnp.minimum(glu, SWIGLU_LIMIT)
        lin = jnp.clip(lin, -SWIGLU_LIMIT, SWIGLU_LIMIT)
        out_ref[...] = (glu * jax.nn.sigmoid(SWIGLU_ALPHA * glu) * (lin + 1.0)).astype(out_ref.dtype)

    @pl.when((flag & 1) == 0)
    def _():
        out_ref[...] = jnp.zeros_like(out_ref)


def _expert_up(tables, xg, w_up, b_up, *, tm, tf):
    p, d = xg.shape
    n_e, _, ff2 = w_up.shape
    ff = ff2 // 2
    nt = ff // tf
    nsteps = tables[0].shape[0]
    bias = lambda half: pl.BlockSpec((None, 1, tf),
                                     lambda s, blk, tile, ex, *_: (ex[s], 0, half * nt + tile[s]))
    grid_spec = pltpu.PrefetchScalarGridSpec(
        num_scalar_prefetch=len(tables),
        grid=(nsteps,),
        in_specs=[
            pl.BlockSpec((tm, d), lambda s, blk, *_: (blk[s], 0)),
            pl.BlockSpec(memory_space=pl.ANY),
            bias(0), bias(1),
        ],
        out_specs=pl.BlockSpec((tm, tf), lambda s, blk, tile, *_: (blk[s], tile[s])),
        scratch_shapes=[pltpu.VMEM((2, 2, d, tf), F32), pltpu.VMEM((d, tf), BF16),
                        pltpu.VMEM((d, tf), BF16), pltpu.SemaphoreType.DMA((2, 2))],
    )
    return pl.pallas_call(
        functools.partial(_expert_up_kernel, tf=tf, nt=nt),
        grid_spec=grid_spec,
        out_shape=jax.ShapeDtypeStruct((p, ff), BF16),
        compiler_params=_params(("arbitrary",), VMEM_LIMIT_MAX),
        name="expert_up",
    )(*tables, xg, w_up, b_up, b_up)


def _expert_down_kernel(blk_ref, tile_ref, exp_ref, flag_ref, gslot_ref, nexp_ref, ntile_ref,
                        a_ref, w_hbm, b_ref, out_ref, wbuf, w_bf, sem, *, tn):
    step = pl.program_id(0)
    flag = flag_ref[step]

    def copies_for(e, tile, slot):
        return [pltpu.make_async_copy(w_hbm.at[e, :, pl.ds(pl.multiple_of(tile * tn, tn), tn)],
                                      wbuf.at[slot], sem.at[slot])]

    def convert(slot):
        w_bf[...] = wbuf[slot].astype(BF16)

    _weight_pipeline(step, flag, (exp_ref, tile_ref, gslot_ref, nexp_ref, ntile_ref), copies_for,
                     convert)

    @pl.when((flag & 1) != 0)
    def _():
        y = jnp.dot(a_ref[...], w_bf[...], preferred_element_type=F32) + b_ref[...]
        out_ref[...] = _pack_bf16_pairs(y)

    @pl.when((flag & 1) == 0)
    def _():
        out_ref[...] = jnp.zeros_like(out_ref)


def _expert_down(tables, act, w_down, b_down, *, tm, tn):
    p, ff = act.shape
    d = w_down.shape[2]
    nsteps = tables[0].shape[0]
    grid_spec = pltpu.PrefetchScalarGridSpec(
        num_scalar_prefetch=len(tables),
        grid=(nsteps,),
        in_specs=[
            pl.BlockSpec((tm, ff), lambda s, blk, *_: (blk[s], 0)),
            pl.BlockSpec(memory_space=pl.ANY),
            pl.BlockSpec((None, 1, tn), lambda s, blk, tile, ex, *_: (ex[s], 0, tile[s])),
        ],
        out_specs=pl.BlockSpec((tm, tn // 2), lambda s, blk, tile, *_: (blk[s], tile[s])),
        scratch_shapes=[pltpu.VMEM((2, ff, tn), F32), pltpu.VMEM((ff, tn), BF16),
                        pltpu.SemaphoreType.DMA((2,))],
    )
    return pl.pallas_call(
        functools.partial(_expert_down_kernel, tn=tn),
        grid_spec=grid_spec,
        out_shape=jax.ShapeDtypeStruct((p, d // 2), jnp.uint32),
        compiler_params=_params(("arbitrary",)),
        name="expert_down",
    )(*tables, act, w_down, b_down)


def _expert_schedule(counts, *, tm, n_tiles, n_blocks):
    n_e = counts.shape[0]
    nb = (counts + tm - 1) // tm
    bend = jnp.cumsum(nb)
    bstart = bend - nb
    steps_e = nb * n_tiles
    cs = jnp.cumsum(steps_e)
    total = cs[-1]
    s = jnp.arange(n_blocks * n_tiles, dtype=jnp.int32)
    sc = jnp.minimum(s, total - 1)
    e = jnp.sum((cs[None, :] <= sc[:, None]).astype(jnp.int32), axis=1)
    e = jnp.minimum(e, n_e - 1)
    r = sc - (cs[e] - steps_e[e])
    nbe = jnp.maximum(nb[e], 1)
    valid = s < total
    n_unused = jnp.maximum(n_blocks - bend[-1], 1)
    u = jnp.maximum(s - total, 0)
    live_tile = r // nbe
    tile = jnp.where(valid, live_tile, u // n_unused)
    blk = jnp.where(valid, bstart[e] + r % nbe, bend[-1] + u % n_unused)
    first = jnp.logical_and(valid, r % nbe == 0)
    ids = jnp.where(nb > 0, jnp.arange(n_e, dtype=jnp.int32), n_e)
    later = jnp.concatenate([lax.cummin(ids, reverse=True)[1:], jnp.full((1,), n_e, jnp.int32)])
    wraps = live_tile + 1 >= n_tiles
    nexp = jnp.where(wraps, later[e], e)
    ntile = jnp.where(wraps, 0, live_tile + 1)
    flag = (valid.astype(jnp.int32) + 2 * first.astype(jnp.int32)
            + 4 * jnp.logical_and(first, nexp < n_e).astype(jnp.int32))
    gslot = (jnp.cumsum(first.astype(jnp.int32)) - 1) % 2
    i32 = lambda a: a.astype(jnp.int32)
    return (i32(blk), i32(tile), i32(e), flag, i32(gslot), i32(jnp.minimum(nexp, n_e - 1)),
            i32(ntile))


def _combine_kernel(slot_ref, x_ref, route_ref, y_hbm, out_ref, buf, sem, *, tc, tn):
    i = pl.program_id(0)
    n = pl.num_programs(0)

    def issue(blk, bslot):
        def body(r, carry):
            for k in range(TOP_K):
                src = slot_ref[(blk * tc + r) * TOP_K + k]
                pltpu.make_async_copy(y_hbm.at[pl.ds(src, 1)], buf.at[bslot, k, pl.ds(r, 1)],
                                      sem.at[bslot]).start()
            return carry
        lax.fori_loop(0, tc, body, 0, unroll=4)

    @pl.when(i == 0)
    def _():
        issue(0, 0)

    @pl.when(i + 1 < n)
    def _():
        issue(i + 1, (i + 1) % 2)

    bslot = i % 2
    for k in range(TOP_K):
        pltpu.make_async_copy(y_hbm.at[pl.ds(0, tc)], buf.at[bslot, k], sem.at[bslot]).wait()
    gates = [route_ref[:, TOP_K + k:TOP_K + k + 1] for k in range(TOP_K)]
    hw = tn // 2
    for j in range(out_ref.shape[1] // tn):
        parts = [_unpack_bf16_pairs(buf[bslot, k, :, j * hw:(j + 1) * hw]) for k in range(TOP_K)]
        for half in range(2):
            cols = slice(j * tn + half * hw, j * tn + (half + 1) * hw)
            acc = x_ref[:, cols]
            for k in range(TOP_K):
                acc = acc + gates[k] * parts[k][half]
            out_ref[:, cols] = acc


def _combine(slot_flat, x2, route, y, *, tc, tn):
    t, d = x2.shape
    grid_spec = pltpu.PrefetchScalarGridSpec(
        num_scalar_prefetch=1,
        grid=(t // tc,),
        in_specs=[pl.BlockSpec((tc, d), lambda i, sl: (i, 0)),
                  pl.BlockSpec((tc, LANES), lambda i, sl: (i, 0)),
                  pl.BlockSpec(memory_space=pl.ANY)],
        out_specs=pl.BlockSpec((tc, d), lambda i, sl: (i, 0)),
        scratch_shapes=[pltpu.VMEM((2, TOP_K, tc, d // 2), jnp.uint32),
                        pltpu.SemaphoreType.DMA((2,))],
    )
    return pl.pallas_call(
        functools.partial(_combine_kernel, tc=tc, tn=tn),
        grid_spec=grid_spec,
        out_shape=jax.ShapeDtypeStruct((t, d), F32),
        compiler_params=_params(("arbitrary",)),
        name="moe_combine",
    )(slot_flat, x2, route, y)


def _pad_lanes(a, offset=0):
    width = a.shape[-1]
    return jnp.pad(a, [(0, 0)] * (a.ndim - 1) + [(offset, LANES - offset - width)])


def _layer(x2d, mem2d, lw, w_in_stack, layer, *, batch, seq, mem_len):
    t, d = x2d.shape
    gla_qk = lw["gla_alpha_up"].shape[1]
    gla_rank = lw["gla_alpha_up"].shape[0]
    gla_v = lw["w_branch_gla"].shape[0]
    fox_w = lw["w_branch_fox"].shape[0]
    dk, dv = gla_qk // GLA_HEADS, gla_v // GLA_HEADS
    dh = fox_w // FOX_HEADS
    mem_w = lw["mem_w_q"].shape[1]
    mem_dh = mem_w // MEM_HEADS
    n_experts = lw["w_router"].shape[1]
    ff = lw["w_expert_down"].shape[1]

    sizes = (gla_qk, gla_qk, gla_v, gla_v, gla_rank, fox_w, fox_w, fox_w, FOX_HEADS, d, d)
    starts = [0]
    for sz in sizes:
        starts.append(starts[-1] + sz)
    big_ids = (0, 1, 2, 3, 5, 6, 7, 9, 10)
    off = {}
    acc = 0
    for i in big_ids:
        off[i] = acc
        acc += sizes[i]
    n_big = acc
    w_big, w_small = _w_in_pack(
        w_in_stack, layer, spans=tuple((starts[i], sizes[i], off[i]) for i in big_ids),
        pieces=((starts[4], sizes[4], 0), (starts[8], sizes[8], sizes[4])), tr=TILE["w_pack_rows"])
    tn1 = TILE["proj_tn"]
    colscale = jnp.ones((1, n_big), F32)
    colscale = colscale.at[:, off[5]:off[5] + fox_w].set(
        jnp.tile(lw["fox_q_norm_g"] * (dh ** -0.5 * LOG2E), FOX_HEADS)[None])
    colscale = colscale.at[:, off[6]:off[6] + fox_w].set(jnp.tile(lw["fox_k_norm_g"], FOX_HEADS)[None])
    proj, small = _norm_matmul(
        x2d, lw["mix_norm_g"][None], w_big, colscale, w_small,
        tm=TILE["proj_tm"], tn=tn1, norm_lo=off[5] // tn1, norm_hi=(off[6] + fox_w) // tn1)

    alpha_up_pad = jnp.pad(lw["gla_alpha_up"], ((0, LANES - gla_rank), (0, 0))).astype(BF16)
    o_gla = _gla(proj, small, alpha_up_pad, lw["gla_alpha_bias"][None], lw["gla_out_norm_g"][None],
                 batch=batch, seq=seq, dk=dk, dv=dv, offs=(off[0], off[1], off[2], off[3]),
                 rows=TILE["gla_rows"], hps=TILE["gla_heads_per_step"])

    c = _fcum(small, _pad_lanes(lw["fox_f_bias"][None], gla_rank), batch=batch, seq=seq,
              tb=TILE["fcum_rows"])
    c_row = c[:, gla_rank:gla_rank + FOX_HEADS].reshape(batch, seq, FOX_HEADS)
    c_row = c_row.transpose(0, 2, 1).reshape(batch * FOX_HEADS, 1, seq) * LOG2E
    o_fox = _fox(proj, c_row, batch=batch, seq=seq, dh=dh, offs=(off[5], off[6], off[7]),
                 tq=min(seq, TILE["fox_tq"]))

    merged = _merge(o_gla, o_fox, proj, lw["w_branch_gla"].astype(BF16),
                    lw["w_branch_fox"].astype(BF16), offs=(off[9], off[10]),
                    tm=TILE["merge_tm"], tn=TILE["merge_tn"])
    x1 = _resid_matmul(merged, lw["w_out"].astype(BF16), x2d, tm=TILE["resid_tm"],
                       tn=TILE["resid_tn"])

    kv_scale = jnp.concatenate([jnp.tile(lw["mem_k_norm_g"], MEM_HEADS),
                                jnp.ones((mem_w,), F32)])[None]
    kv = _norm_matmul(mem2d, lw["mem_kv_norm_g"][None], lw["mem_w_kv"].astype(BF16), kv_scale, None,
                      tm=mem2d.shape[0], tn=mem_w, norm_lo=0, norm_hi=1)
    x2, rlogits, xs_packed = _memory(
        x1, lw["mem_norm_g"][None], lw["mem_w_q"].astype(BF16),
        (lw["mem_q_norm_g"] * (mem_dh ** -0.5))[None], kv,
        lw["mem_w_o"].astype(BF16), lw["moe_norm_g"][None],
        _pad_lanes(lw["w_router"]).astype(BF16), _pad_lanes(lw["b_router"][None]),
        batch=batch, seq=seq, mem_len=mem_len, tm=TILE["mem_tm"])

    route, counts_f = _route(rlogits, n_experts=n_experts, tr=TILE["route_rows"])
    tm_e = TILE["expert_tm"]
    a_total = t * TOP_K
    n_blocks = a_total // tm_e + n_experts
    p_rows = n_blocks * tm_e
    counts = counts_f[0, :n_experts].astype(jnp.int32)
    padded = (counts + tm_e - 1) // tm_e * tm_e
    pstart = jnp.cumsum(padded) - padded
    slot = _slots(route, _pad_lanes(pstart.astype(F32)[None]), tr=TILE["route_rows"])
    slot = slot[:, :TOP_K].astype(jnp.int32).reshape(a_total)
    slot_tok = (jnp.arange(p_rows, dtype=jnp.int32) % t).at[slot].set(
        jnp.arange(a_total, dtype=jnp.int32) // TOP_K, unique_indices=True,
        mode="promise_in_bounds")

    tf = TILE["expert_tf"]
    g_rows = TILE["gather_rows"]
    nused = (jnp.sum(padded) // g_rows).astype(jnp.int32).reshape(1)
    xg = _gather_rows(slot_tok, nused, xs_packed, rows=g_rows)
    up_tables = _expert_schedule(counts, tm=tm_e, n_tiles=ff // tf, n_blocks=n_blocks)
    act = _expert_up(up_tables, xg, lw["w_expert_up"], lw["b_expert_up"][:, None, :], tm=tm_e, tf=tf)
    tn_d = TILE["expert_tn"]
    down_tables = _expert_schedule(counts, tm=tm_e, n_tiles=d // tn_d, n_blocks=n_blocks)
    y = _expert_down(down_tables, act, lw["w_expert_down"], lw["b_expert_down"][:, None, :],
                     tm=tm_e, tn=tn_d)
    return _combine(slot, x2, route, y, tc=TILE["combine_rows"], tn=tn_d)


def kernel(x, mem, mix_norm_g, w_in, gla_alpha_up, gla_alpha_bias, gla_out_norm_g, fox_f_bias, fox_q_norm_g, fox_k_norm_g, w_branch_gla, w_branch_fox, w_out, mem_norm_g, mem_kv_norm_g, mem_w_q, mem_w_kv, mem_q_norm_g, mem_k_norm_g, mem_w_o, moe_norm_g, w_router, b_router, w_expert_up, b_expert_up, w_expert_down, b_expert_down):
    weights = dict(
        mix_norm_g=mix_norm_g, w_in=w_in, gla_alpha_up=gla_alpha_up, gla_alpha_bias=gla_alpha_bias,
        gla_out_norm_g=gla_out_norm_g, fox_f_bias=fox_f_bias, fox_q_norm_g=fox_q_norm_g,
        fox_k_norm_g=fox_k_norm_g, w_branch_gla=w_branch_gla, w_branch_fox=w_branch_fox, w_out=w_out,
        mem_norm_g=mem_norm_g, mem_kv_norm_g=mem_kv_norm_g, mem_w_q=mem_w_q, mem_w_kv=mem_w_kv,
        mem_q_norm_g=mem_q_norm_g, mem_k_norm_g=mem_k_norm_g, mem_w_o=mem_w_o, moe_norm_g=moe_norm_g,
        w_router=w_router, b_router=b_router, w_expert_up=w_expert_up, b_expert_up=b_expert_up,
        w_expert_down=w_expert_down, b_expert_down=b_expert_down)
    batch, seq, d = x.shape
    mem_len = mem.shape[1]
    x2d = x.reshape(batch * seq, d)
    mem2d = mem.reshape(batch * mem_len, d)
    for layer in range(mix_norm_g.shape[0]):
        lw = {name: w[layer] for name, w in weights.items() if name != "w_in"}
        x2d = _layer(x2d, mem2d, lw, w_in, layer, batch=batch, seq=seq, mem_len=mem_len)
    return x2d.reshape(batch, seq, d)
```

```python
import functools

import jax
import jax.numpy as jnp
from jax import lax
from jax.experimental import pallas as pl
from jax.experimental.pallas import tpu as pltpu

F32 = jnp.float32
BF16 = jnp.bfloat16

EPS = 1e-6
LANES = 128
SUBLANES = 8
VMEM_LIMIT = 56 * 1024 * 1024
VMEM_LIMIT_MAX = 61 * 1024 * 1024

GLA_HEADS = 4
GLA_GATE_NORM = 16.0
GLA_CHUNK = 64
GLA_SUB = 16
FOX_HEADS = 16
FOX_STRIP = 16
FOX_GROUP = 128
LOG2E = 1.4426950408889634
MEM_HEADS = 4
TOP_K = 4
SWIGLU_LIMIT = 7.0
SWIGLU_ALPHA = 1.702

TILE = dict(
    w_pack_rows=128,
    proj_tm=1024, proj_tn=1024,
    gla_rows=256, gla_heads_per_step=4,
    fcum_rows=512,
    fox_tq=2048,
    merge_tm=1024, merge_tn=512,
    resid_tm=1024, resid_tn=1024,
    mem_tm=512,
    route_rows=512,
    expert_tm=512, expert_tf=1024, expert_tn=1024,
    gather_rows=1024, combine_rows=512,
)

NT_DIMS = (((1,), (1,)), ((), ()))
TN_DIMS = (((0,), (0,)), ((), ()))


def _params(sem, vmem=VMEM_LIMIT):
    return pltpu.CompilerParams(dimension_semantics=sem, vmem_limit_bytes=vmem)


def _rms(xf, g):
    return xf * lax.rsqrt(jnp.mean(xf * xf, axis=-1, keepdims=True) + EPS) * g


def _pack_bf16_pairs(a):
    half = a.shape[1] // 2
    bits = lambda v: lax.bitcast_convert_type(v.astype(BF16).astype(F32), jnp.uint32)
    return (bits(a[:, half:]) & jnp.uint32(0xFFFF0000)) | (bits(a[:, :half]) >> 16)


def _unpack_bf16_pairs(u):
    lo = lax.bitcast_convert_type(u << 16, F32)
    hi = lax.bitcast_convert_type(u & jnp.uint32(0xFFFF0000), F32)
    return lo, hi


def _log_sigmoid(z):
    return jnp.minimum(z, 0.0) - jnp.log1p(jnp.exp(-jnp.abs(z)))


def _w_in_pack_kernel(w_ref, tail_ref, big_ref, small_ref, *, spans, pieces):
    rows, n_cols = w_ref.shape
    lane = lax.broadcasted_iota(jnp.int32, (rows, LANES), 1)
    tile_at = lambda c0: w_ref[:, c0:c0 + LANES]
    for s0, width, d0 in spans:
        sh = s0 % LANES
        a0 = s0 - sh
        for t in range(width // LANES):
            dst = slice(d0 + t * LANES, d0 + (t + 1) * LANES)
            if sh == 0:
                big_ref[:, dst] = tile_at(a0 + t * LANES).astype(BF16)
            elif a0 + (t + 2) * LANES <= n_cols:
                left = pltpu.roll(tile_at(a0 + t * LANES), LANES - sh, 1)
                right = pltpu.roll(tile_at(a0 + (t + 1) * LANES), LANES - sh, 1)
                big_ref[:, dst] = jnp.where(lane < LANES - sh, left, right).astype(BF16)
            else:
                assert s0 + (t + 1) * LANES == n_cols
                big_ref[:, dst] = tail_ref[...].astype(BF16)
    side = jnp.zeros((rows, LANES), F32)
    for s0, width, dl in pieces:
        sh = s0 % LANES
        tile = tile_at(s0 - sh)
        if (dl - sh) % LANES:
            tile = pltpu.roll(tile, (dl - sh) % LANES, 1)
        side = jnp.where(jnp.logical_and(lane >= dl, lane < dl + width), tile, side)
    small_ref[...] = side.astype(BF16)


def _w_in_pack(w_stack, layer, *, spans, pieces, tr):
    _, d, n_cols = w_stack.shape
    n_big = sum(width for _, width, _ in spans)
    w_big, w_small = pl.pallas_call(
        functools.partial(_w_in_pack_kernel, spans=spans, pieces=pieces),
        grid=(d // tr,),
        in_specs=[pl.BlockSpec((None, tr, n_cols), lambda i: (layer, i, 0)),
                  pl.BlockSpec((tr, LANES), lambda i: (i, 0))],
        out_specs=[pl.BlockSpec((tr, n_big), lambda i: (i, 0)),
                   pl.BlockSpec((tr, LANES), lambda i: (i, 0))],
        out_shape=[jax.ShapeDtypeStruct((d, n_big), BF16), jax.ShapeDtypeStruct((d, LANES), BF16)],
        compiler_params=_params(("arbitrary",)),
        name="w_in_pack",
    )(w_stack, w_stack[layer, :, n_cols - LANES:])
    return w_big, w_small


def _norm_matmul_kernel(*refs, norm_lo, norm_hi, tn, has_small):
    if has_small:
        x_ref, g_ref, w_ref, cs_ref, ws_ref, out_ref, small_ref, h_ref = refs
    else:
        x_ref, g_ref, w_ref, cs_ref, out_ref, h_ref = refs
    j = pl.program_id(1)

    @pl.when(j == 0)
    def _():
        hb = _rms(x_ref[...], g_ref[...]).astype(BF16)
        h_ref[...] = hb
        if has_small:
            small_ref[...] = jnp.dot(hb, ws_ref[...], preferred_element_type=F32)

    acc = jnp.dot(h_ref[...], w_ref[...], preferred_element_type=F32)
    is_norm = jnp.logical_and(j >= norm_lo, j < norm_hi)

    @pl.when(is_norm)
    def _():
        for c in range(tn // LANES):
            sl = slice(c * LANES, (c + 1) * LANES)
            out_ref[:, sl] = _rms(acc[:, sl], cs_ref[:, sl]).astype(out_ref.dtype)

    @pl.when(jnp.logical_not(is_norm))
    def _():
        out_ref[...] = acc.astype(out_ref.dtype)


def _norm_matmul(x, g, w, colscale, w_small, *, tm, tn, norm_lo, norm_hi):
    m, d = x.shape
    n = w.shape[1]
    has_small = w_small is not None
    in_specs = [
        pl.BlockSpec((tm, d), lambda i, j: (i, 0)),
        pl.BlockSpec((1, d), lambda i, j: (0, 0)),
        pl.BlockSpec((d, tn), lambda i, j: (0, j)),
        pl.BlockSpec((1, tn), lambda i, j: (0, j)),
    ]
    args = [x, g, w, colscale]
    out_shape = [jax.ShapeDtypeStruct((m, n), BF16)]
    out_specs = [pl.BlockSpec((tm, tn), lambda i, j: (i, j))]
    if has_small:
        in_specs.append(pl.BlockSpec((d, LANES), lambda i, j: (0, 0)))
        args.append(w_small)
        out_shape.append(jax.ShapeDtypeStruct((m, LANES), F32))
        out_specs.append(pl.BlockSpec((tm, LANES), lambda i, j: (i, 0)))
    res = pl.pallas_call(
        functools.partial(_norm_matmul_kernel, norm_lo=norm_lo, norm_hi=norm_hi, tn=tn,
                          has_small=has_small),
        grid=(m // tm, n // tn),
        in_specs=in_specs,
        out_specs=out_specs,
        out_shape=out_shape,
        scratch_shapes=[pltpu.VMEM((tm, d), BF16)],
        compiler_params=_params(("arbitrary", "arbitrary")),
        name="norm_matmul",
    )(*args)
    return res if has_small else res[0]


def _prefix_sum_rows(tri_bf, x):
    p1 = x.astype(BF16)
    r1 = x - p1.astype(F32)
    p2 = r1.astype(BF16)
    p3 = (r1 - p2.astype(F32)).astype(BF16)
    dot = lambda p: jnp.dot(tri_bf, p, preferred_element_type=F32)
    return (dot(p1) + dot(p2)) + dot(p3)


def _gla_kernel(q_ref, k_ref, v_ref, r_ref, sm_ref, au_ref, ab_ref, gn_ref, out_ref, st_ref,
                *, chunk, nsub, scale, hps, dk, dv):
    @pl.when(pl.program_id(2) == 0)
    def _():
        st_ref[...] = jnp.zeros_like(st_ref)

    row = lax.broadcasted_iota(jnp.int32, (chunk, chunk), 0)
    col = lax.broadcasted_iota(jnp.int32, (chunk, chunk), 1)
    causal = row >= col
    tri = causal.astype(BF16)
    key_row = lax.broadcasted_iota(jnp.int32, (chunk, 1), 0)
    for s in range(nsub):
        rows = pl.ds(s * chunk, chunk)
        ga = sm_ref[rows, :].astype(BF16)
        for hh in range(hps):
            kc = slice(hh * dk, (hh + 1) * dk)
            vc = slice(hh * dv, (hh + 1) * dv)
            z = jnp.dot(ga, au_ref[:, kc], preferred_element_type=F32)
            la = _log_sigmoid(z + ab_ref[:, kc]) * (1.0 / GLA_GATE_NORM)
            b = _prefix_sum_rows(tri, la)
            b_last = b[chunk - 1:chunk, :]
            q = q_ref[rows, kc].astype(F32)
            k = k_ref[rows, kc].astype(F32)
            v = v_ref[rows, vc]
            qd = (q * (scale * jnp.exp(b))).astype(BF16)
            kl = (k * jnp.exp(b_last - b)).astype(BF16)
            blocks = []
            for gi in range(chunk // GLA_SUB):
                grp = slice(gi * GLA_SUB, (gi + 1) * GLA_SUB)
                if gi == 0:
                    q_fac, k_arg = b[grp], -b
                else:
                    r = b[gi * GLA_SUB - 1:gi * GLA_SUB, :]
                    q_fac, k_arg = b[grp] - r, r - b
                k_arg = jnp.where(key_row < (gi + 1) * GLA_SUB, k_arg, -jnp.inf)
                qg = (q[grp] * (scale * jnp.exp(q_fac))).astype(BF16)
                kg = (k * jnp.exp(k_arg)).astype(BF16)
                blocks.append(lax.dot_general(qg, kg, NT_DIMS, preferred_element_type=F32))
            sc = jnp.where(causal, jnp.concatenate(blocks, axis=0), 0.0).astype(BF16)
            st = st_ref[hh]
            o = jnp.dot(sc, v, preferred_element_type=F32) + lax.dot_general(
                qd, st.astype(BF16), NT_DIMS, preferred_element_type=F32)
            st_ref[hh] = st * jnp.exp(b_last) + lax.dot_general(
                v, kl, TN_DIMS, preferred_element_type=F32)
            r = r_ref[rows, vc].astype(F32)
            out_ref[rows, vc] = (_rms(o, gn_ref[...]) * (r * jax.nn.sigmoid(r))).astype(out_ref.dtype)


def _gla(proj, small, alpha_up_pad, alpha_bias, out_norm_g, *, batch, seq, dk, dv, offs, rows, hps):
    t = batch * seq
    nblk = seq // rows
    h = GLA_HEADS
    wk, wv = hps * dk, hps * dv
    qo, ko, vo, ro = (offs[0] // wk, offs[1] // wk, offs[2] // wv, offs[3] // wv)
    rmap = lambda b, hh, c: b * nblk + c
    return pl.pallas_call(
        functools.partial(_gla_kernel, chunk=GLA_CHUNK, nsub=rows // GLA_CHUNK, scale=dk ** -0.5,
                          hps=hps, dk=dk, dv=dv),
        grid=(batch, h // hps, nblk),
        in_specs=[
            pl.BlockSpec((rows, wk), lambda b, hh, c: (rmap(b, hh, c), qo + hh)),
            pl.BlockSpec((rows, wk), lambda b, hh, c: (rmap(b, hh, c), ko + hh)),
            pl.BlockSpec((rows, wv), lambda b, hh, c: (rmap(b, hh, c), vo + hh)),
            pl.BlockSpec((rows, wv), lambda b, hh, c: (rmap(b, hh, c), ro + hh)),
            pl.BlockSpec((rows, LANES), lambda b, hh, c: (rmap(b, hh, c), 0)),
            pl.BlockSpec((LANES, wk), lambda b, hh, c: (0, hh)),
            pl.BlockSpec((1, wk), lambda b, hh, c: (0, hh)),
            pl.BlockSpec((1, dv), lambda b, hh, c: (0, 0)),
        ],
        out_specs=pl.BlockSpec((rows, wv), lambda b, hh, c: (rmap(b, hh, c), hh)),
        out_shape=jax.ShapeDtypeStruct((t, h * dv), BF16),
        scratch_shapes=[pltpu.VMEM((hps, dv, dk), F32)],
        compiler_params=_params(("arbitrary", "arbitrary", "arbitrary")),
        name="gla",
    )(proj, proj, proj, proj, small, alpha_up_pad, alpha_bias, out_norm_g)


def _fcum_kernel(sm_ref, bias_ref, out_ref, carry_ref, *, tb):
    @pl.when(pl.program_id(1) == 0)
    def _():
        carry_ref[...] = jnp.zeros_like(carry_ref)

    row = lax.broadcasted_iota(jnp.int32, (tb, tb), 0)
    col = lax.broadcasted_iota(jnp.int32, (tb, tb), 1)
    tri = (row >= col).astype(BF16)
    lf = _log_sigmoid(sm_ref[...] + bias_ref[...])
    c = _prefix_sum_rows(tri, lf) + carry_ref[...]
    out_ref[...] = c
    carry_ref[...] = c[tb - 1:tb, :]


def _fcum(small, bias_pad, *, batch, seq, tb):
    nb = seq // tb
    return pl.pallas_call(
        functools.partial(_fcum_kernel, tb=tb),
        grid=(batch, nb),
        in_specs=[pl.BlockSpec((tb, LANES), lambda b, i: (b * nb + i, 0)),
                  pl.BlockSpec((1, LANES), lambda b, i: (0, 0))],
        out_specs=pl.BlockSpec((tb, LANES), lambda b, i: (b * nb + i, 0)),
        out_shape=jax.ShapeDtypeStruct(small.shape, F32),
        scratch_shapes=[pltpu.VMEM((1, LANES), F32)],
        compiler_params=_params(("arbitrary", "arbitrary")),
        name="forget_cumsum",
    )(small, bias_pad)


def _fox_kernel(qt_ref, kt_ref, q_ref, k_ref, v_ref, c_ref, out_ref, m_ref, mn_ref, a_ref, l_ref,
                acc_ref, s_ref, p_ref, *, tq, rb, grp):
    p = pl.program_id(2)
    qi = qt_ref[p]
    ki = kt_ref[p]

    @pl.when(ki == 0)
    def _():
        m_ref[...] = jnp.full_like(m_ref, -jnp.inf)
        l_ref[...] = jnp.zeros_like(l_ref)
        acc_ref[...] = jnp.zeros_like(acc_ref)

    def update(diag):
        c = c_ref[0]

        def ncols(g):
            return (g + 1) * grp if diag else tq

        def qk(g):
            n = ncols(g)
            s_ref[g % 2, :, 0:n] = lax.dot_general(q_ref[g * grp:(g + 1) * grp, :], k_ref[0:n, :],
                                                   NT_DIMS, preferred_element_type=F32)

        def pv(g):
            n = ncols(g)
            rows = slice(g * grp, (g + 1) * grp)
            acc_ref[rows, :] += jnp.dot(p_ref[g % 2, :, 0:n], v_ref[0:n, :],
                                        preferred_element_type=F32)

        def softmax(g):
            buf = g % 2
            strips = []
            for st in range(grp // rb):
                r0 = g * grp + st * rb
                visible = (r0 + rb - 1) // LANES + 1 if diag else tq // LANES
                strips.append((slice(st * rb, (st + 1) * rb), slice(r0, r0 + rb), r0, visible))

            def logits(rows_l, r0, t):
                cols = slice(t * LANES, (t + 1) * LANES)
                s = s_ref[buf, rows_l, cols] - c[:, cols]
                if diag and (t + 1) * LANES - 1 > r0:
                    row = r0 + lax.broadcasted_iota(jnp.int32, (rb, LANES), 0)
                    col = t * LANES + lax.broadcasted_iota(jnp.int32, (rb, LANES), 1)
                    s = jnp.where(row >= col, s, -jnp.inf)
                return s

            for rows_l, rows, r0, visible in strips:
                part = logits(rows_l, r0, 0)
                for t in range(1, visible):
                    part = jnp.maximum(part, logits(rows_l, r0, t))
                m_old = m_ref[rows, :]
                m_new = jnp.maximum(m_old, jnp.max(part, axis=-1, keepdims=True))
                a_ref[rows, :] = jnp.exp2(m_old - m_new)
                mn_ref[rows, :] = m_new
            for rows_l, rows, r0, visible in strips:
                m_new = mn_ref[rows, :]
                m_ref[rows, :] = m_new
                part = jnp.zeros((rb, LANES), F32)
                for t in range(ncols(g) // LANES):
                    cols = slice(t * LANES, (t + 1) * LANES)
                    if t < visible:
                        pr = jnp.exp2(logits(rows_l, r0, t) - m_new)
                        part = part + pr
                        p_ref[buf, rows_l, cols] = pr.astype(BF16)
                    else:
                        p_ref[buf, rows_l, cols] = jnp.zeros((rb, LANES), BF16)
                alpha = a_ref[rows, :]
                l_ref[rows, :] = alpha * l_ref[rows, :] + part
                acc_ref[rows, :] = alpha * acc_ref[rows, :]

        ngrp = tq // grp
        qk(0)
        for g in range(ngrp):
            if g + 1 < ngrp:
                qk(g + 1)
            softmax(g)
            if g >= 1:
                pv(g - 1)
        pv(ngrp - 1)

    @pl.when(ki < qi)
    def _():
        update(False)

    @pl.when(ki == qi)
    def _():
        update(True)
        denom = jnp.sum(l_ref[...], axis=-1, keepdims=True)
        out_ref[...] = (acc_ref[...] / denom).astype(out_ref.dtype)


def _fox(proj, c_row, *, batch, seq, dh, offs, tq):
    t = batch * seq
    nq = seq // tq
    pairs = [(a, b) for a in range(nq) for b in range(a + 1)]
    qt = jnp.asarray([a for a, _ in pairs], jnp.int32)
    kt = jnp.asarray([b for _, b in pairs], jnp.int32)
    qo, ko, vo = (o // dh for o in offs)
    grp = min(tq, FOX_GROUP)
    grid_spec = pltpu.PrefetchScalarGridSpec(
        num_scalar_prefetch=2,
        grid=(batch, FOX_HEADS, len(pairs)),
        in_specs=[
            pl.BlockSpec((tq, dh), lambda b, h, p, qt, kt: (b * nq + qt[p], qo + h)),
            pl.BlockSpec((tq, dh), lambda b, h, p, qt, kt: (b * nq + kt[p], ko + h)),
            pl.BlockSpec((tq, dh), lambda b, h, p, qt, kt: (b * nq + kt[p], vo + h)),
            pl.BlockSpec((1, 1, tq), lambda b, h, p, qt, kt: (b * FOX_HEADS + h, 0, kt[p])),
        ],
        out_specs=pl.BlockSpec((tq, dh), lambda b, h, p, qt, kt: (b * nq + qt[p], h)),
        scratch_shapes=[pltpu.VMEM((tq, LANES), F32)] * 4 + [
                        pltpu.VMEM((tq, dh), F32), pltpu.VMEM((2, grp, tq), F32),
                        pltpu.VMEM((2, grp, tq), BF16)],
    )
    return pl.pallas_call(
        functools.partial(_fox_kernel, tq=tq, rb=FOX_STRIP, grp=grp),
        grid_spec=grid_spec,
        out_shape=jax.ShapeDtypeStruct((t, FOX_HEADS * dh), BF16),
        compiler_params=_params(("arbitrary", "arbitrary", "arbitrary")),
        name="fox_attention",
    )(qt, kt, proj, proj, proj, c_row)


def _merge_kernel(a_ref, f_ref, ga_ref, gf_ref, wa_ref, wf_ref, out_ref):
    ya = jnp.dot(a_ref[...], wa_ref[...], preferred_element_type=F32)
    yf = jnp.dot(f_ref[...], wf_ref[...], preferred_element_type=F32)
    out_ref[...] = (jax.nn.sigmoid(ga_ref[...].astype(F32)) * ya
                    + jax.nn.sigmoid(gf_ref[...].astype(F32)) * yf).astype(out_ref.dtype)


def _merge(o_gla, o_fox, proj, w_gla, w_fox, *, offs, tm, tn):
    m, kdim = o_gla.shape
    n = w_gla.shape[1]
    go, fo = offs[0] // tn, offs[1] // tn
    return pl.pallas_call(
        _merge_kernel,
        grid=(m // tm, n // tn),
        in_specs=[
            pl.BlockSpec((tm, kdim), lambda i, j: (i, 0)),
            pl.BlockSpec((tm, o_fox.shape[1]), lambda i, j: (i, 0)),
            pl.BlockSpec((tm, tn), lambda i, j: (i, go + j)),
            pl.BlockSpec((tm, tn), lambda i, j: (i, fo + j)),
            pl.BlockSpec((kdim, tn), lambda i, j: (0, j)),
            pl.BlockSpec((o_fox.shape[1], tn), lambda i, j: (0, j)),
        ],
        out_specs=pl.BlockSpec((tm, tn), lambda i, j: (i, j)),
        out_shape=jax.ShapeDtypeStruct((m, n), BF16),
        compiler_params=_params(("arbitrary", "arbitrary")),
        name="gated_merge",
    )(o_gla, o_fox, proj, proj, w_gla, w_fox)


def _resid_matmul_kernel(a_ref, w_ref, res_ref, out_ref):
    out_ref[...] = res_ref[...] + jnp.dot(a_ref[...], w_ref[...], preferred_element_type=F32)


def _resid_matmul(a, w, res, *, tm, tn):
    m, kdim = a.shape
    n = w.shape[1]
    return pl.pallas_call(
        _resid_matmul_kernel,
        grid=(m // tm, n // tn),
        in_specs=[pl.BlockSpec((tm, kdim), lambda i, j: (i, 0)),
                  pl.BlockSpec((kdim, tn), lambda i, j: (0, j)),
                  pl.BlockSpec((tm, tn), lambda i, j: (i, j))],
        out_specs=pl.BlockSpec((tm, tn), lambda i, j: (i, j)),
        out_shape=jax.ShapeDtypeStruct((m, n), F32),
        compiler_params=_params(("arbitrary", "arbitrary")),
        name="resid_matmul",
    )(a, w, res)


def _memory_kernel(x_ref, gn_ref, wq_ref, gq_ref, k_ref, v_ref, wo_ref, gm_ref, wr_ref, br_ref,
                   x2_ref, rl_ref, xsp_ref, *, dh):
    x1 = x_ref[...]
    hb = _rms(x1, gn_ref[...]).astype(BF16)
    q = jnp.dot(hb, wq_ref[...], preferred_element_type=F32)
    outs = []
    for h in range(MEM_HEADS):
        sl = slice(h * dh, (h + 1) * dh)
        qn = _rms(q[:, sl], gq_ref[...]).astype(BF16)
        s = lax.dot_general(qn, k_ref[:, sl], NT_DIMS, preferred_element_type=F32)
        s = s - jnp.max(s, axis=-1, keepdims=True)
        p = jnp.exp(s)
        p = p / jnp.sum(p, axis=-1, keepdims=True)
        outs.append(jnp.dot(p.astype(BF16), v_ref[:, sl], preferred_element_type=F32).astype(BF16))
    o = jnp.concatenate(outs, axis=-1)
    x2 = x1 + jnp.dot(o, wo_ref[...], preferred_element_type=F32)
    x2_ref[...] = x2
    xs = _rms(x2, gm_ref[...])
    rl_ref[...] = jnp.dot(xs.astype(BF16), wr_ref[...], preferred_element_type=F32) + br_ref[...]
    xsp_ref[...] = _pack_bf16_pairs(xs)


def _memory(x1, mem_norm_g, wq, gq, kv, wo, moe_norm_g, wr_pad, br_pad, *, batch, seq,
            mem_len, tm):
    t, d = x1.shape
    w = wq.shape[1]
    dh = w // MEM_HEADS
    nblk = seq // tm
    const = lambda i: (0, 0)
    return pl.pallas_call(
        functools.partial(_memory_kernel, dh=dh),
        grid=(t // tm,),
        in_specs=[
            pl.BlockSpec((tm, d), lambda i: (i, 0)),
            pl.BlockSpec((1, d), const),
            pl.BlockSpec((d, w), const),
            pl.BlockSpec((1, dh), const),
            pl.BlockSpec((mem_len, w), lambda i: (i // nblk, 0)),
            pl.BlockSpec((mem_len, w), lambda i: (i // nblk, 1)),
            pl.BlockSpec((w, d), const),
            pl.BlockSpec((1, d), const),
            pl.BlockSpec((d, LANES), const),
            pl.BlockSpec((1, LANES), const),
        ],
        out_specs=[pl.BlockSpec((tm, d), lambda i: (i, 0)),
                   pl.BlockSpec((tm, LANES), lambda i: (i, 0)),
                   pl.BlockSpec((tm, d // 2), lambda i: (i, 0))],
        out_shape=[jax.ShapeDtypeStruct((t, d), F32), jax.ShapeDtypeStruct((t, LANES), F32),
                   jax.ShapeDtypeStruct((t, d // 2), jnp.uint32)],
        compiler_params=_params(("arbitrary",)),
        name="memory_block",
    )(x1, mem_norm_g, wq, gq, kv, kv, wo, moe_norm_g, wr_pad, br_pad)


def _route_kernel(rl_ref, route_ref, counts_ref, carry_ref, *, tr, n_experts):
    @pl.when(pl.program_id(0) == 0)
    def _():
        carry_ref[...] = jnp.zeros_like(carry_ref)

    lane = lax.broadcasted_iota(jnp.int32, (tr, LANES), 1)
    lg = jnp.where(lane < n_experts, rl_ref[...], -jnp.inf)
    vals, hots = [], []
    for _ in range(TOP_K):
        mx = jnp.max(lg, axis=-1, keepdims=True)
        idx = jnp.min(jnp.where(lg == mx, lane, LANES), axis=-1, keepdims=True)
        hot = lane == idx
        vals.append(mx)
        hots.append(hot)
        lg = jnp.where(hot, -jnp.inf, lg)
    exps = [jnp.exp(v - vals[0]) for v in vals]
    denom = exps[0]
    for e in exps[1:]:
        denom = denom + e
    onehot = hots[0]
    for hsel in hots[1:]:
        onehot = jnp.logical_or(onehot, hsel)
    onehot_f = onehot.astype(F32)
    row = lax.broadcasted_iota(jnp.int32, (tr, tr), 0)
    col = lax.broadcasted_iota(jnp.int32, (tr, tr), 1)
    strict = (row > col).astype(BF16)
    before = jnp.dot(strict, onehot_f.astype(BF16), preferred_element_type=F32) + carry_ref[...]
    route = jnp.zeros((tr, LANES), F32)
    for k in range(TOP_K):
        idx_f = jnp.sum(jnp.where(hots[k], lane, 0), axis=-1, keepdims=True).astype(F32)
        rank = jnp.sum(jnp.where(hots[k], before, 0.0), axis=-1, keepdims=True)
        route = jnp.where(lane == k, idx_f, route)
        route = jnp.where(lane == TOP_K + k, exps[k] / denom, route)
        route = jnp.where(lane == 2 * TOP_K + k, rank, route)
    route_ref[...] = route
    total = carry_ref[...] + jnp.sum(onehot_f, axis=0, keepdims=True)
    carry_ref[...] = total
    counts_ref[...] = jnp.broadcast_to(total, counts_ref.shape)


def _route(rlogits, *, n_experts, tr):
    t = rlogits.shape[0]
    return pl.pallas_call(
        functools.partial(_route_kernel, tr=tr, n_experts=n_experts),
        grid=(t // tr,),
        in_specs=[pl.BlockSpec((tr, LANES), lambda i: (i, 0))],
        out_specs=[pl.BlockSpec((tr, LANES), lambda i: (i, 0)),
                   pl.BlockSpec((8, LANES), lambda i: (0, 0))],
        out_shape=[jax.ShapeDtypeStruct((t, LANES), F32), jax.ShapeDtypeStruct((8, LANES), F32)],
        scratch_shapes=[pltpu.VMEM((1, LANES), F32)],
        compiler_params=_params(("arbitrary",)),
        name="route_topk",
    )(rlogits)


def _gather_kernel(tok_ref, nused_ref, x_hbm, out_ref, buf, sem, *, rows):
    i = pl.program_id(0)
    nused = nused_ref[0]

    def issue(blk, slot):
        def body(r8, carry):
            for j in range(SUBLANES):
                tok = tok_ref[blk * rows + r8 * SUBLANES + j]
                pltpu.make_async_copy(
                    x_hbm.at[tok >> 3, pl.ds(tok & (SUBLANES - 1), 1)],
                    buf.at[slot, r8, pl.ds(j, 1)], sem.at[slot]).start()
            return carry
        lax.fori_loop(0, rows // SUBLANES, body, 0)

    @pl.when(i == 0)
    def _():
        issue(0, 0)

    @pl.when(i + 1 < nused)
    def _():
        issue(i + 1, (i + 1) % 2)

    @pl.when(i < nused)
    def _():
        slot = i % 2
        pltpu.make_async_copy(x_hbm.at[pl.ds(0, rows // SUBLANES)], buf.at[slot],
                              sem.at[slot]).wait()
        lo, hi = _unpack_bf16_pairs(buf[slot].reshape(rows, -1))
        half = lo.shape[1]
        out_ref[:, :half] = lo.astype(out_ref.dtype)
        out_ref[:, half:] = hi.astype(out_ref.dtype)

    @pl.when(i >= nused)
    def _():
        out_ref[...] = jnp.zeros_like(out_ref)


def _gather_rows(slot_tok, nused, xs_packed, *, rows):
    p = slot_tok.shape[0]
    half = xs_packed.shape[1]
    grid_spec = pltpu.PrefetchScalarGridSpec(
        num_scalar_prefetch=2,
        grid=(p // rows,),
        in_specs=[pl.BlockSpec(memory_space=pl.ANY)],
        out_specs=pl.BlockSpec((rows, 2 * half), lambda i, tok, nu: (i, 0)),
        scratch_shapes=[pltpu.VMEM((2, rows // SUBLANES, SUBLANES, half), jnp.uint32),
                        pltpu.SemaphoreType.DMA((2,))],
    )
    return pl.pallas_call(
        functools.partial(_gather_kernel, rows=rows),
        grid_spec=grid_spec,
        out_shape=jax.ShapeDtypeStruct((p, 2 * half), BF16),
        compiler_params=_params(("arbitrary",)),
        name="moe_gather",
    )(slot_tok, nused, xs_packed.reshape(-1, SUBLANES, half))


def _start_all(copies):
    for c in copies:
        c.start()


def _wait_all(copies):
    for c in copies:
        c.wait()


def _weight_pipeline(step, flag, tabs, copies_for, convert):
    exp_ref, tile_ref, gslot_ref, nexp_ref, ntile_ref = tabs

    @pl.when((flag & 2) != 0)
    def _():
        slot = gslot_ref[step]
        mine = copies_for(exp_ref[step], tile_ref[step], slot)

        @pl.when(step == 0)
        def _():
            _start_all(mine)

        _wait_all(mine)
        convert(slot)

        @pl.when((flag & 4) != 0)
        def _():
            _start_all(copies_for(nexp_ref[step], ntile_ref[step], 1 - slot))


def _expert_up_kernel(blk_ref, tile_ref, exp_ref, flag_ref, gslot_ref, nexp_ref, ntile_ref,
                      x_ref, w_hbm, bg_ref, bl_ref, out_ref, wbuf, wg_bf, wl_bf, sem, *, tf, nt):
    step = pl.program_id(0)
    flag = flag_ref[step]

    def copies_for(e, tile, slot):
        return [pltpu.make_async_copy(
            w_hbm.at[e, :, pl.ds(pl.multiple_of((half * nt + tile) * tf, tf), tf)],
            wbuf.at[slot, half], sem.at[slot, half]) for half in range(2)]

    def convert(slot):
        wg_bf[...] = wbuf[slot, 0].astype(BF16)
        wl_bf[...] = wbuf[slot, 1].astype(BF16)

    _weight_pipeline(step, flag, (exp_ref, tile_ref, gslot_ref, nexp_ref, ntile_ref), copies_for,
                     convert)

    @pl.when((flag & 1) != 0)
    def _():
        x = x_ref[...]
        glu = jnp.dot(x, wg_bf[...], preferred_element_type=F32) + bg_ref[...]
        lin = jnp.dot(x, wl_bf[...], preferred_element_type=F32) + bl_ref[...]
        glu = jnp.minimum(glu, SWIGLU_LIMIT)
        lin = jnp.clip(lin, -SWIGLU_LIMIT, SWIGLU_LIMIT)
        out_ref[...] = (glu * jax.nn.sigmoid(SWIGLU_ALPHA * glu) * (lin + 1.0)).astype(out_ref.dtype)

    @pl.when((flag & 1) == 0)
    def _():
        out_ref[...] = jnp.zeros_like(out_ref)


def _expert_up(tables, xg, w_up, b_up, *, tm, tf):
    p, d = xg.shape
    n_e, _, ff2 = w_up.shape
    ff = ff2 // 2
    nt = ff // tf
    nsteps = tables[0].shape[0]
    bias = lambda half: pl.BlockSpec((None, 1, tf),
                                     lambda s, blk, tile, ex, *_: (ex[s], 0, half * nt + tile[s]))
    grid_spec = pltpu.PrefetchScalarGridSpec(
        num_scalar_prefetch=len(tables),
        grid=(nsteps,),
        in_specs=[
            pl.BlockSpec((tm, d), lambda s, blk, *_: (blk[s], 0)),
            pl.BlockSpec(memory_space=pl.ANY),
            bias(0), bias(1),
        ],
        out_specs=pl.BlockSpec((tm, tf), lambda s, blk, tile, *_: (blk[s], tile[s])),
        scratch_shapes=[pltpu.VMEM((2, 2, d, tf), F32), pltpu.VMEM((d, tf), BF16),
                        pltpu.VMEM((d, tf), BF16), pltpu.SemaphoreType.DMA((2, 2))],
    )
    return pl.pallas_call(
        functools.partial(_expert_up_kernel, tf=tf, nt=nt),
        grid_spec=grid_spec,
        out_shape=jax.ShapeDtypeStruct((p, ff), BF16),
        compiler_params=_params(("arbitrary",), VMEM_LIMIT_MAX),
        name="expert_up",
    )(*tables, xg, w_up, b_up, b_up)


def _expert_down_kernel(blk_ref, tile_ref, exp_ref, flag_ref, gslot_ref, nexp_ref, ntile_ref,
                        a_ref, w_hbm, b_ref, out_ref, wbuf, w_bf, sem, *, tn):
    step = pl.program_id(0)
    flag = flag_ref[step]

    def copies_for(e, tile, slot):
        return [pltpu.make_async_copy(w_hbm.at[e, :, pl.ds(pl.multiple_of(tile * tn, tn), tn)],
                                      wbuf.at[slot], sem.at[slot])]

    def convert(slot):
        w_bf[...] = wbuf[slot].astype(BF16)

    _weight_pipeline(step, flag, (exp_ref, tile_ref, gslot_ref, nexp_ref, ntile_ref), copies_for,
                     convert)

    @pl.when((flag & 1) != 0)
    def _():
        y = jnp.dot(a_ref[...], w_bf[...], preferred_element_type=F32) + b_ref[...]
        out_ref[...] = _pack_bf16_pairs(y)

    @pl.when((flag & 1) == 0)
    def _():
        out_ref[...] = jnp.zeros_like(out_ref)


def _expert_down(tables, act, w_down, b_down, *, tm, tn):
    p, ff = act.shape
    d = w_down.shape[2]
    nsteps = tables[0].shape[0]
    grid_spec = pltpu.PrefetchScalarGridSpec(
        num_scalar_prefetch=len(tables),
        grid=(nsteps,),
        in_specs=[
            pl.BlockSpec((tm, ff), lambda s, blk, *_: (blk[s], 0)),
            pl.BlockSpec(memory_space=pl.ANY),
            pl.BlockSpec((None, 1, tn), lambda s, blk, tile, ex, *_: (ex[s], 0, tile[s])),
        ],
        out_specs=pl.BlockSpec((tm, tn // 2), lambda s, blk, tile, *_: (blk[s], tile[s])),
        scratch_shapes=[pltpu.VMEM((2, ff, tn), F32), pltpu.VMEM((ff, tn), BF16),
                        pltpu.SemaphoreType.DMA((2,))],
    )
    return pl.pallas_call(
        functools.partial(_expert_down_kernel, tn=tn),
        grid_spec=grid_spec,
        out_shape=jax.ShapeDtypeStruct((p, d // 2), jnp.uint32),
        compiler_params=_params(("arbitrary",)),
        name="expert_down",
    )(*tables, act, w_down, b_down)


def _expert_schedule(counts, *, tm, n_tiles, n_blocks):
    n_e = counts.shape[0]
    nb = (counts + tm - 1) // tm
    bend = jnp.cumsum(nb)
    bstart = bend - nb
    steps_e = nb * n_tiles
    cs = jnp.cumsum(steps_e)
    total = cs[-1]
    s = jnp.arange(n_blocks * n_tiles, dtype=jnp.int32)
    sc = jnp.minimum(s, total - 1)
    e = jnp.sum((cs[None, :] <= sc[:, None]).astype(jnp.int32), axis=1)
    e = jnp.minimum(e, n_e - 1)
    r = sc - (cs[e] - steps_e[e])
    nbe = jnp.maximum(nb[e], 1)
    valid = s < total
    n_unused = jnp.maximum(n_blocks - bend[-1], 1)
    u = jnp.maximum(s - total, 0)
    live_tile = r // nbe
    tile = jnp.where(valid, live_tile, u // n_unused)
    blk = jnp.where(valid, bstart[e] + r % nbe, bend[-1] + u % n_unused)
    first = jnp.logical_and(valid, r % nbe == 0)
    ids = jnp.where(nb > 0, jnp.arange(n_e, dtype=jnp.int32), n_e)
    later = jnp.concatenate([lax.cummin(ids, reverse=True)[1:], jnp.full((1,), n_e, jnp.int32)])
    wraps = live_tile + 1 >= n_tiles
    nexp = jnp.where(wraps, later[e], e)
    ntile = jnp.where(wraps, 0, live_tile + 1)
    flag = (valid.astype(jnp.int32) + 2 * first.astype(jnp.int32)
            + 4 * jnp.logical_and(first, nexp < n_e).astype(jnp.int32))
    gslot = (jnp.cumsum(first.astype(jnp.int32)) - 1) % 2
    i32 = lambda a: a.astype(jnp.int32)
    return (i32(blk), i32(tile), i32(e), flag, i32(gslot), i32(jnp.minimum(nexp, n_e - 1)),
            i32(ntile))


def _combine_kernel(slot_ref, x_ref, route_ref, y_hbm, out_ref, buf, sem, *, tc, tn):
    i = pl.program_id(0)
    n = pl.num_programs(0)

    def issue(blk, bslot):
        def body(r, carry):
            for k in range(TOP_K):
                src = slot_ref[(blk * tc + r) * TOP_K + k]
                pltpu.make_async_copy(y_hbm.at[pl.ds(src, 1)], buf.at[bslot, k, pl.ds(r, 1)],
                                      sem.at[bslot]).start()
            return carry
        lax.fori_loop(0, tc, body, 0, unroll=4)

    @pl.when(i == 0)
    def _():
        issue(0, 0)

    @pl.when(i + 1 < n)
    def _():
        issue(i + 1, (i + 1) % 2)

    bslot = i % 2
    for k in range(TOP_K):
        pltpu.make_async_copy(y_hbm.at[pl.ds(0, tc)], buf.at[bslot, k], sem.at[bslot]).wait()
    gates = [route_ref[:, TOP_K + k:TOP_K + k + 1] for k in range(TOP_K)]
    hw = tn // 2
    for j in range(out_ref.shape[1] // tn):
        parts = [_unpack_bf16_pairs(buf[bslot, k, :, j * hw:(j + 1) * hw]) for k in range(TOP_K)]
        for half in range(2):
            cols = slice(j * tn + half * hw, j * tn + (half + 1) * hw)
            acc = x_ref[:, cols]
            for k in range(TOP_K):
                acc = acc + gates[k] * parts[k][half]
            out_ref[:, cols] = acc


def _combine(slot_flat, x2, route, y, *, tc, tn):
    t, d = x2.shape
    grid_spec = pltpu.PrefetchScalarGridSpec(
        num_scalar_prefetch=1,
        grid=(t // tc,),
        in_specs=[pl.BlockSpec((tc, d), lambda i, sl: (i, 0)),
                  pl.BlockSpec((tc, LANES), lambda i, sl: (i, 0)),
                  pl.BlockSpec(memory_space=pl.ANY)],
        out_specs=pl.BlockSpec((tc, d), lambda i, sl: (i, 0)),
        scratch_shapes=[pltpu.VMEM((2, TOP_K, tc, d // 2), jnp.uint32),
                        pltpu.SemaphoreType.DMA((2,))],
    )
    return pl.pallas_call(
        functools.partial(_combine_kernel, tc=tc, tn=tn),
        grid_spec=grid_spec,
        out_shape=jax.ShapeDtypeStruct((t, d), F32),
        compiler_params=_params(("arbitrary",)),
        name="moe_combine",
    )(slot_flat, x2, route, y)


def _pad_lanes(a, offset=0):
    width = a.shape[-1]
    return jnp.pad(a, [(0, 0)] * (a.ndim - 1) + [(offset, LANES - offset - width)])


def _layer(x2d, mem2d, lw, w_in_stack, layer, *, batch, seq, mem_len):
    t, d = x2d.shape
    gla_qk = lw["gla_alpha_up"].shape[1]
    gla_rank = lw["gla_alpha_up"].shape[0]
    gla_v = lw["w_branch_gla"].shape[0]
    fox_w = lw["w_branch_fox"].shape[0]
    dk, dv = gla_qk // GLA_HEADS, gla_v // GLA_HEADS
    dh = fox_w // FOX_HEADS
    mem_w = lw["mem_w_q"].shape[1]
    mem_dh = mem_w // MEM_HEADS
    n_experts = lw["w_router"].shape[1]
    ff = lw["w_expert_down"].shape[1]

    sizes = (gla_qk, gla_qk, gla_v, gla_v, gla_rank, fox_w, fox_w, fox_w, FOX_HEADS, d, d)
    starts = [0]
    for sz in sizes:
        starts.append(starts[-1] + sz)
    big_ids = (0, 1, 2, 3, 5, 6, 7, 9, 10)
    off = {}
    acc = 0
    for i in big_ids:
        off[i] = acc
        acc += sizes[i]
    n_big = acc
    w_big, w_small = _w_in_pack(
        w_in_stack, layer, spans=tuple((starts[i], sizes[i], off[i]) for i in big_ids),
        pieces=((starts[4], sizes[4], 0), (starts[8], sizes[8], sizes[4])), tr=TILE["w_pack_rows"])
    tn1 = TILE["proj_tn"]
    colscale = jnp.ones((1, n_big), F32)
    colscale = colscale.at[:, off[5]:off[5] + fox_w].set(
        jnp.tile(lw["fox_q_norm_g"] * (dh ** -0.5 * LOG2E), FOX_HEADS)[None])
    colscale = colscale.at[:, off[6]:off[6] + fox_w].set(jnp.tile(lw["fox_k_norm_g"], FOX_HEADS)[None])
    proj, small = _norm_matmul(
        x2d, lw["mix_norm_g"][None], w_big, colscale, w_small,
        tm=TILE["proj_tm"], tn=tn1, norm_lo=off[5] // tn1, norm_hi=(off[6] + fox_w) // tn1)

    alpha_up_pad = jnp.pad(lw["gla_alpha_up"], ((0, LANES - gla_rank), (0, 0))).astype(BF16)
    o_gla = _gla(proj, small, alpha_up_pad, lw["gla_alpha_bias"][None], lw["gla_out_norm_g"][None],
                 batch=batch, seq=seq, dk=dk, dv=dv, offs=(off[0], off[1], off[2], off[3]),
                 rows=TILE["gla_rows"], hps=TILE["gla_heads_per_step"])

    c = _fcum(small, _pad_lanes(lw["fox_f_bias"][None], gla_rank), batch=batch, seq=seq,
              tb=TILE["fcum_rows"])
    c_row = c[:, gla_rank:gla_rank + FOX_HEADS].reshape(batch, seq, FOX_HEADS)
    c_row = c_row.transpose(0, 2, 1).reshape(batch * FOX_HEADS, 1, seq) * LOG2E
    o_fox = _fox(proj, c_row, batch=batch, seq=seq, dh=dh, offs=(off[5], off[6], off[7]),
                 tq=min(seq, TILE["fox_tq"]))

    merged = _merge(o_gla, o_fox, proj, lw["w_branch_gla"].astype(BF16),
                    lw["w_branch_fox"].astype(BF16), offs=(off[9], off[10]),
                    tm=TILE["merge_tm"], tn=TILE["merge_tn"])
    x1 = _resid_matmul(merged, lw["w_out"].astype(BF16), x2d, tm=TILE["resid_tm"],
                       tn=TILE["resid_tn"])

    kv_scale = jnp.concatenate([jnp.tile(lw["mem_k_norm_g"], MEM_HEADS),
                                jnp.ones((mem_w,), F32)])[None]
    kv = _norm_matmul(mem2d, lw["mem_kv_norm_g"][None], lw["mem_w_kv"].astype(BF16), kv_scale, None,
                      tm=mem2d.shape[0], tn=mem_w, norm_lo=0, norm_hi=1)
    x2, rlogits, xs_packed = _memory(
        x1, lw["mem_norm_g"][None], lw["mem_w_q"].astype(BF16),
        (lw["mem_q_norm_g"] * (mem_dh ** -0.5))[None], kv,
        lw["mem_w_o"].astype(BF16), lw["moe_norm_g"][None],
        _pad_lanes(lw["w_router"]).astype(BF16), _pad_lanes(lw["b_router"][None]),
        batch=batch, seq=seq, mem_len=mem_len, tm=TILE["mem_tm"])

    route, counts_f = _route(rlogits, n_experts=n_experts, tr=TILE["route_rows"])
    tm_e = TILE["expert_tm"]
    a_total = t * TOP_K
    n_blocks = a_total // tm_e + n_experts
    p_rows = n_blocks * tm_e
    counts = counts_f[0, :n_experts].astype(jnp.int32)
    padded = (counts + tm_e - 1) // tm_e * tm_e
    pstart = jnp.cumsum(padded) - padded
    idx = route[:, :TOP_K].astype(jnp.int32)
    rank = route[:, 2 * TOP_K:3 * TOP_K].astype(jnp.int32)
    hit = idx[:, :, None] == jnp.arange(n_experts, dtype=jnp.int32)
    slot = (jnp.sum(jnp.where(hit, pstart, 0), axis=-1) + rank).reshape(a_total)
    slot_tok = (jnp.arange(p_rows, dtype=jnp.int32) % t).at[slot].set(
        jnp.arange(a_total, dtype=jnp.int32) // TOP_K, unique_indices=True,
        mode="promise_in_bounds")

    tf = TILE["expert_tf"]
    g_rows = TILE["gather_rows"]
    nused = ((jnp.sum(padded) + g_rows - 1) // g_rows).astype(jnp.int32).reshape(1)
    xg = _gather_rows(slot_tok, nused, xs_packed, rows=g_rows)
    up_tables = _expert_schedule(counts, tm=tm_e, n_tiles=ff // tf, n_blocks=n_blocks)
    act = _expert_up(up_tables, xg, lw["w_expert_up"], lw["b_expert_up"][:, None, :], tm=tm_e, tf=tf)
    tn_d = TILE["expert_tn"]
    down_tables = _expert_schedule(counts, tm=tm_e, n_tiles=d // tn_d, n_blocks=n_blocks)
    y = _expert_down(down_tables, act, lw["w_expert_down"], lw["b_expert_down"][:, None, :],
                     tm=tm_e, tn=tn_d)
    return _combine(slot, x2, route, y, tc=TILE["combine_rows"], tn=tn_d)


def kernel(x, mem, mix_norm_g, w_in, gla_alpha_up, gla_alpha_bias, gla_out_norm_g, fox_f_bias, fox_q_norm_g, fox_k_norm_g, w_branch_gla, w_branch_fox, w_out, mem_norm_g, mem_kv_norm_g, mem_w_q, mem_w_kv, mem_q_norm_g, mem_k_norm_g, mem_w_o, moe_norm_g, w_router, b_router, w_expert_up, b_expert_up, w_expert_down, b_expert_down):
    weights = dict(
        mix_norm_g=mix_norm_g, w_in=w_in, gla_alpha_up=gla_alpha_up, gla_alpha_bias=gla_alpha_bias,
        gla_out_norm_g=gla_out_norm_g, fox_f_bias=fox_f_bias, fox_q_norm_g=fox_q_norm_g,
        fox_k_norm_g=fox_k_norm_g, w_branch_gla=w_branch_gla, w_branch_fox=w_branch_fox, w_out=w_out,
        mem_norm_g=mem_norm_g, mem_kv_norm_g=mem_kv_norm_g, mem_w_q=mem_w_q, mem_w_kv=mem_w_kv,
        mem_q_norm_g=mem_q_norm_g, mem_k_norm_g=mem_k_norm_g, mem_w_o=mem_w_o, moe_norm_g=moe_norm_g,
        w_router=w_router, b_router=b_router, w_expert_up=w_expert_up, b_expert_up=b_expert_up,
        w_expert_down=w_expert_down, b_expert_down=b_expert_down)
    batch, seq, d = x.shape
    mem_len = mem.shape[1]
    x2d = x.reshape(batch * seq, d)
    mem2d = mem.reshape(batch * mem_len, d)
    for layer in range(mix_norm_g.shape[0]):
        lw = {name: w[layer] for name, w in weights.items() if name != "w_in"}
        x2d = _layer(x2d, mem2d, lw, w_in, layer, batch=batch, seq=seq, mem_len=mem_len)
    return x2d.reshape(batch, seq, d)
```
